```python
import jax, jax.numpy as jnp
from jax import lax
import numpy as np

D_MODEL = 1024
BATCH = 8
SEQ = 4096
DEPTH = 2

CHUNK = 64
Q_BLOCK = 128
HEAD_DIM = 64
SB_HEADS = 4
MLA_HEADS = 8
MLA_Q_RANK = 256
MLA_KV_RANK = 128
MLA_NOPE = 64
MLA_ROPE = 32
MLA_V = 64
ROPE_THETA = 10000.0
CK_HEADS = 4
CK_LEFT_CHUNKS = 8
REL_CLIP = 128

D_SB = SB_HEADS * HEAD_DIM
D_MLA = MLA_HEADS * MLA_V
D_CK = CK_HEADS * HEAD_DIM
D_MIX = D_SB + D_MLA + D_CK
IN_SIZES = (D_SB, D_SB, D_SB, MLA_Q_RANK, MLA_KV_RANK, MLA_ROPE, D_CK, D_CK, D_CK)
IN_COLS = sum(IN_SIZES)
IN_SPLITS = tuple(int(v) for v in np.cumsum(IN_SIZES)[:-1])
GROUP_SPLITS = (D_SB, D_SB + D_MLA)

D_FF = 2816
N_EXPERTS = 8
TOP_K = 2
N_DENSE = (DEPTH + 1) // 2
N_MOE = DEPTH // 2
EPS = 1e-6
NEG = -1e30

kernel_name = "hybrid_streaming_sb_mla_chunkattn_moe"


def _rms_norm(x, g):
    xf = x.astype(jnp.float32)
    y = xf * lax.rsqrt(jnp.mean(xf * xf, axis=-1, keepdims=True) + EPS)
    return (y * g.astype(jnp.float32)).astype(x.dtype)


def _rope(x, cos, sin):
    half = x.shape[-1] // 2
    x1, x2 = x[..., :half], x[..., half:]
    return jnp.concatenate([x1 * cos - x2 * sin, x2 * cos + x1 * sin], axis=-1)


def _stick_breaking(q, k, v):
    S, Dh = q.shape[1], q.shape[-1]
    scale = Dh ** -0.5
    outs = []
    for i in range(S // Q_BLOCK):
        q0, q1 = i * Q_BLOCK, (i + 1) * Q_BLOCK
        z = jnp.einsum('bqhd,bkhd->bhqk', q[:, q0:q1], k[:, :q1]).astype(jnp.float32) * scale
        tq = q0 + jnp.arange(Q_BLOCK)
        tk = jnp.arange(q1)
        strict = tk[None, :] < tq[:, None]
        log_1mb = jnp.where(strict, jax.nn.log_sigmoid(-z), 0.0)
        tail = lax.cumsum(log_1mb, axis=3, reverse=True) - log_1mb
        w = jnp.where(strict, jnp.exp(jax.nn.log_sigmoid(z) + tail), 0.0)
        outs.append(jnp.einsum('bhqk,bkhd->bqhd', w.astype(v.dtype), v[:, :q1]))
    return jnp.concatenate(outs, axis=1)


def _chunk_causal_attention(q, k, v, scale):
    S = q.shape[1]
    outs = []
    for i in range(S // Q_BLOCK):
        q0, q1 = i * Q_BLOCK, (i + 1) * Q_BLOCK
        s = jnp.einsum('bqhd,bkhd->bhqk', q[:, q0:q1], k[:, :q1]).astype(jnp.float32) * scale
        tq = q0 + jnp.arange(Q_BLOCK)
        tk = jnp.arange(q1)
        allowed = (tk[None, :] // CHUNK) <= (tq[:, None] // CHUNK)
        p = jax.nn.softmax(jnp.where(allowed, s, NEG), axis=-1)
        outs.append(jnp.einsum('bhqk,bkhd->bqhd', p.astype(v.dtype), v[:, :q1]))
    return jnp.concatenate(outs, axis=1)


def _chunked_band_attention(q, k, v, rel_bias):
    B, S, H, Dh = q.shape
    NC = S // CHUNK
    W = CK_LEFT_CHUNKS + 1
    qc = q.reshape(B, NC, CHUNK, H, Dh)
    pad = ((0, 0), (CK_LEFT_CHUNKS * CHUNK, 0), (0, 0), (0, 0))
    kp = jnp.pad(k, pad).reshape(B, NC + CK_LEFT_CHUNKS, CHUNK, H, Dh)
    vp = jnp.pad(v, pad).reshape(B, NC + CK_LEFT_CHUNKS, CHUNK, H, Dh)
    band = jnp.arange(NC)[:, None] + jnp.arange(W)[None, :]
    kb = kp[:, band].reshape(B, NC, W * CHUNK, H, Dh)
    vb = vp[:, band].reshape(B, NC, W * CHUNK, H, Dh)
    s = jnp.einsum('bnqhd,bnkhd->bnhqk', qc, kb).astype(jnp.float32) * (Dh ** -0.5)
    q_pos = jnp.arange(CHUNK) + CK_LEFT_CHUNKS * CHUNK
    k_pos = jnp.arange(W * CHUNK)
    rel = jnp.clip(q_pos[:, None] - k_pos[None, :], -REL_CLIP, REL_CLIP) + REL_CLIP
    bias = rel_bias.astype(jnp.float32)[:, rel]
    key_chunk = jnp.arange(NC)[:, None] + (k_pos // CHUNK)[None, :] - CK_LEFT_CHUNKS
    valid = (key_chunk >= 0)[None, :, None, None, :]
    p = jax.nn.softmax(jnp.where(valid, s + bias[None, None], NEG), axis=-1)
    o = jnp.einsum('bnhqk,bnkhd->bnqhd', p.astype(v.dtype), vb)
    return o.reshape(B, S, H, Dh)


def _mixer(h, cos, sin, w_in, mla_q_norm, w_q_up, mla_kv_norm, w_kv_up, mla_q_qknorm,
           mla_k_qknorm, ck_q_qknorm, ck_k_qknorm, ck_rel_bias, group_out_norm, w_out):
    B, S, _ = h.shape
    proj = h @ w_in
    sb_q, sb_k, sb_v, cq, ckv, k_pe, ck_q, ck_k, ck_v = jnp.split(proj, IN_SPLITS, axis=-1)

    hd = lambda t, n: t.reshape(B, S, n, -1)
    o_sb = _stick_breaking(hd(sb_q, SB_HEADS), hd(sb_k, SB_HEADS), hd(sb_v, SB_HEADS))

    q = (_rms_norm(cq, mla_q_norm) @ w_q_up).reshape(B, S, MLA_HEADS, MLA_NOPE + MLA_ROPE)
    q = jnp.concatenate([q[..., :MLA_NOPE], _rope(q[..., MLA_NOPE:], cos, sin)], axis=-1)
    kv = (_rms_norm(ckv, mla_kv_norm) @ w_kv_up).reshape(B, S, MLA_HEADS, MLA_NOPE + MLA_V)
    k_pe = _rope(k_pe[:, :, None, :], cos, sin)
    k = jnp.concatenate([kv[..., :MLA_NOPE],
                         jnp.broadcast_to(k_pe, (B, S, MLA_HEADS, MLA_ROPE))], axis=-1)
    q = _rms_norm(q, mla_q_qknorm)
    k = _rms_norm(k, mla_k_qknorm)
    o_mla = _chunk_causal_attention(q, k, kv[..., MLA_NOPE:], (MLA_NOPE + MLA_ROPE) ** -0.5)

    o_ck = _chunked_band_attention(_rms_norm(hd(ck_q, CK_HEADS), ck_q_qknorm),
                                   _rms_norm(hd(ck_k, CK_HEADS), ck_k_qknorm),
                                   hd(ck_v, CK_HEADS), ck_rel_bias)

    g_sb, g_mla, g_ck = jnp.split(group_out_norm, GROUP_SPLITS)
    merged = jnp.concatenate([_rms_norm(o_sb.reshape(B, S, D_SB), g_sb),
                              _rms_norm(o_mla.reshape(B, S, D_MLA), g_mla),
                              _rms_norm(o_ck.reshape(B, S, D_CK), g_ck)], axis=-1)
    return merged @ w_out


def _swiglu(h, w_gate, w_up, w_down):
    return (jax.nn.silu(h @ w_gate) * (h @ w_up)) @ w_down


def _moe(h, w_router, w_gate, w_up, w_down):
    logits = (h @ w_router).astype(jnp.float32)
    top_val, top_idx = lax.top_k(logits, TOP_K)
    top_w = jax.nn.softmax(top_val, axis=-1)
    gates = jnp.sum(jax.nn.one_hot(top_idx, N_EXPERTS, dtype=jnp.float32) * top_w[..., None], axis=-2)
    gates = gates.astype(h.dtype)
    out = jnp.zeros_like(h)
    for e in range(N_EXPERTS):
        out = out + gates[..., e:e + 1] * _swiglu(h, w_gate[e], w_up[e], w_down[e])
    return out


def setup_inputs(seed: int = 0) -> dict:
    key = jax.random.key(seed)
    ks = iter(jax.random.split(key, 32))
    f32 = jnp.float32

    def w(shape, fan_in, mult=1.0):
        return jax.random.normal(next(ks), shape, f32) * (mult * fan_in ** -0.5)

    def gain(shape):
        return 1.0 + 0.1 * jax.random.normal(next(ks), shape, f32)

    x = jax.random.normal(next(ks), (BATCH, SEQ, D_MODEL), f32)
    c = jax.random.normal(next(ks), (BATCH, D_MODEL), f32)
    offsets = jax.random.randint(next(ks), (BATCH, 1), 0, 4096, dtype=jnp.int32)
    positions = (offsets + jnp.arange(SEQ, dtype=jnp.int32)[None, :]).astype(jnp.int32)
    return {
        "x": x,
        "c": c,
        "positions": positions,
        "ada_w": w((DEPTH, D_MODEL, 6 * D_MODEL), D_MODEL, 0.5),
        "ada_b": 0.02 * jax.random.normal(next(ks), (DEPTH, 6 * D_MODEL), f32),
        "norm_mix": gain((DEPTH, D_MODEL)),
        "norm_ffn": gain((DEPTH, D_MODEL)),
        "w_in": w((DEPTH, D_MODEL, IN_COLS), D_MODEL),
        "mla_q_norm": gain((DEPTH, MLA_Q_RANK)),
        "w_q_up": w((DEPTH, MLA_Q_RANK, MLA_HEADS * (MLA_NOPE + MLA_ROPE)), MLA_Q_RANK),
        "mla_kv_norm": gain((DEPTH, MLA_KV_RANK)),
        "w_kv_up": w((DEPTH, MLA_KV_RANK, MLA_HEADS * (MLA_NOPE + MLA_V)), MLA_KV_RANK),
        "mla_q_qknorm": gain((DEPTH, MLA_NOPE + MLA_ROPE)),
        "mla_k_qknorm": gain((DEPTH, MLA_NOPE + MLA_ROPE)),
        "ck_q_qknorm": gain((DEPTH, HEAD_DIM)),
        "ck_k_qknorm": gain((DEPTH, HEAD_DIM)),
        "ck_rel_bias": 0.5 * jax.random.normal(next(ks), (DEPTH, CK_HEADS, 2 * REL_CLIP + 1), f32),
        "group_out_norm": gain((DEPTH, D_MIX)),
        "w_out": w((DEPTH, D_MIX, D_MODEL), D_MIX),
        "ffn_w_gate": w((N_DENSE, D_MODEL, D_FF), D_MODEL),
        "ffn_w_up": w((N_DENSE, D_MODEL, D_FF), D_MODEL),
        "ffn_w_down": w((N_DENSE, D_FF, D_MODEL), D_FF),
        "moe_router": w((N_MOE, D_MODEL, N_EXPERTS), D_MODEL),
        "moe_w_gate": w((N_MOE, N_EXPERTS, D_MODEL, D_FF), D_MODEL),
        "moe_w_up": w((N_MOE, N_EXPERTS, D_MODEL, D_FF), D_MODEL),
        "moe_w_down": w((N_MOE, N_EXPERTS, D_FF, D_MODEL), D_FF),
    }


def reference(x, c, positions, ada_w, ada_b, norm_mix, norm_ffn, w_in, mla_q_norm, w_q_up,
              mla_kv_norm, w_kv_up, mla_q_qknorm, mla_k_qknorm, ck_q_qknorm, ck_k_qknorm,
              ck_rel_bias, group_out_norm, w_out, ffn_w_gate, ffn_w_up, ffn_w_down,
              moe_router, moe_w_gate, moe_w_up, moe_w_down):
    inv_freq = ROPE_THETA ** (-jnp.arange(0, MLA_ROPE, 2, dtype=jnp.float32) / MLA_ROPE)
    ang = positions.astype(jnp.float32)[..., None] * inv_freq
    cos = jnp.cos(ang)[:, :, None, :].astype(x.dtype)
    sin = jnp.sin(ang)[:, :, None, :].astype(x.dtype)
    c_act = jax.nn.silu(c)
    for layer in range(DEPTH):
        mod = c_act @ ada_w[layer] + ada_b[layer]
        sh1, sc1, g1, sh2, sc2, g2 = [m[:, None, :] for m in jnp.split(mod, 6, axis=-1)]
        h = _rms_norm(x, norm_mix[layer]) * (1.0 + sc1) + sh1
        x = x + g1 * _mixer(h, cos, sin, w_in[layer], mla_q_norm[layer], w_q_up[layer],
                            mla_kv_norm[layer], w_kv_up[layer], mla_q_qknorm[layer],
                            mla_k_qknorm[layer], ck_q_qknorm[layer], ck_k_qknorm[layer],
                            ck_rel_bias[layer], group_out_norm[layer], w_out[layer])
        h = _rms_norm(x, norm_ffn[layer]) * (1.0 + sc2) + sh2
        i = layer // 2
        if layer % 2 == 0:
            y = _swiglu(h, ffn_w_gate[i], ffn_w_up[i], ffn_w_down[i])
        else:
            y = _moe(h, moe_router[i], moe_w_gate[i], moe_w_up[i], moe_w_down[i])
        x = x + g2 * y
    return x
```

```python
import functools

import numpy as np
import jax
import jax.numpy as jnp
from jax import lax
from jax.experimental import pallas as pl
from jax.experimental.pallas import tpu as pltpu

F32 = jnp.float32
BF16 = jnp.bfloat16

D_MODEL = 1024
CHUNK = 64
HEAD_DIM = 64
SB_HEADS = 4
MLA_HEADS = 8
MLA_Q_RANK = 256
MLA_KV_RANK = 128
MLA_NOPE = 64
MLA_ROPE = 32
MLA_V = 64
MLA_QK = MLA_NOPE + MLA_ROPE
ROPE_THETA = 10000.0
CK_HEADS = 4
CK_LEFT_CHUNKS = 8
CK_WINDOW = (CK_LEFT_CHUNKS + 1) * CHUNK
REL_CLIP = 128
D_SB = SB_HEADS * HEAD_DIM
D_MLA = MLA_HEADS * MLA_V
D_CK = CK_HEADS * HEAD_DIM
D_FF = 2816
N_EXPERTS = 8
EPS = 1e-6
NEG = -1e30

LANES = 128
VMEM_LIMIT = 56 * 1024 * 1024

C_SBQ, C_SBK, C_SBV = 0, 256, 512
C_CQ = 768
C_CKV = 1024
C_KPA = 1152
C_KPB = 1280
C_CKQ, C_CKK, C_CKVV = 1408, 1664, 1920
IN_EXT = 2176

NT_DIMS = (((1,), (1,)), ((), ()))


def _cparams(sem, vmem=VMEM_LIMIT):
    return pltpu.CompilerParams(dimension_semantics=sem, vmem_limit_bytes=vmem)


def _ada_kernel(c_ref, w_ref, b_ref, o_ref):
    c = c_ref[...]
    ca = c * jax.nn.sigmoid(c)
    o_ref[...] = jnp.dot(ca, w_ref[...], preferred_element_type=F32,
                         precision=lax.Precision.HIGHEST) + b_ref[...]


def _ada(c, ada_w, ada_b):
    depth, d, n = ada_w.shape
    b = c.shape[0]
    tn = 1536
    return pl.pallas_call(
        _ada_kernel,
        out_shape=jax.ShapeDtypeStruct((depth, b, n), F32),
        grid=(depth, n // tn),
        in_specs=[pl.BlockSpec((b, d), lambda l, j: (0, 0)),
                  pl.BlockSpec((None, d, tn), lambda l, j: (l, 0, j)),
                  pl.BlockSpec((None, 1, tn), lambda l, j: (l, 0, j))],
        out_specs=pl.BlockSpec((None, b, tn), lambda l, j: (l, 0, j)),
        compiler_params=_cparams(("parallel", "parallel")),
        name="ada_mod",
    )(c, ada_w, ada_b.reshape(depth, 1, n))


def _rope_kernel(pos_ref, invf_ref, c_ref, s_ref):
    ang = pos_ref[...] * invf_ref[...]
    lane = lax.broadcasted_iota(jnp.int32, ang.shape, 1)
    rope = (lane >= MLA_NOPE) & (lane < MLA_QK)
    c_ref[...] = jnp.where(rope, jnp.cos(ang), jnp.where(lane < MLA_NOPE, 1.0, 0.0))
    s_ref[...] = jnp.where(rope, jnp.sin(ang), 0.0)


def _rope_tables(positions):
    t = positions.size
    tm = 1024
    inv_freq = ROPE_THETA ** (-jnp.arange(0, MLA_ROPE, 2, dtype=F32) / MLA_ROPE)
    invf = jnp.zeros((1, LANES), F32)
    invf = invf.at[0, MLA_NOPE:MLA_NOPE + 16].set(inv_freq)
    invf = invf.at[0, MLA_NOPE + 16:MLA_QK].set(inv_freq)
    pos_b = jnp.broadcast_to(positions.reshape(t, 1).astype(F32), (t, LANES))
    return pl.pallas_call(
        _rope_kernel,
        out_shape=(jax.ShapeDtypeStruct((t, LANES), F32),
                   jax.ShapeDtypeStruct((t, LANES), F32)),
        grid=(t // tm,),
        in_specs=[pl.BlockSpec((tm, LANES), lambda i: (i, 0)),
                  pl.BlockSpec((1, LANES), lambda i: (0, 0))],
        out_specs=(pl.BlockSpec((tm, LANES), lambda i: (i, 0)),
                   pl.BlockSpec((tm, LANES), lambda i: (i, 0))),
        compiler_params=_cparams(("parallel",)),
        name="rope_tables",
    )(pos_b, invf)


def _rms(x, g, n):
    ms = jnp.sum(x * x, axis=-1, keepdims=True) * (1.0 / n)
    return x * lax.rsqrt(ms + EPS) * g


def _pair_rms(blk, g, scale):
    lane = lax.broadcasted_iota(jnp.int32, blk.shape, 1)
    lo = lane < HEAD_DIM
    sq = blk * blk
    s_all = jnp.sum(sq, axis=-1, keepdims=True)
    s_lo = jnp.sum(jnp.where(lo, sq, 0.0), axis=-1, keepdims=True)
    ms = jnp.where(lo, s_lo, s_all - s_lo) * (1.0 / HEAD_DIM)
    return blk * lax.rsqrt(ms + EPS) * (g * scale)


def _inproj_kernel(x_ref, mod_ref, nw_ref, win_ref, rc_ref, rs_ref, qn_ref, wq_ref,
                   kvn_ref, wk_ref, wv_ref, gq_ref, gk_ref, gcq_ref, gck_ref,
                   sbq_ref, sbk_ref, sbv_ref, mq_ref, mk_ref, mv_ref,
                   cq_ref, ck_ref, cv_ref):
    x = x_ref[...]
    xn = _rms(x, nw_ref[...], D_MODEL)
    h = (xn * (1.0 + mod_ref[1:2, :]) + mod_ref[0:1, :]).astype(BF16)
    proj = jnp.dot(h, win_ref[...], preferred_element_type=F32)

    sbq_ref[...] = (proj[:, C_SBQ:C_SBQ + D_SB] * (HEAD_DIM ** -0.5)).astype(BF16)
    sbk_ref[...] = proj[:, C_SBK:C_SBK + D_SB].astype(BF16)
    sbv_ref[...] = proj[:, C_SBV:C_SBV + D_SB].astype(BF16)

    rc = rc_ref[...]
    rs = rs_ref[...]
    cqn = _rms(proj[:, C_CQ:C_CQ + MLA_Q_RANK], qn_ref[...], MLA_Q_RANK).astype(BF16)
    qq = jnp.dot(cqn, wq_ref[...], preferred_element_type=F32)
    gq = gq_ref[...] * (MLA_QK ** -0.5)
    nq = MLA_HEADS * LANES
    for hh in range(MLA_HEADS):
        a = qq[:, hh * LANES:(hh + 1) * LANES]
        b = qq[:, nq + hh * LANES:nq + (hh + 1) * LANES]
        qh = a * rc + b * rs
        mq_ref[:, hh * LANES:(hh + 1) * LANES] = _rms(qh, gq, MLA_QK).astype(BF16)

    ckvn = _rms(proj[:, C_CKV:C_CKV + MLA_KV_RANK], kvn_ref[...], MLA_KV_RANK).astype(BF16)
    kn = jnp.dot(ckvn, wk_ref[...], preferred_element_type=F32)
    mv_ref[...] = jnp.dot(ckvn, wv_ref[...], preferred_element_type=F32).astype(BF16)
    kpe = proj[:, C_KPA:C_KPA + LANES] * rc + proj[:, C_KPB:C_KPB + LANES] * rs
    gk = gk_ref[...]
    for hh in range(MLA_HEADS):
        kh = kn[:, hh * LANES:(hh + 1) * LANES] + kpe
        mk_ref[:, hh * LANES:(hh + 1) * LANES] = _rms(kh, gk, MLA_QK).astype(BF16)

    gcq = gcq_ref[...]
    gck = gck_ref[...]
    for p in range(D_CK // LANES):
        qb = proj[:, C_CKQ + p * LANES:C_CKQ + (p + 1) * LANES]
        kb = proj[:, C_CKK + p * LANES:C_CKK + (p + 1) * LANES]
        cq_ref[:, p * LANES:(p + 1) * LANES] = _pair_rms(qb, gcq, HEAD_DIM ** -0.5).astype(BF16)
        ck_ref[:, p * LANES:(p + 1) * LANES] = _pair_rms(kb, gck, 1.0).astype(BF16)
    cv_ref[...] = proj[:, C_CKVV:C_CKVV + D_CK].astype(BF16)


def _inproj(x2, mod_l, nw, win, rc, rs, qn, wq, kvn, wk, wv, gq, gk, gcq, gck, seq):
    t = x2.shape[0]
    tm = 512
    tpb = seq // tm
    full = lambda a: pl.BlockSpec(a.shape, lambda i: (0,) * a.ndim)
    row = lambda w: pl.BlockSpec((tm, w), lambda i: (i, 0))
    outs = [D_SB, D_SB, D_SB, MLA_HEADS * LANES, MLA_HEADS * LANES, D_MLA, D_CK, D_CK, D_CK]
    return pl.pallas_call(
        _inproj_kernel,
        out_shape=tuple(jax.ShapeDtypeStruct((t, w), BF16) for w in outs),
        grid=(t // tm,),
        in_specs=[row(D_MODEL),
                  pl.BlockSpec((None, 6, D_MODEL), lambda i: (i // tpb, 0, 0)),
                  full(nw), full(win), row(LANES), row(LANES), full(qn), full(wq),
                  full(kvn), full(wk), full(wv), full(gq), full(gk), full(gcq), full(gck)],
        out_specs=tuple(row(w) for w in outs),
        compiler_params=_cparams(("parallel",)),
        name="inproj",
    )(x2, mod_l, nw, win, rc, rs, qn, wq, kvn, wk, wv, gq, gk, gcq, gck)


def _sb_kernel(q_ref, k_ref, v_ref, o_ref, *, tq, tk):
    i = pl.program_id(2)
    q = q_ref[...]
    lane = lax.broadcasted_iota(jnp.int32, (tq, LANES), 1)
    lo = lane < HEAD_DIM
    zq = jnp.zeros_like(q)
    row = lax.broadcasted_iota(jnp.int32, (tq, tk), 0)
    col = lax.broadcasted_iota(jnp.int32, (tq, tk), 1)
    strict = col < row
    ur = lax.broadcasted_iota(jnp.int32, (tk, tk), 0)
    uc = lax.broadcasted_iota(jnp.int32, (tk, tk), 1)
    upper = jnp.where(ur > uc, 1.0, 0.0).astype(BF16)

    def block(j, qh, acc, car, diag):
        r0 = pl.multiple_of(j * tk, tk)
        kb = k_ref[pl.ds(r0, tk), :]
        vb = v_ref[pl.ds(r0, tk), :]
        z = lax.dot_general(qh, kb, NT_DIMS, preferred_element_type=F32)
        l1 = jnp.minimum(-z, 0.0) - jnp.log(1.0 + jnp.exp(-jnp.abs(z)))
        lm = jnp.where(strict, l1, 0.0) if diag else l1
        tail = jnp.dot(lm.astype(BF16), upper, preferred_element_type=F32) + car
        w = jnp.exp(z + l1 + tail)
        if diag:
            w = jnp.where(strict, w, 0.0)
        acc = acc + jnp.dot(w.astype(BF16), vb, preferred_element_type=F32)
        car = car + jnp.sum(lm, axis=-1, keepdims=True)
        return acc, car

    outs = []
    for hh in range(2):
        qh = jnp.where(lo, q, zq) if hh == 0 else jnp.where(lo, zq, q)
        acc, car = block(i, qh, jnp.zeros((tq, LANES), F32), jnp.zeros((tq, 1), F32), True)

        def body(jj, carry, qh=qh):
            return block(i - 1 - jj, qh, carry[0], carry[1], False)

        acc, car = lax.fori_loop(0, i, body, (acc, car))
        outs.append(acc)
    o_ref[...] = jnp.where(lo, outs[0], outs[1]).astype(o_ref.dtype)


def _sb_attention(q, k, v):
    b, s, d = q.shape
    tq = tk = 256
    kern = functools.partial(_sb_kernel, tq=tq, tk=tk)
    return pl.pallas_call(
        kern,
        out_shape=jax.ShapeDtypeStruct((b, s, d), BF16),
        grid=(b, d // LANES, s // tq),
        in_specs=[pl.BlockSpec((None, tq, LANES), lambda bi, p, i: (bi, i, p)),
                  pl.BlockSpec((None, s, LANES), lambda bi, p, i: (bi, 0, p)),
                  pl.BlockSpec((None, s, LANES), lambda bi, p, i: (bi, 0, p))],
        out_specs=pl.BlockSpec((None, tq, LANES), lambda bi, p, i: (bi, i, p)),
        compiler_params=_cparams(("parallel", "parallel", "arbitrary")),
        name="sb_attn",
    )(q, k, v)


def _mla_kernel(q_ref, k_ref, v_ref, o_ref, *, tq, tk):
    i = pl.program_id(2)
    lane = lax.broadcasted_iota(jnp.int32, (tq, LANES), 1)
    lo = lane < MLA_V
    row = lax.broadcasted_iota(jnp.int32, (tq, tk), 0)
    col = lax.broadcasted_iota(jnp.int32, (tq, tk), 1)
    shift = CHUNK.bit_length() - 1
    allowed = jnp.right_shift(col, shift) <= jnp.right_shift(row, shift)

    def block(j, hh, q, m, l, acc, diag):
        r0 = pl.multiple_of(j * tk, tk)
        kb = k_ref[pl.ds(r0, tk), hh * LANES:(hh + 1) * LANES]
        vb = v_ref[pl.ds(r0, tk), :]
        s = lax.dot_general(q, kb, NT_DIMS, preferred_element_type=F32)
        if diag:
            s = jnp.where(allowed, s, NEG)
        m_new = jnp.maximum(m, jnp.max(s, axis=-1, keepdims=True))
        alpha = jnp.exp(m - m_new)
        p = jnp.exp(s - m_new)
        l = alpha * l + jnp.sum(p, axis=-1, keepdims=True)
        acc = alpha * acc + jnp.dot(p.astype(BF16), vb, preferred_element_type=F32)
        return m_new, l, acc

    outs = []
    for hh in range(2):
        q = q_ref[:, hh * LANES:(hh + 1) * LANES]
        m, l, acc = block(i, hh, q, jnp.full((tq, 1), NEG, F32), jnp.zeros((tq, 1), F32),
                          jnp.zeros((tq, LANES), F32), True)

        def body(jj, carry, hh=hh, q=q):
            return block(i - 1 - jj, hh, q, carry[0], carry[1], carry[2], False)

        m, l, acc = lax.fori_loop(0, i, body, (m, l, acc))
        outs.append(acc / l)
    o_ref[...] = jnp.where(lo, outs[0], outs[1]).astype(o_ref.dtype)


def _mla_attention(q, k, v):
    b, s, _ = q.shape
    tq = tk = 256
    kern = functools.partial(_mla_kernel, tq=tq, tk=tk)
    return pl.pallas_call(
        kern,
        out_shape=jax.ShapeDtypeStruct((b, s, D_MLA), BF16),
        grid=(b, MLA_HEADS // 2, s // tq),
        in_specs=[pl.BlockSpec((None, tq, 2 * LANES), lambda bi, p, i: (bi, i, p)),
                  pl.BlockSpec((None, s, 2 * LANES), lambda bi, p, i: (bi, 0, p)),
                  pl.BlockSpec((None, s, LANES), lambda bi, p, i: (bi, 0, p))],
        out_specs=pl.BlockSpec((None, tq, LANES), lambda bi, p, i: (bi, i, p)),
        compiler_params=_cparams(("parallel", "parallel", "arbitrary")),
        name="mla_attn",
    )(q, k, v)


def _ck_kernel(q_ref, k_ref, v_ref, bias_ref, o_ref, *, n_chunks):
    lane = lax.broadcasted_iota(jnp.int32, (CHUNK, LANES), 1)
    lo = lane < HEAD_DIM
    kpos = lax.broadcasted_iota(jnp.int32, (CHUNK, CK_WINDOW), 1)

    def body(c, carry):
        r0 = pl.multiple_of(c * CHUNK, CHUNK)
        q = q_ref[pl.ds(r0, CHUNK), :]
        kw = k_ref[pl.ds(r0, CK_WINDOW), :]
        vw = v_ref[pl.ds(r0, CK_WINDOW), :]
        valid = kpos >= (CK_LEFT_CHUNKS - c) * CHUNK
        zq = jnp.zeros_like(q)
        outs = []
        for hh in range(2):
            qh = jnp.where(lo, q, zq) if hh == 0 else jnp.where(lo, zq, q)
            s = lax.dot_general(qh, kw, NT_DIMS, preferred_element_type=F32) + bias_ref[hh]
            s = jnp.where(valid, s, NEG)
            m = jnp.max(s, axis=-1, keepdims=True)
            p = jnp.exp(s - m)
            l = jnp.sum(p, axis=-1, keepdims=True)
            outs.append(jnp.dot(p.astype(BF16), vw, preferred_element_type=F32) / l)
        o_ref[pl.ds(r0, CHUNK), :] = jnp.where(lo, outs[0], outs[1]).astype(o_ref.dtype)
        return carry

    lax.fori_loop(0, n_chunks, body, 0)


def _ck_attention(q, kpad, vpad, bias):
    b, s, d = q.shape
    sp = kpad.shape[1]
    kern = functools.partial(_ck_kernel, n_chunks=s // CHUNK)
    return pl.pallas_call(
        kern,
        out_shape=jax.ShapeDtypeStruct((b, s, d), BF16),
        grid=(b, d // LANES),
        in_specs=[pl.BlockSpec((None, s, LANES), lambda bi, p: (bi, 0, p)),
                  pl.BlockSpec((None, sp, LANES), lambda bi, p: (bi, 0, p)),
                  pl.BlockSpec((None, sp, LANES), lambda bi, p: (bi, 0, p)),
                  pl.BlockSpec((2, CHUNK, CK_WINDOW), lambda bi, p: (p, 0, 0))],
        out_specs=pl.BlockSpec((None, s, LANES), lambda bi, p: (bi, 0, p)),
        compiler_params=_cparams(("parallel", "parallel")),
        name="ck_attn",
    )(q, kpad, vpad, bias)


def _outproj_kernel(osb_ref, omla_ref, ock_ref, x_ref, mod_ref, gn_ref, wout_ref, nffn_ref,
                    *rest, moe):
    if moe:
        wr_ref, xo_ref, h2_ref, gates_ref = rest
    else:
        xo_ref, h2_ref = rest
    gn = gn_ref[...]

    def gnorm(o_ref, c0, w):
        return _rms(o_ref[...].astype(F32), gn[:, c0:c0 + w], w).astype(BF16)

    merged = jnp.concatenate([gnorm(osb_ref, 0, D_SB), gnorm(omla_ref, D_SB, D_MLA),
                              gnorm(ock_ref, D_SB + D_MLA, D_CK)], axis=-1)
    y = jnp.dot(merged, wout_ref[...], preferred_element_type=F32)
    xn = x_ref[...] + mod_ref[2:3, :] * y
    xo_ref[...] = xn
    h2 = _rms(xn, nffn_ref[...], D_MODEL) * (1.0 + mod_ref[4:5, :]) + mod_ref[3:4, :]
    h2_ref[...] = h2.astype(BF16)
    if moe:
        logits = jnp.dot(h2, wr_ref[...], preferred_element_type=F32,
                         precision=lax.Precision.HIGHEST)
        lane = lax.broadcasted_iota(jnp.int32, logits.shape, 1).astype(F32)
        logits = jnp.where(lane < N_EXPERTS, logits, -jnp.inf)
        m1 = jnp.max(logits, axis=-1, keepdims=True)
        i1 = jnp.min(jnp.where(logits == m1, lane, float(LANES)), axis=-1, keepdims=True)
        rest_l = jnp.where(lane == i1, -jnp.inf, logits)
        m2 = jnp.max(rest_l, axis=-1, keepdims=True)
        i2 = jnp.min(jnp.where(rest_l == m2, lane, float(LANES)), axis=-1, keepdims=True)
        e2 = jnp.exp(m2 - m1)
        w1 = 1.0 / (1.0 + e2)
        w2 = e2 / (1.0 + e2)
        gates_ref[...] = jnp.where(lane == i1, w1, 0.0) + jnp.where(lane == i2, w2, 0.0)


def _outproj(osb, omla, ock, x2, mod_l, gn, wout, nffn, wr, seq):
    t = x2.shape[0]
    tm = 512
    tpb = seq // tm
    moe = wr is not None
    full = lambda a: pl.BlockSpec(a.shape, lambda i: (0,) * a.ndim)
    row = lambda w: pl.BlockSpec((tm, w), lambda i: (i, 0))
    ins = [osb, omla, ock, x2, mod_l, gn, wout, nffn]
    in_specs = [row(D_SB), row(D_MLA), row(D_CK), row(D_MODEL),
                pl.BlockSpec((None, 6, D_MODEL), lambda i: (i // tpb, 0, 0)),
                full(gn), full(wout), full(nffn)]
    out_shape = [jax.ShapeDtypeStruct((t, D_MODEL), F32), jax.ShapeDtypeStruct((t, D_MODEL), BF16)]
    out_specs = [row(D_MODEL), row(D_MODEL)]
    if moe:
        ins.append(wr)
        in_specs.append(full(wr))
        out_shape.append(jax.ShapeDtypeStruct((t, LANES), F32))
        out_specs.append(row(LANES))
    return pl.pallas_call(
        functools.partial(_outproj_kernel, moe=moe),
        out_shape=tuple(out_shape),
        grid=(t // tm,),
        in_specs=in_specs,
        out_specs=tuple(out_specs),
        compiler_params=_cparams(("parallel",)),
        name="outproj_moe" if moe else "outproj",
    )(*ins)


def _ffn_kernel(h_ref, x_ref, mod_ref, *rest, moe):
    if moe:
        gates_ref, wg_ref, wu_ref, wd_ref, o_ref, acc_ref = rest
        e = pl.program_id(1)
        f = pl.program_id(2)
        first = (e == 0) & (f == 0)
        last = (e == pl.num_programs(1) - 1) & (f == pl.num_programs(2) - 1)
    else:
        wg_ref, wu_ref, wd_ref, o_ref, acc_ref = rest
        f = pl.program_id(1)
        first = f == 0
        last = f == pl.num_programs(1) - 1

    @pl.when(first)
    def _():
        acc_ref[...] = jnp.zeros_like(acc_ref)

    h = h_ref[...]
    g = jnp.dot(h, wg_ref[...], preferred_element_type=F32)
    u = jnp.dot(h, wu_ref[...], preferred_element_type=F32)
    a = g * jax.nn.sigmoid(g) * u
    if moe:
        gates = gates_ref[...]
        lane = lax.broadcasted_iota(jnp.int32, gates.shape, 1)
        a = a * jnp.sum(jnp.where(lane == e, gates, 0.0), axis=-1, keepdims=True)
    acc_ref[...] += jnp.dot(a.astype(BF16), wd_ref[...], preferred_element_type=F32)

    @pl.when(last)
    def _():
        o_ref[...] = x_ref[...] + mod_ref[5:6, :] * acc_ref[...]


def _ffn(h2, x2, mod_l, gates, wg, wu, wd, seq):
    t = h2.shape[0]
    tm = 512
    tf = 1408
    tpb = seq // tm
    moe = gates is not None
    nf = D_FF // tf
    if moe:
        ne = wg.shape[0]
        grid = (t // tm, ne, nf)
        rmap = lambda i, e, f: (i, 0)
        in_specs = [pl.BlockSpec((tm, D_MODEL), rmap), pl.BlockSpec((tm, D_MODEL), rmap),
                    pl.BlockSpec((None, 6, D_MODEL), lambda i, e, f: (i // tpb, 0, 0)),
                    pl.BlockSpec((tm, LANES), rmap),
                    pl.BlockSpec((None, D_MODEL, tf), lambda i, e, f: (e, 0, f)),
                    pl.BlockSpec((None, D_MODEL, tf), lambda i, e, f: (e, 0, f)),
                    pl.BlockSpec((None, tf, D_MODEL), lambda i, e, f: (e, f, 0))]
        ins = (h2, x2, mod_l, gates, wg, wu, wd)
        sem = ("parallel", "arbitrary", "arbitrary")
    else:
        grid = (t // tm, nf)
        rmap = lambda i, f: (i, 0)
        in_specs = [pl.BlockSpec((tm, D_MODEL), rmap), pl.BlockSpec((tm, D_MODEL), rmap),
                    pl.BlockSpec((None, 6, D_MODEL), lambda i, f: (i // tpb, 0, 0)),
                    pl.BlockSpec((D_MODEL, tf), lambda i, f: (0, f)),
                    pl.BlockSpec((D_MODEL, tf), lambda i, f: (0, f)),
                    pl.BlockSpec((tf, D_MODEL), lambda i, f: (f, 0))]
        ins = (h2, x2, mod_l, wg, wu, wd)
        sem = ("parallel", "arbitrary")
    return pl.pallas_call(
        functools.partial(_ffn_kernel, moe=moe),
        out_shape=jax.ShapeDtypeStruct((t, D_MODEL), F32),
        grid=grid,
        in_specs=in_specs,
        out_specs=pl.BlockSpec((tm, D_MODEL), rmap),
        scratch_shapes=[pltpu.VMEM((tm, D_MODEL), F32)],
        compiler_params=_cparams(sem),
        name="moe_ffn" if moe else "dense_ffn",
    )(*ins)


def _rot_half_cols(w):
    half = w.shape[-1] // 2
    return jnp.concatenate([-w[..., half:], w[..., :half]], axis=-1)


def _pad_cols(w, left, total):
    return jnp.pad(w, ((0, 0), (left, total - left - w.shape[-1])))


def _layout_w_in(w_in):
    sizes = (D_SB, D_SB, D_SB, MLA_Q_RANK, MLA_KV_RANK, MLA_ROPE, D_CK, D_CK, D_CK)
    splits = [int(v) for v in np.cumsum(sizes)[:-1]]
    sbq, sbk, sbv, cq, ckv, kpe, ckq, ckk, ckv2 = jnp.split(w_in, splits, axis=-1)
    kpa = _pad_cols(kpe, MLA_NOPE, LANES)
    kpb = _pad_cols(_rot_half_cols(kpe), MLA_NOPE, LANES)
    return jnp.concatenate([sbq, sbk, sbv, cq, ckv, kpa, kpb, ckq, ckk, ckv2], axis=-1).astype(BF16)


def _layout_w_q_up(w):
    r = w.shape[0]
    w3 = w.reshape(r, MLA_HEADS, MLA_QK)
    nope, pe = w3[..., :MLA_NOPE], w3[..., MLA_NOPE:]
    zpad = jnp.zeros((r, MLA_HEADS, LANES - MLA_QK), w.dtype)
    a = jnp.concatenate([nope, pe, zpad], axis=-1).reshape(r, MLA_HEADS * LANES)
    b = jnp.concatenate([jnp.zeros_like(nope), _rot_half_cols(pe), zpad], axis=-1)
    return jnp.concatenate([a, b.reshape(r, MLA_HEADS * LANES)], axis=-1).astype(BF16)


def _layout_w_kv_up(w):
    r = w.shape[0]
    w3 = w.reshape(r, MLA_HEADS, MLA_NOPE + MLA_V)
    kn = jnp.pad(w3[..., :MLA_NOPE], ((0, 0), (0, 0), (0, LANES - MLA_NOPE)))
    return (kn.reshape(r, MLA_HEADS * LANES).astype(BF16),
            w3[..., MLA_NOPE:].reshape(r, D_MLA).astype(BF16))


def _ck_bias_table(rel_bias):
    q_pos = np.arange(CHUNK) + CK_LEFT_CHUNKS * CHUNK
    k_pos = np.arange(CK_WINDOW)
    rel = np.clip(q_pos[:, None] - k_pos[None, :], -REL_CLIP, REL_CLIP) + REL_CLIP
    return rel_bias.astype(F32)[:, rel]


def kernel(x, c, positions, ada_w, ada_b, norm_mix, norm_ffn, w_in, mla_q_norm, w_q_up, mla_kv_norm, w_kv_up, mla_q_qknorm, mla_k_qknorm, ck_q_qknorm, ck_k_qknorm, ck_rel_bias, group_out_norm, w_out, ffn_w_gate, ffn_w_up, ffn_w_down, moe_router, moe_w_gate, moe_w_up, moe_w_down):
    bsz, seq, d = x.shape
    depth = ada_w.shape[0]
    t = bsz * seq
    x2 = x.reshape(t, d)

    mod = _ada(c, ada_w, ada_b).reshape(depth, bsz, 6, d)
    rc, rs = _rope_tables(positions)
    pad128 = lambda g: jnp.pad(g, (0, LANES - g.shape[0])).reshape(1, LANES)
    pair = lambda g: jnp.concatenate([g, g]).reshape(1, LANES)

    for layer in range(depth):
        win = _layout_w_in(w_in[layer])
        wq = _layout_w_q_up(w_q_up[layer])
        wk, wv = _layout_w_kv_up(w_kv_up[layer])
        sbq, sbk, sbv, mq, mk, mv, cq, ck, cv = _inproj(
            x2, mod[layer], norm_mix[layer].reshape(1, d), win, rc, rs,
            mla_q_norm[layer].reshape(1, -1), wq, mla_kv_norm[layer].reshape(1, -1), wk, wv,
            pad128(mla_q_qknorm[layer]), pad128(mla_k_qknorm[layer]),
            pair(ck_q_qknorm[layer]), pair(ck_k_qknorm[layer]), seq)

        r3 = lambda a: a.reshape(bsz, seq, a.shape[-1])
        o_sb = _sb_attention(r3(sbq), r3(sbk), r3(sbv))
        o_mla = _mla_attention(r3(mq), r3(mk), r3(mv))
        lpad = ((0, 0), (CK_LEFT_CHUNKS * CHUNK, 0), (0, 0))
        o_ck = _ck_attention(r3(cq), jnp.pad(r3(ck), lpad), jnp.pad(r3(cv), lpad),
                             _ck_bias_table(ck_rel_bias[layer]))

        i = layer // 2
        is_moe = layer % 2 == 1
        wr = None
        if is_moe:
            wr = jnp.pad(moe_router[i], ((0, 0), (0, LANES - N_EXPERTS)))
        res = _outproj(o_sb.reshape(t, -1), o_mla.reshape(t, -1), o_ck.reshape(t, -1), x2,
                       mod[layer], group_out_norm[layer].reshape(1, -1),
                       w_out[layer].astype(BF16), norm_ffn[layer].reshape(1, d), wr, seq)
        if is_moe:
            x2, h2, gates = res
            x2 = _ffn(h2, x2, mod[layer], gates, moe_w_gate[i].astype(BF16),
                      moe_w_up[i].astype(BF16), moe_w_down[i].astype(BF16), seq)
        else:
            x2, h2 = res
            x2 = _ffn(h2, x2, mod[layer], None, ffn_w_gate[i].astype(BF16),
                      ffn_w_up[i].astype(BF16), ffn_w_down[i].astype(BF16), seq)
    return x2.reshape(bsz, seq, d)
```

```python
import functools

import numpy as np
import jax
import jax.numpy as jnp
from jax import lax
from jax.experimental import pallas as pl
from jax.experimental.pallas import tpu as pltpu

F32 = jnp.float32
BF16 = jnp.bfloat16

D_MODEL = 1024
CHUNK = 64
HEAD_DIM = 64
SB_HEADS = 4
MLA_HEADS = 8
MLA_Q_RANK = 256
MLA_KV_RANK = 128
MLA_NOPE = 64
MLA_ROPE = 32
MLA_V = 64
MLA_QK = MLA_NOPE + MLA_ROPE
ROPE_THETA = 10000.0
CK_HEADS = 4
CK_LEFT_CHUNKS = 8
CK_WINDOW = (CK_LEFT_CHUNKS + 1) * CHUNK
REL_CLIP = 128
D_SB = SB_HEADS * HEAD_DIM
D_MLA = MLA_HEADS * MLA_V
D_CK = CK_HEADS * HEAD_DIM
D_FF = 2816
N_EXPERTS = 8
EPS = 1e-6
NEG = -1e30

LANES = 128
VMEM_LIMIT = 56 * 1024 * 1024

C_SBQ, C_SBK = 0, 256
C_CQ = 512
C_CKV = 768
C_KPA = 896
C_KPB = 1024
C_CKQ, C_CKK, C_CKVV = 1152, 1408, 1664
IN_EXT = 1920

NT_DIMS = (((1,), (1,)), ((), ()))
TN_DIMS = (((0,), (0,)), ((), ()))
TKV = 256
LOG2E = 1.4426950408889634


def _cparams(sem, vmem=VMEM_LIMIT):
    return pltpu.CompilerParams(dimension_semantics=sem, vmem_limit_bytes=vmem)


def _ada_kernel(c_ref, w_ref, b_ref, o_ref):
    c = c_ref[...]
    ca = c * jax.nn.sigmoid(c)
    o_ref[...] = jnp.dot(ca, w_ref[...], preferred_element_type=F32,
                         precision=lax.Precision.HIGHEST) + b_ref[...]


def _ada(c, ada_w, ada_b):
    depth, d, n = ada_w.shape
    b = c.shape[0]
    tn = 1536
    return pl.pallas_call(
        _ada_kernel,
        out_shape=jax.ShapeDtypeStruct((depth, b, n), F32),
        grid=(depth, n // tn),
        in_specs=[pl.BlockSpec((b, d), lambda l, j: (0, 0)),
                  pl.BlockSpec((None, d, tn), lambda l, j: (l, 0, j)),
                  pl.BlockSpec((None, 1, tn), lambda l, j: (l, 0, j))],
        out_specs=pl.BlockSpec((None, b, tn), lambda l, j: (l, 0, j)),
        compiler_params=_cparams(("parallel", "parallel")),
        name="ada_mod",
    )(c, ada_w, ada_b.reshape(depth, 1, n))


def _rope_kernel(pos_ref, invf_ref, c_ref, s_ref):
    ang = pos_ref[...] * invf_ref[...]
    lane = lax.broadcasted_iota(jnp.int32, ang.shape, 1)
    rope = (lane >= MLA_NOPE) & (lane < MLA_QK)
    c_ref[...] = jnp.where(rope, jnp.cos(ang), jnp.where(lane < MLA_NOPE, 1.0, 0.0))
    s_ref[...] = jnp.where(rope, jnp.sin(ang), 0.0)


def _rope_tables(positions):
    t = positions.size
    tm = 1024
    inv_freq = ROPE_THETA ** (-jnp.arange(0, MLA_ROPE, 2, dtype=F32) / MLA_ROPE)
    invf = jnp.zeros((1, LANES), F32)
    invf = invf.at[0, MLA_NOPE:MLA_NOPE + 16].set(inv_freq)
    invf = invf.at[0, MLA_NOPE + 16:MLA_QK].set(inv_freq)
    pos_b = jnp.broadcast_to(positions.reshape(t, 1).astype(F32), (t, LANES))
    return pl.pallas_call(
        _rope_kernel,
        out_shape=(jax.ShapeDtypeStruct((t, LANES), F32),
                   jax.ShapeDtypeStruct((t, LANES), F32)),
        grid=(t // tm,),
        in_specs=[pl.BlockSpec((tm, LANES), lambda i: (i, 0)),
                  pl.BlockSpec((1, LANES), lambda i: (0, 0))],
        out_specs=(pl.BlockSpec((tm, LANES), lambda i: (i, 0)),
                   pl.BlockSpec((tm, LANES), lambda i: (i, 0))),
        compiler_params=_cparams(("parallel",)),
        name="rope_tables",
    )(pos_b, invf)


def _rms(x, g, n):
    ms = jnp.sum(x * x, axis=-1, keepdims=True) * (1.0 / n)
    return x * lax.rsqrt(ms + EPS) * g


def _pair_rms(blk, g, scale):
    lane = lax.broadcasted_iota(jnp.int32, blk.shape, 1)
    lo = lane < HEAD_DIM
    sq = blk * blk
    s_all = jnp.sum(sq, axis=-1, keepdims=True)
    s_lo = jnp.sum(jnp.where(lo, sq, 0.0), axis=-1, keepdims=True)
    ms = jnp.where(lo, s_lo, s_all - s_lo) * (1.0 / HEAD_DIM)
    return blk * lax.rsqrt(ms + EPS) * (g * scale)


def _store_kv_blocks(ref, xt):
    for cblk in range(xt.shape[1] // TKV):
        ref[cblk] = xt[:, cblk * TKV:(cblk + 1) * TKV].astype(ref.dtype)


def _inproj_kernel(x_ref, mod_ref, nw_ref, win_ref, wsbvt_ref, rc_ref, rs_ref, qn_ref, wq_ref,
                   kvn_ref, wk_ref, wvt_ref, gq_ref, gk_ref, gcq_ref, gck_ref,
                   sbq_ref, sbk_ref, sbvt_ref, mq_ref, mk_ref, mvt_ref,
                   cq_ref, ck_ref, cv_ref):
    x = x_ref[...]
    xn = _rms(x, nw_ref[...], D_MODEL)
    h = (xn * (1.0 + mod_ref[1:2, :]) + mod_ref[0:1, :]).astype(BF16)
    proj = jnp.dot(h, win_ref[...], preferred_element_type=F32)

    sbq_ref[...] = (proj[:, C_SBQ:C_SBQ + D_SB] * (HEAD_DIM ** -0.5 * LOG2E)).astype(BF16)
    sbk_ref[...] = proj[:, C_SBK:C_SBK + D_SB].astype(BF16)
    _store_kv_blocks(sbvt_ref, lax.dot_general(wsbvt_ref[...], h, NT_DIMS,
                                               preferred_element_type=F32))

    rc = rc_ref[...]
    rs = rs_ref[...]
    cqn = _rms(proj[:, C_CQ:C_CQ + MLA_Q_RANK], qn_ref[...], MLA_Q_RANK).astype(BF16)
    qq = jnp.dot(cqn, wq_ref[...], preferred_element_type=F32)
    gq = gq_ref[...] * (MLA_QK ** -0.5 * LOG2E)
    nq = MLA_HEADS * LANES
    for hh in range(MLA_HEADS):
        a = qq[:, hh * LANES:(hh + 1) * LANES]
        b = qq[:, nq + hh * LANES:nq + (hh + 1) * LANES]
        qh = a * rc + b * rs
        mq_ref[:, hh * LANES:(hh + 1) * LANES] = _rms(qh, gq, MLA_QK).astype(BF16)

    ckvn = _rms(proj[:, C_CKV:C_CKV + MLA_KV_RANK], kvn_ref[...], MLA_KV_RANK).astype(BF16)
    kn = jnp.dot(ckvn, wk_ref[...], preferred_element_type=F32)
    _store_kv_blocks(mvt_ref, lax.dot_general(wvt_ref[...], ckvn, NT_DIMS,
                                              preferred_element_type=F32))
    kpe = proj[:, C_KPA:C_KPA + LANES] * rc + proj[:, C_KPB:C_KPB + LANES] * rs
    gk = gk_ref[...]
    for hh in range(MLA_HEADS):
        kh = kn[:, hh * LANES:(hh + 1) * LANES] + kpe
        mk_ref[:, hh * LANES:(hh + 1) * LANES] = _rms(kh, gk, MLA_QK).astype(BF16)

    gcq = gcq_ref[...]
    gck = gck_ref[...]
    for p in range(D_CK // LANES):
        qb = proj[:, C_CKQ + p * LANES:C_CKQ + (p + 1) * LANES]
        kb = proj[:, C_CKK + p * LANES:C_CKK + (p + 1) * LANES]
        cq_ref[:, p * LANES:(p + 1) * LANES] = _pair_rms(qb, gcq, HEAD_DIM ** -0.5).astype(BF16)
        ck_ref[:, p * LANES:(p + 1) * LANES] = _pair_rms(kb, gck, 1.0).astype(BF16)
    cv_ref[...] = proj[:, C_CKVV:C_CKVV + D_CK].astype(BF16)


def _inproj(x2, mod_l, nw, win, wsbvt, rc, rs, qn, wq, kvn, wk, wvt, gq, gk, gcq, gck, seq):
    t = x2.shape[0]
    tm = 512
    tpb = seq // tm
    bsz = t // seq
    full = lambda a: pl.BlockSpec(a.shape, lambda i: (0,) * a.ndim)
    row = lambda w: pl.BlockSpec((tm, w), lambda i: (i, 0))
    rows = lambda w: (jax.ShapeDtypeStruct((t, w), BF16), row(w))
    kvt = lambda w: (jax.ShapeDtypeStruct((bsz, seq // TKV, w, TKV), BF16),
                     pl.BlockSpec((None, tm // TKV, w, TKV), lambda i: (i // tpb, i % tpb, 0, 0)))
    outs = [rows(D_SB), rows(D_SB), kvt(D_SB), rows(MLA_HEADS * LANES), rows(MLA_HEADS * LANES),
            kvt(D_MLA), rows(D_CK), rows(D_CK), rows(D_CK)]
    return pl.pallas_call(
        _inproj_kernel,
        out_shape=tuple(o[0] for o in outs),
        grid=(t // tm,),
        in_specs=[row(D_MODEL),
                  pl.BlockSpec((None, 6, D_MODEL), lambda i: (i // tpb, 0, 0)),
                  full(nw), full(win), full(wsbvt), row(LANES), row(LANES), full(qn), full(wq),
                  full(kvn), full(wk), full(wvt), full(gq), full(gk), full(gcq), full(gck)],
        out_specs=tuple(o[1] for o in outs),
        compiler_params=_cparams(("parallel",)),
        name="inproj",
    )(x2, mod_l, nw, win, wsbvt, rc, rs, qn, wq, kvn, wk, wvt, gq, gk, gcq, gck)


def _kv_iotas(tk, tq):
    return (lax.broadcasted_iota(jnp.int32, (tk, tq), 0),
            lax.broadcasted_iota(jnp.int32, (tk, tq), 1))


def _sb_kernel(q_ref, k_ref, vt_ref, o_ref, *, tq, heads):
    tk = TKV
    i = pl.program_id(2)
    lo = lax.broadcasted_iota(jnp.int32, (tq, LANES), 1) < HEAD_DIM
    qs = []
    for pb in range(heads // 2):
        q = q_ref[:, pb * LANES:(pb + 1) * LANES]
        zq = jnp.zeros_like(q)
        qs += [jnp.where(lo, q, zq), jnp.where(lo, zq, q)]
    krow, qcol = _kv_iotas(tk, tq)
    strict = krow < qcol
    ur, uc = _kv_iotas(tk, tk)
    later = jnp.where(uc > ur, 1.0, 0.0).astype(BF16)

    sign = jnp.uint32(0x80000000)

    def stage1(j, diag):
        r0 = pl.multiple_of(j * tk, tk)
        out = []
        for hh in range(heads):
            pb = hh // 2
            kb = k_ref[pl.ds(r0, tk), pb * LANES:(pb + 1) * LANES]
            y = lax.dot_general(kb, qs[hh], NT_DIMS, preferred_element_type=F32)
            neg_abs = pltpu.bitcast(pltpu.bitcast(y, jnp.uint32) | sign, F32)
            sp = jnp.maximum(y, 0.0) + jnp.log2(1.0 + jnp.exp2(neg_abs))
            spm = jnp.where(strict, sp, 0.0) if diag else sp
            tail = jnp.dot(later, spm.astype(BF16), preferred_element_type=F32)
            pre = y - sp - tail
            if diag:
                pre = jnp.where(strict, pre, NEG)
            out.append((pre, tail[0:1, :] + spm[0:1, :]))
        return tuple(out)

    def stage2(j, pres, carry):
        out = []
        for hh in range(heads):
            acc, car = carry[hh]
            pre, inc = pres[hh]
            w = jnp.exp2(pre - car)
            vb = vt_ref[j, hh * HEAD_DIM:(hh + 1) * HEAD_DIM, :]
            acc = acc + jnp.dot(vb, w.astype(BF16), preferred_element_type=F32)
            out.append((acc, car + inc))
        return tuple(out)

    init = tuple((jnp.zeros((HEAD_DIM, tq), F32), jnp.zeros((1, tq), F32))
                 for _ in range(heads))

    def body(jj, state):
        carry, pres = state
        j = i - 1 - jj
        ahead = stage1(j, False)
        return stage2(j + 1, pres, carry), ahead

    carry, pres = lax.fori_loop(0, i, body, (init, stage1(i, True)))
    carry = stage2(0, pres, carry)
    for hh in range(heads):
        o_ref[hh * HEAD_DIM:(hh + 1) * HEAD_DIM, :] = carry[hh][0].astype(o_ref.dtype)


def _sb_attention(q, k, vt):
    b, s, d = q.shape
    tq = TKV
    heads = 4
    w = heads * HEAD_DIM
    return pl.pallas_call(
        functools.partial(_sb_kernel, tq=tq, heads=heads),
        out_shape=jax.ShapeDtypeStruct((b, d, s), BF16),
        grid=(b, d // w, s // tq),
        in_specs=[pl.BlockSpec((None, tq, w), lambda bi, p, i: (bi, i, p)),
                  pl.BlockSpec((None, s, w), lambda bi, p, i: (bi, 0, p)),
                  pl.BlockSpec((None, s // TKV, w, TKV), lambda bi, p, i: (bi, 0, p, 0))],
        out_specs=pl.BlockSpec((None, w, tq), lambda bi, p, i: (bi, p, i)),
        compiler_params=_cparams(("parallel", "parallel", "arbitrary")),
        name="sb_attn",
    )(q, k, vt)


def _mla_kernel(q_ref, k_ref, vt_ref, o_ref, *, tq):
    tk = TKV
    i = pl.program_id(2)
    nsub = tq // tk
    krow, qcol = _kv_iotas(tk, tq)
    shift = CHUNK.bit_length() - 1
    qs = tuple(q_ref[:, hh * LANES:(hh + 1) * LANES] for hh in range(2))

    def scores(j):
        r0 = pl.multiple_of(j * tk, tk)
        return tuple(lax.dot_general(k_ref[pl.ds(r0, tk), hh * LANES:(hh + 1) * LANES], qs[hh],
                                     NT_DIMS, preferred_element_type=F32) for hh in range(2))

    def update(j, ss, carry, diag):
        if diag is not None:
            allowed = (jnp.right_shift(krow + diag * tk, shift) <= jnp.right_shift(qcol, shift))
        out = []
        for hh in range(2):
            m, l, acc = carry[hh]
            s = jnp.where(allowed, ss[hh], NEG) if diag is not None else ss[hh]
            m_new = jnp.maximum(m, jnp.max(s, axis=0, keepdims=True))
            alpha = jnp.exp2(m - m_new)
            p = jnp.exp2(s - m_new)
            l = alpha * l + jnp.sum(p, axis=0, keepdims=True)
            vb = vt_ref[j, hh * MLA_V:(hh + 1) * MLA_V, :]
            acc = alpha * acc + jnp.dot(vb, p.astype(BF16), preferred_element_type=F32)
            out.append((m_new, l, acc))
        return tuple(out)

    init = tuple((jnp.full((1, tq), NEG, F32), jnp.zeros((1, tq), F32),
                  jnp.zeros((MLA_V, tq), F32)) for _ in range(2))
    base = i * nsub
    carry = init
    ss = scores(base + nsub - 1)
    for d in range(nsub - 1, -1, -1):
        s_ahead = scores(jnp.maximum(base + d - 1, 0))
        carry = update(base + d, ss, carry, d)
        ss = s_ahead

    def body(jj, state):
        carry, ss = state
        j = base - 1 - jj
        s_ahead = scores(jnp.maximum(j - 1, 0))
        return update(j, ss, carry, None), s_ahead

    carry, _ = lax.fori_loop(0, base, body, (carry, ss))
    for hh in range(2):
        m, l, acc = carry[hh]
        o_ref[hh * MLA_V:(hh + 1) * MLA_V, :] = (acc / l).astype(o_ref.dtype)


def _mla_attention(q, k, vt):
    b, s, _ = q.shape
    tq = 2 * TKV
    return pl.pallas_call(
        functools.partial(_mla_kernel, tq=tq),
        out_shape=jax.ShapeDtypeStruct((b, D_MLA, s), BF16),
        grid=(b, MLA_HEADS // 2, s // tq),
        in_specs=[pl.BlockSpec((None, tq, 2 * LANES), lambda bi, p, i: (bi, i, p)),
                  pl.BlockSpec((None, s, 2 * LANES), lambda bi, p, i: (bi, 0, p)),
                  pl.BlockSpec((None, s // TKV, LANES, TKV), lambda bi, p, i: (bi, 0, p, 0))],
        out_specs=pl.BlockSpec((None, LANES, tq), lambda bi, p, i: (bi, p, i)),
        compiler_params=_cparams(("parallel", "parallel", "arbitrary")),
        name="mla_attn",
    )(q, k, vt)


def _ck_kernel(q_ref, k_ref, v_ref, bias_ref, o_ref, *, n_chunks):
    lane = lax.broadcasted_iota(jnp.int32, (CHUNK, LANES), 1)
    lo = lane < HEAD_DIM
    kpos = lax.broadcasted_iota(jnp.int32, (CHUNK, CK_WINDOW), 1)

    def body(c, carry):
        r0 = pl.multiple_of(c * CHUNK, CHUNK)
        q = q_ref[pl.ds(r0, CHUNK), :]
        kw = k_ref[pl.ds(r0, CK_WINDOW), :]
        vw = v_ref[pl.ds(r0, CK_WINDOW), :]
        valid = kpos >= (CK_LEFT_CHUNKS - c) * CHUNK
        zq = jnp.zeros_like(q)
        outs = []
        for hh in range(2):
            qh = jnp.where(lo, q, zq) if hh == 0 else jnp.where(lo, zq, q)
            s = lax.dot_general(qh, kw, NT_DIMS, preferred_element_type=F32) + bias_ref[hh]
            s = jnp.where(valid, s, NEG)
            m = jnp.max(s, axis=-1, keepdims=True)
            p = jnp.exp(s - m)
            l = jnp.sum(p, axis=-1, keepdims=True)
            outs.append(jnp.dot(p.astype(BF16), vw, preferred_element_type=F32) / l)
        o_ref[pl.ds(r0, CHUNK), :] = jnp.where(lo, outs[0], outs[1]).astype(o_ref.dtype)
        return carry

    lax.fori_loop(0, n_chunks, body, 0)


def _ck_attention(q, kpad, vpad, bias):
    b, s, d = q.shape
    sp = kpad.shape[1]
    kern = functools.partial(_ck_kernel, n_chunks=s // CHUNK)
    return pl.pallas_call(
        kern,
        out_shape=jax.ShapeDtypeStruct((b, s, d), BF16),
        grid=(b, d // LANES),
        in_specs=[pl.BlockSpec((None, s, LANES), lambda bi, p: (bi, 0, p)),
                  pl.BlockSpec((None, sp, LANES), lambda bi, p: (bi, 0, p)),
                  pl.BlockSpec((None, sp, LANES), lambda bi, p: (bi, 0, p)),
                  pl.BlockSpec((2, CHUNK, CK_WINDOW), lambda bi, p: (p, 0, 0))],
        out_specs=pl.BlockSpec((None, s, LANES), lambda bi, p: (bi, 0, p)),
        compiler_params=_cparams(("parallel", "parallel")),
        name="ck_attn",
    )(q, kpad, vpad, bias)


def _outproj_kernel(osb_ref, omla_ref, ock_ref, x_ref, mod_ref, gn_ref, wout_ref, nffn_ref,
                    *rest, moe):
    if moe:
        wr_ref, xo_ref, h2_ref, gates_ref = rest
    else:
        xo_ref, h2_ref = rest
    gn = gn_ref[...]

    def gnorm(o_ref, c0, w):
        return _rms(o_ref[...].astype(F32), gn[:, c0:c0 + w], w).astype(BF16)

    def gnorm_t(ot_ref, c0, w):
        ot = ot_ref[...].astype(F32)
        ms = jnp.sum(ot * ot, axis=0, keepdims=True) * (1.0 / w)
        return ((ot * lax.rsqrt(ms + EPS)).T * gn[:, c0:c0 + w]).astype(BF16)

    merged = jnp.concatenate([gnorm_t(osb_ref, 0, D_SB), gnorm_t(omla_ref, D_SB, D_MLA),
                              gnorm(ock_ref, D_SB + D_MLA, D_CK)], axis=-1)
    y = jnp.dot(merged, wout_ref[...], preferred_element_type=F32)
    xn = x_ref[...] + mod_ref[2:3, :] * y
    xo_ref[...] = xn
    h2 = _rms(xn, nffn_ref[...], D_MODEL) * (1.0 + mod_ref[4:5, :]) + mod_ref[3:4, :]
    h2_ref[...] = h2.astype(BF16)
    if moe:
        logits = jnp.dot(h2, wr_ref[...], preferred_element_type=F32,
                         precision=lax.Precision.HIGHEST)
        lane = lax.broadcasted_iota(jnp.int32, logits.shape, 1).astype(F32)
        logits = jnp.where(lane < N_EXPERTS, logits, -jnp.inf)
        m1 = jnp.max(logits, axis=-1, keepdims=True)
        i1 = jnp.min(jnp.where(logits == m1, lane, float(LANES)), axis=-1, keepdims=True)
        rest_l = jnp.where(lane == i1, -jnp.inf, logits)
        m2 = jnp.max(rest_l, axis=-1, keepdims=True)
        i2 = jnp.min(jnp.where(rest_l == m2, lane, float(LANES)), axis=-1, keepdims=True)
        e2 = jnp.exp(m2 - m1)
        w1 = 1.0 / (1.0 + e2)
        w2 = e2 / (1.0 + e2)
        gates_ref[...] = jnp.where(lane == i1, w1, 0.0) + jnp.where(lane == i2, w2, 0.0)


def _outproj(osb, omla, ock, x2, mod_l, gn, wout, nffn, wr, seq):
    t = x2.shape[0]
    tm = 512
    tpb = seq // tm
    moe = wr is not None
    full = lambda a: pl.BlockSpec(a.shape, lambda i: (0,) * a.ndim)
    row = lambda w: pl.BlockSpec((tm, w), lambda i: (i, 0))
    colt = lambda w: pl.BlockSpec((None, w, tm), lambda i: (i // tpb, 0, i % tpb))
    ins = [osb, omla, ock, x2, mod_l, gn, wout, nffn]
    in_specs = [colt(D_SB), colt(D_MLA), row(D_CK), row(D_MODEL),
                pl.BlockSpec((None, 6, D_MODEL), lambda i: (i // tpb, 0, 0)),
                full(gn), full(wout), full(nffn)]
    out_shape = [jax.ShapeDtypeStruct((t, D_MODEL), F32), jax.ShapeDtypeStruct((t, D_MODEL), BF16)]
    out_specs = [row(D_MODEL), row(D_MODEL)]
    if moe:
        ins.append(wr)
        in_specs.append(full(wr))
        out_shape.append(jax.ShapeDtypeStruct((t, LANES), F32))
        out_specs.append(row(LANES))
    return pl.pallas_call(
        functools.partial(_outproj_kernel, moe=moe),
        out_shape=tuple(out_shape),
        grid=(t // tm,),
        in_specs=in_specs,
        out_specs=tuple(out_specs),
        compiler_params=_cparams(("parallel",)),
        name="outproj_moe" if moe else "outproj",
    )(*ins)


def _ffn_kernel(h_ref, x_ref, mod_ref, *rest, moe):
    if moe:
        gates_ref, wg_ref, wu_ref, wd_ref, o_ref, acc_ref = rest
        e = pl.program_id(1)
        f = pl.program_id(2)
        first = (e == 0) & (f == 0)
        last = (e == pl.num_programs(1) - 1) & (f == pl.num_programs(2) - 1)
    else:
        wg_ref, wu_ref, wd_ref, o_ref, acc_ref = rest
        f = pl.program_id(1)
        first = f == 0
        last = f == pl.num_programs(1) - 1

    @pl.when(first)
    def _():
        acc_ref[...] = jnp.zeros_like(acc_ref)

    h = h_ref[...]
    g = jnp.dot(h, wg_ref[...], preferred_element_type=F32)
    u = jnp.dot(h, wu_ref[...], preferred_element_type=F32)
    a = g * jax.nn.sigmoid(g) * u
    if moe:
        gates = gates_ref[...]
        lane = lax.broadcasted_iota(jnp.int32, gates.shape, 1)
        a = a * jnp.sum(jnp.where(lane == e, gates, 0.0), axis=-1, keepdims=True)
    acc_ref[...] += jnp.dot(a.astype(BF16), wd_ref[...], preferred_element_type=F32)

    @pl.when(last)
    def _():
        o_ref[...] = x_ref[...] + mod_ref[5:6, :] * acc_ref[...]


def _ffn(h2, x2, mod_l, gates, wg, wu, wd, seq):
    t = h2.shape[0]
    tm = 512
    tf = 1408
    tpb = seq // tm
    moe = gates is not None
    nf = D_FF // tf
    if moe:
        ne = wg.shape[0]
        grid = (t // tm, ne, nf)
        rmap = lambda i, e, f: (i, 0)
        in_specs = [pl.BlockSpec((tm, D_MODEL), rmap), pl.BlockSpec((tm, D_MODEL), rmap),
                    pl.BlockSpec((None, 6, D_MODEL), lambda i, e, f: (i // tpb, 0, 0)),
                    pl.BlockSpec((tm, LANES), rmap),
                    pl.BlockSpec((None, D_MODEL, tf), lambda i, e, f: (e, 0, f)),
                    pl.BlockSpec((None, D_MODEL, tf), lambda i, e, f: (e, 0, f)),
                    pl.BlockSpec((None, tf, D_MODEL), lambda i, e, f: (e, f, 0))]
        ins = (h2, x2, mod_l, gates, wg, wu, wd)
        sem = ("parallel", "arbitrary", "arbitrary")
    else:
        grid = (t // tm, nf)
        rmap = lambda i, f: (i, 0)
        in_specs = [pl.BlockSpec((tm, D_MODEL), rmap), pl.BlockSpec((tm, D_MODEL), rmap),
                    pl.BlockSpec((None, 6, D_MODEL), lambda i, f: (i // tpb, 0, 0)),
                    pl.BlockSpec((D_MODEL, tf), lambda i, f: (0, f)),
                    pl.BlockSpec((D_MODEL, tf), lambda i, f: (0, f)),
                    pl.BlockSpec((tf, D_MODEL), lambda i, f: (f, 0))]
        ins = (h2, x2, mod_l, wg, wu, wd)
        sem = ("parallel", "arbitrary")
    return pl.pallas_call(
        functools.partial(_ffn_kernel, moe=moe),
        out_shape=jax.ShapeDtypeStruct((t, D_MODEL), F32),
        grid=grid,
        in_specs=in_specs,
        out_specs=pl.BlockSpec((tm, D_MODEL), rmap),
        scratch_shapes=[pltpu.VMEM((tm, D_MODEL), F32)],
        compiler_params=_cparams(sem),
        name="moe_ffn" if moe else "dense_ffn",
    )(*ins)


def _rot_half_cols(w):
    half = w.shape[-1] // 2
    return jnp.concatenate([-w[..., half:], w[..., :half]], axis=-1)


def _pad_cols(w, left, total):
    return jnp.pad(w, ((0, 0), (left, total - left - w.shape[-1])))


def _layout_w_in(w_in):
    sizes = (D_SB, D_SB, D_SB, MLA_Q_RANK, MLA_KV_RANK, MLA_ROPE, D_CK, D_CK, D_CK)
    splits = [int(v) for v in np.cumsum(sizes)[:-1]]
    sbq, sbk, sbv, cq, ckv, kpe, ckq, ckk, ckv2 = jnp.split(w_in, splits, axis=-1)
    kpa = _pad_cols(kpe, MLA_NOPE, LANES)
    kpb = _pad_cols(_rot_half_cols(kpe), MLA_NOPE, LANES)
    main = jnp.concatenate([sbq, sbk, cq, ckv, kpa, kpb, ckq, ckk, ckv2], axis=-1)
    return main.astype(BF16), sbv.T.astype(BF16)


def _layout_w_q_up(w):
    r = w.shape[0]
    w3 = w.reshape(r, MLA_HEADS, MLA_QK)
    nope, pe = w3[..., :MLA_NOPE], w3[..., MLA_NOPE:]
    zpad = jnp.zeros((r, MLA_HEADS, LANES - MLA_QK), w.dtype)
    a = jnp.concatenate([nope, pe, zpad], axis=-1).reshape(r, MLA_HEADS * LANES)
    b = jnp.concatenate([jnp.zeros_like(nope), _rot_half_cols(pe), zpad], axis=-1)
    return jnp.concatenate([a, b.reshape(r, MLA_HEADS * LANES)], axis=-1).astype(BF16)


def _layout_w_kv_up(w):
    r = w.shape[0]
    w3 = w.reshape(r, MLA_HEADS, MLA_NOPE + MLA_V)
    kn = jnp.pad(w3[..., :MLA_NOPE], ((0, 0), (0, 0), (0, LANES - MLA_NOPE)))
    return (kn.reshape(r, MLA_HEADS * LANES).astype(BF16),
            w3[..., MLA_NOPE:].reshape(r, D_MLA).T.astype(BF16))


def _ck_bias_table(rel_bias):
    q_pos = np.arange(CHUNK) + CK_LEFT_CHUNKS * CHUNK
    k_pos = np.arange(CK_WINDOW)
    rel = np.clip(q_pos[:, None] - k_pos[None, :], -REL_CLIP, REL_CLIP) + REL_CLIP
    return rel_bias.astype(F32)[:, rel]


def kernel(x, c, positions, ada_w, ada_b, norm_mix, norm_ffn, w_in, mla_q_norm, w_q_up, mla_kv_norm, w_kv_up, mla_q_qknorm, mla_k_qknorm, ck_q_qknorm, ck_k_qknorm, ck_rel_bias, group_out_norm, w_out, ffn_w_gate, ffn_w_up, ffn_w_down, moe_router, moe_w_gate, moe_w_up, moe_w_down):
    bsz, seq, d = x.shape
    depth = ada_w.shape[0]
    t = bsz * seq
    x2 = x.reshape(t, d)

    mod = _ada(c, ada_w, ada_b).reshape(depth, bsz, 6, d)
    rc, rs = _rope_tables(positions)
    pad128 = lambda g: jnp.pad(g, (0, LANES - g.shape[0])).reshape(1, LANES)
    pair = lambda g: jnp.concatenate([g, g]).reshape(1, LANES)

    for layer in range(depth):
        win, wsbvt = _layout_w_in(w_in[layer])
        wq = _layout_w_q_up(w_q_up[layer])
        wk, wvt = _layout_w_kv_up(w_kv_up[layer])
        sbq, sbk, sbvt, mq, mk, mvt, cq, ck, cv = _inproj(
            x2, mod[layer], norm_mix[layer].reshape(1, d), win, wsbvt, rc, rs,
            mla_q_norm[layer].reshape(1, -1), wq, mla_kv_norm[layer].reshape(1, -1), wk, wvt,
            pad128(mla_q_qknorm[layer]), pad128(mla_k_qknorm[layer]),
            pair(ck_q_qknorm[layer]), pair(ck_k_qknorm[layer]), seq)

        r3 = lambda a: a.reshape(bsz, seq, a.shape[-1])
        o_sb = _sb_attention(r3(sbq), r3(sbk), sbvt)
        o_mla = _mla_attention(r3(mq), r3(mk), mvt)
        lpad = ((0, 0), (CK_LEFT_CHUNKS * CHUNK, 0), (0, 0))
        o_ck = _ck_attention(r3(cq), jnp.pad(r3(ck), lpad), jnp.pad(r3(cv), lpad),
                             _ck_bias_table(ck_rel_bias[layer]))

        i = layer // 2
        is_moe = layer % 2 == 1
        wr = None
        if is_moe:
            wr = jnp.pad(moe_router[i], ((0, 0), (0, LANES - N_EXPERTS)))
        res = _outproj(o_sb, o_mla, o_ck.reshape(t, -1), x2,
                       mod[layer], group_out_norm[layer].reshape(1, -1),
                       w_out[layer].astype(BF16), norm_ffn[layer].reshape(1, d), wr, seq)
        if is_moe:
            x2, h2, gates = res
            x2 = _ffn(h2, x2, mod[layer], gates, moe_w_gate[i].astype(BF16),
                      moe_w_up[i].astype(BF16), moe_w_down[i].astype(BF16), seq)
        else:
            x2, h2 = res
            x2 = _ffn(h2, x2, mod[layer], None, ffn_w_gate[i].astype(BF16),
                      ffn_w_up[i].astype(BF16), ffn_w_down[i].astype(BF16), seq)
    return x2.reshape(bsz, seq, d)
```

```python
import functools

import numpy as np
import jax
import jax.numpy as jnp
from jax import lax
from jax.experimental import pallas as pl
from jax.experimental.pallas import tpu as pltpu

F32 = jnp.float32
BF16 = jnp.bfloat16

D_MODEL = 1024
CHUNK = 64
HEAD_DIM = 64
SB_HEADS = 4
MLA_HEADS = 8
MLA_Q_RANK = 256
MLA_KV_RANK = 128
MLA_NOPE = 64
MLA_ROPE = 32
MLA_V = 64
MLA_QK = MLA_NOPE + MLA_ROPE
ROPE_THETA = 10000.0
CK_HEADS = 4
CK_LEFT_CHUNKS = 8
CK_WINDOW = (CK_LEFT_CHUNKS + 1) * CHUNK
REL_CLIP = 128
D_SB = SB_HEADS * HEAD_DIM
D_MLA = MLA_HEADS * MLA_V
D_CK = CK_HEADS * HEAD_DIM
D_FF = 2816
N_EXPERTS = 8
EPS = 1e-6
NEG = -1e30

LANES = 128
VMEM_LIMIT = 56 * 1024 * 1024

C_SBQ, C_SBK = 0, 256
C_CQ = 512
C_CKV = 768
C_KPA = 896
C_KPB = 1024
C_CKQ, C_CKK, C_CKVV = 1152, 1408, 1664
IN_EXT = 1920

NT_DIMS = (((1,), (1,)), ((), ()))
TN_DIMS = (((0,), (0,)), ((), ()))
TKV = 256
LOG2E = 1.4426950408889634


def _cparams(sem, vmem=VMEM_LIMIT):
    return pltpu.CompilerParams(dimension_semantics=sem, vmem_limit_bytes=vmem)


def _ada_kernel(c_ref, w_ref, b_ref, o_ref):
    c = c_ref[...]
    ca = c * jax.nn.sigmoid(c)
    o_ref[...] = jnp.dot(ca, w_ref[...], preferred_element_type=F32,
                         precision=lax.Precision.HIGHEST) + b_ref[...]


def _ada(c, ada_w, ada_b):
    depth, d, n = ada_w.shape
    b = c.shape[0]
    tn = 1536
    return pl.pallas_call(
        _ada_kernel,
        out_shape=jax.ShapeDtypeStruct((depth, b, n), F32),
        grid=(depth, n // tn),
        in_specs=[pl.BlockSpec((b, d), lambda l, j: (0, 0)),
                  pl.BlockSpec((None, d, tn), lambda l, j: (l, 0, j)),
                  pl.BlockSpec((None, 1, tn), lambda l, j: (l, 0, j))],
        out_specs=pl.BlockSpec((None, b, tn), lambda l, j: (l, 0, j)),
        compiler_params=_cparams(("parallel", "parallel")),
        name="ada_mod",
    )(c, ada_w, ada_b.reshape(depth, 1, n))


def _rope_kernel(pos_ref, invf_ref, c_ref, s_ref):
    ang = pos_ref[...] * invf_ref[...]
    lane = lax.broadcasted_iota(jnp.int32, ang.shape, 1)
    rope = (lane >= MLA_NOPE) & (lane < MLA_QK)
    c_ref[...] = jnp.where(rope, jnp.cos(ang), jnp.where(lane < MLA_NOPE, 1.0, 0.0))
    s_ref[...] = jnp.where(rope, jnp.sin(ang), 0.0)


def _rope_tables(positions):
    t = positions.size
    tm = 1024
    inv_freq = ROPE_THETA ** (-jnp.arange(0, MLA_ROPE, 2, dtype=F32) / MLA_ROPE)
    invf = jnp.zeros((1, LANES), F32)
    invf = invf.at[0, MLA_NOPE:MLA_NOPE + 16].set(inv_freq)
    invf = invf.at[0, MLA_NOPE + 16:MLA_QK].set(inv_freq)
    pos_b = jnp.broadcast_to(positions.reshape(t, 1).astype(F32), (t, LANES))
    return pl.pallas_call(
        _rope_kernel,
        out_shape=(jax.ShapeDtypeStruct((t, LANES), F32),
                   jax.ShapeDtypeStruct((t, LANES), F32)),
        grid=(t // tm,),
        in_specs=[pl.BlockSpec((tm, LANES), lambda i: (i, 0)),
                  pl.BlockSpec((1, LANES), lambda i: (0, 0))],
        out_specs=(pl.BlockSpec((tm, LANES), lambda i: (i, 0)),
                   pl.BlockSpec((tm, LANES), lambda i: (i, 0))),
        compiler_params=_cparams(("parallel",)),
        name="rope_tables",
    )(pos_b, invf)


def _rms(x, g, n):
    ms = jnp.sum(x * x, axis=-1, keepdims=True) * (1.0 / n)
    return x * lax.rsqrt(ms + EPS) * g


def _pair_rms(blk, g, scale):
    lane = lax.broadcasted_iota(jnp.int32, blk.shape, 1)
    lo = lane < HEAD_DIM
    sq = blk * blk
    s_all = jnp.sum(sq, axis=-1, keepdims=True)
    s_lo = jnp.sum(jnp.where(lo, sq, 0.0), axis=-1, keepdims=True)
    ms = jnp.where(lo, s_lo, s_all - s_lo) * (1.0 / HEAD_DIM)
    return blk * lax.rsqrt(ms + EPS) * (g * scale)


def _store_kv_blocks(ref, xt):
    for cblk in range(xt.shape[1] // TKV):
        ref[cblk] = xt[:, cblk * TKV:(cblk + 1) * TKV].astype(ref.dtype)


def _inproj_kernel(x_ref, mod_ref, nw_ref, win_ref, wsbvt_ref, rc_ref, rs_ref, qn_ref, wq_ref,
                   kvn_ref, wk_ref, wvt_ref, gq_ref, gk_ref, gcq_ref, gck_ref,
                   sbq_ref, sbk_ref, sbvt_ref, mq_ref, mk_ref, mvt_ref,
                   cq_ref, ck_ref, cv_ref):
    x = x_ref[...]
    xn = _rms(x, nw_ref[...], D_MODEL)
    h = (xn * (1.0 + mod_ref[1:2, :]) + mod_ref[0:1, :]).astype(BF16)
    proj = jnp.dot(h, win_ref[...], preferred_element_type=F32)

    sbq_ref[...] = (proj[:, C_SBQ:C_SBQ + D_SB] * (HEAD_DIM ** -0.5 * LOG2E)).astype(BF16)
    sbk_ref[...] = proj[:, C_SBK:C_SBK + D_SB].astype(BF16)
    _store_kv_blocks(sbvt_ref, lax.dot_general(wsbvt_ref[...], h, NT_DIMS,
                                               preferred_element_type=F32))

    rc = rc_ref[...]
    rs = rs_ref[...]
    cqn = _rms(proj[:, C_CQ:C_CQ + MLA_Q_RANK], qn_ref[...], MLA_Q_RANK).astype(BF16)
    qq = jnp.dot(cqn, wq_ref[...], preferred_element_type=F32)
    gq = gq_ref[...] * (MLA_QK ** -0.5 * LOG2E)
    nq = MLA_HEADS * LANES
    for hh in range(MLA_HEADS):
        a = qq[:, hh * LANES:(hh + 1) * LANES]
        b = qq[:, nq + hh * LANES:nq + (hh + 1) * LANES]
        qh = a * rc + b * rs
        mq_ref[:, hh * LANES:(hh + 1) * LANES] = _rms(qh, gq, MLA_QK).astype(BF16)

    ckvn = _rms(proj[:, C_CKV:C_CKV + MLA_KV_RANK], kvn_ref[...], MLA_KV_RANK).astype(BF16)
    kn = jnp.dot(ckvn, wk_ref[...], preferred_element_type=F32)
    _store_kv_blocks(mvt_ref, lax.dot_general(wvt_ref[...], ckvn, NT_DIMS,
                                              preferred_element_type=F32))
    kpe = proj[:, C_KPA:C_KPA + LANES] * rc + proj[:, C_KPB:C_KPB + LANES] * rs
    gk = gk_ref[...]
    for hh in range(MLA_HEADS):
        kh = kn[:, hh * LANES:(hh + 1) * LANES] + kpe
        mk_ref[:, hh * LANES:(hh + 1) * LANES] = _rms(kh, gk, MLA_QK).astype(BF16)

    gcq = gcq_ref[...]
    gck = gck_ref[...]
    for p in range(D_CK // LANES):
        qb = proj[:, C_CKQ + p * LANES:C_CKQ + (p + 1) * LANES]
        kb = proj[:, C_CKK + p * LANES:C_CKK + (p + 1) * LANES]
        cq_ref[:, p * LANES:(p + 1) * LANES] = _pair_rms(qb, gcq, HEAD_DIM ** -0.5).astype(BF16)
        ck_ref[:, p * LANES:(p + 1) * LANES] = _pair_rms(kb, gck, 1.0).astype(BF16)
    cv_ref[...] = proj[:, C_CKVV:C_CKVV + D_CK].astype(BF16)


def _inproj(x2, mod_l, nw, win, wsbvt, rc, rs, qn, wq, kvn, wk, wvt, gq, gk, gcq, gck, seq):
    t = x2.shape[0]
    tm = 512
    tpb = seq // tm
    bsz = t // seq
    full = lambda a: pl.BlockSpec(a.shape, lambda i: (0,) * a.ndim)
    row = lambda w: pl.BlockSpec((tm, w), lambda i: (i, 0))
    rows = lambda w: (jax.ShapeDtypeStruct((t, w), BF16), row(w))
    kvt = lambda w: (jax.ShapeDtypeStruct((bsz, seq // TKV, w, TKV), BF16),
                     pl.BlockSpec((None, tm // TKV, w, TKV), lambda i: (i // tpb, i % tpb, 0, 0)))
    outs = [rows(D_SB), rows(D_SB), kvt(D_SB), rows(MLA_HEADS * LANES), rows(MLA_HEADS * LANES),
            kvt(D_MLA), rows(D_CK), rows(D_CK), rows(D_CK)]
    return pl.pallas_call(
        _inproj_kernel,
        out_shape=tuple(o[0] for o in outs),
        grid=(t // tm,),
        in_specs=[row(D_MODEL),
                  pl.BlockSpec((None, 6, D_MODEL), lambda i: (i // tpb, 0, 0)),
                  full(nw), full(win), full(wsbvt), row(LANES), row(LANES), full(qn), full(wq),
                  full(kvn), full(wk), full(wvt), full(gq), full(gk), full(gcq), full(gck)],
        out_specs=tuple(o[1] for o in outs),
        compiler_params=_cparams(("parallel",)),
        name="inproj",
    )(x2, mod_l, nw, win, wsbvt, rc, rs, qn, wq, kvn, wk, wvt, gq, gk, gcq, gck)


def _kv_iotas(tk, tq):
    return (lax.broadcasted_iota(jnp.int32, (tk, tq), 0),
            lax.broadcasted_iota(jnp.int32, (tk, tq), 1))


def _sb_kernel(q_ref, k_ref, vt_ref, o_ref, *, tq, heads):
    tk = TKV
    i = pl.program_id(2)
    lo = lax.broadcasted_iota(jnp.int32, (tq, LANES), 1) < HEAD_DIM
    qs = []
    for pb in range(heads // 2):
        q = q_ref[:, pb * LANES:(pb + 1) * LANES]
        zq = jnp.zeros_like(q)
        qs += [jnp.where(lo, q, zq), jnp.where(lo, zq, q)]
    krow, qcol = _kv_iotas(tk, tq)
    strict = krow < qcol
    ur, uc = _kv_iotas(tk, tk)
    later = jnp.where(uc > ur, 1.0, 0.0).astype(BF16)

    sign = jnp.uint32(0x80000000)

    def stage1(j, diag):
        r0 = pl.multiple_of(j * tk, tk)
        out = []
        for hh in range(heads):
            pb = hh // 2
            kb = k_ref[pl.ds(r0, tk), pb * LANES:(pb + 1) * LANES]
            y = lax.dot_general(kb, qs[hh], NT_DIMS, preferred_element_type=F32)
            neg_abs = pltpu.bitcast(pltpu.bitcast(y, jnp.uint32) | sign, F32)
            sp = jnp.maximum(y, 0.0) + jnp.log2(1.0 + jnp.exp2(neg_abs))
            spm = jnp.where(strict, sp, 0.0) if diag else sp
            tail = jnp.dot(later, spm.astype(BF16), preferred_element_type=F32)
            pre = y - sp - tail
            if diag:
                pre = jnp.where(strict, pre, NEG)
            out.append((pre, tail[0:1, :] + spm[0:1, :]))
        return tuple(out)

    def stage2(j, pres, carry):
        out = []
        for hh in range(heads):
            acc, car = carry[hh]
            pre, inc = pres[hh]
            w = jnp.exp2(pre - car)
            vb = vt_ref[j, hh * HEAD_DIM:(hh + 1) * HEAD_DIM, :]
            acc = acc + jnp.dot(vb, w.astype(BF16), preferred_element_type=F32)
            out.append((acc, car + inc))
        return tuple(out)

    init = tuple((jnp.zeros((HEAD_DIM, tq), F32), jnp.zeros((1, tq), F32))
                 for _ in range(heads))

    def body(jj, state):
        carry, pres = state
        j = i - 1 - jj
        ahead = stage1(j, False)
        return stage2(j + 1, pres, carry), ahead

    carry, pres = lax.fori_loop(0, i, body, (init, stage1(i, True)))
    carry = stage2(0, pres, carry)
    for hh in range(heads):
        o_ref[hh * HEAD_DIM:(hh + 1) * HEAD_DIM, :] = carry[hh][0].astype(o_ref.dtype)


def _sb_attention(q, k, vt):
    b, s, d = q.shape
    tq = TKV
    heads = 4
    w = heads * HEAD_DIM
    return pl.pallas_call(
        functools.partial(_sb_kernel, tq=tq, heads=heads),
        out_shape=jax.ShapeDtypeStruct((b, d, s), BF16),
        grid=(b, d // w, s // tq),
        in_specs=[pl.BlockSpec((None, tq, w), lambda bi, p, i: (bi, i, p)),
                  pl.BlockSpec((None, s, w), lambda bi, p, i: (bi, 0, p)),
                  pl.BlockSpec((None, s // TKV, w, TKV), lambda bi, p, i: (bi, 0, p, 0))],
        out_specs=pl.BlockSpec((None, w, tq), lambda bi, p, i: (bi, p, i)),
        compiler_params=_cparams(("parallel", "parallel", "arbitrary")),
        name="sb_attn",
    )(q, k, vt)


def _mla_kernel(q_ref, k_ref, vt_ref, o_ref, *, tq):
    tk = TKV
    i = pl.program_id(2)
    nsub = tq // tk
    krow, qcol = _kv_iotas(tk, tq)
    shift = CHUNK.bit_length() - 1
    qs = tuple(q_ref[:, hh * LANES:(hh + 1) * LANES] for hh in range(2))

    def scores(j):
        r0 = pl.multiple_of(j * tk, tk)
        return tuple(lax.dot_general(k_ref[pl.ds(r0, tk), hh * LANES:(hh + 1) * LANES], qs[hh],
                                     NT_DIMS, preferred_element_type=F32) for hh in range(2))

    def update(j, ss, carry, diag):
        if diag is not None:
            allowed = (jnp.right_shift(krow + diag * tk, shift) <= jnp.right_shift(qcol, shift))
        out = []
        for hh in range(2):
            m, l, acc = carry[hh]
            s = jnp.where(allowed, ss[hh], NEG) if diag is not None else ss[hh]
            m_new = jnp.maximum(m, jnp.max(s, axis=0, keepdims=True))
            alpha = jnp.exp2(m - m_new)
            p = jnp.exp2(s - m_new)
            l = alpha * l + jnp.sum(p, axis=0, keepdims=True)
            vb = vt_ref[j, hh * MLA_V:(hh + 1) * MLA_V, :]
            acc = alpha * acc + jnp.dot(vb, p.astype(BF16), preferred_element_type=F32)
            out.append((m_new, l, acc))
        return tuple(out)

    init = tuple((jnp.full((1, tq), NEG, F32), jnp.zeros((1, tq), F32),
                  jnp.zeros((MLA_V, tq), F32)) for _ in range(2))
    base = i * nsub
    carry = init
    ss = scores(base + nsub - 1)
    for d in range(nsub - 1, -1, -1):
        s_ahead = scores(jnp.maximum(base + d - 1, 0))
        carry = update(base + d, ss, carry, d)
        ss = s_ahead

    def body(jj, state):
        carry, ss = state
        j = base - 1 - jj
        s_ahead = scores(jnp.maximum(j - 1, 0))
        return update(j, ss, carry, None), s_ahead

    carry, _ = lax.fori_loop(0, base, body, (carry, ss))
    for hh in range(2):
        m, l, acc = carry[hh]
        o_ref[hh * MLA_V:(hh + 1) * MLA_V, :] = (acc / l).astype(o_ref.dtype)


def _mla_attention(q, k, vt):
    b, s, _ = q.shape
    tq = 2 * TKV
    return pl.pallas_call(
        functools.partial(_mla_kernel, tq=tq),
        out_shape=jax.ShapeDtypeStruct((b, D_MLA, s), BF16),
        grid=(b, MLA_HEADS // 2, s // tq),
        in_specs=[pl.BlockSpec((None, tq, 2 * LANES), lambda bi, p, i: (bi, i, p)),
                  pl.BlockSpec((None, s, 2 * LANES), lambda bi, p, i: (bi, 0, p)),
                  pl.BlockSpec((None, s // TKV, LANES, TKV), lambda bi, p, i: (bi, 0, p, 0))],
        out_specs=pl.BlockSpec((None, LANES, tq), lambda bi, p, i: (bi, p, i)),
        compiler_params=_cparams(("parallel", "parallel", "arbitrary")),
        name="mla_attn",
    )(q, k, vt)


def _ck_kernel(q_ref, k_ref, v_ref, bias_ref, o_ref, *, n_chunks):
    lane = lax.broadcasted_iota(jnp.int32, (CHUNK, LANES), 1)
    lo = lane < HEAD_DIM
    kpos = lax.broadcasted_iota(jnp.int32, (CHUNK, CK_WINDOW), 1)

    def body(c, carry):
        r0 = pl.multiple_of(c * CHUNK, CHUNK)
        q = q_ref[pl.ds(r0, CHUNK), :]
        kw = k_ref[pl.ds(r0, CK_WINDOW), :]
        vw = v_ref[pl.ds(r0, CK_WINDOW), :]
        valid = kpos >= (CK_LEFT_CHUNKS - c) * CHUNK
        zq = jnp.zeros_like(q)
        outs = []
        for hh in range(2):
            qh = jnp.where(lo, q, zq) if hh == 0 else jnp.where(lo, zq, q)
            s = lax.dot_general(qh, kw, NT_DIMS, preferred_element_type=F32) + bias_ref[hh]
            s = jnp.where(valid, s, NEG)
            m = jnp.max(s, axis=-1, keepdims=True)
            p = jnp.exp(s - m)
            l = jnp.sum(p, axis=-1, keepdims=True)
            outs.append(jnp.dot(p.astype(BF16), vw, preferred_element_type=F32) / l)
        o_ref[pl.ds(r0, CHUNK), :] = jnp.where(lo, outs[0], outs[1]).astype(o_ref.dtype)
        return carry

    lax.fori_loop(0, n_chunks, body, 0)


def _ck_attention(q, kpad, vpad, bias):
    b, s, d = q.shape
    sp = kpad.shape[1]
    kern = functools.partial(_ck_kernel, n_chunks=s // CHUNK)
    return pl.pallas_call(
        kern,
        out_shape=jax.ShapeDtypeStruct((b, s, d), BF16),
        grid=(b, d // LANES),
        in_specs=[pl.BlockSpec((None, s, LANES), lambda bi, p: (bi, 0, p)),
                  pl.BlockSpec((None, sp, LANES), lambda bi, p: (bi, 0, p)),
                  pl.BlockSpec((None, sp, LANES), lambda bi, p: (bi, 0, p)),
                  pl.BlockSpec((2, CHUNK, CK_WINDOW), lambda bi, p: (p, 0, 0))],
        out_specs=pl.BlockSpec((None, s, LANES), lambda bi, p: (bi, 0, p)),
        compiler_params=_cparams(("parallel", "parallel")),
        name="ck_attn",
    )(q, kpad, vpad, bias)


def _outproj_kernel(osb_ref, omla_ref, ock_ref, x_ref, mod_ref, gn_ref, wout_ref, nffn_ref,
                    *rest, moe):
    if moe:
        wr_ref, xo_ref, h2_ref, gates_ref = rest
    else:
        xo_ref, h2_ref = rest
    gn = gn_ref[...]

    def gnorm(o_ref, c0, w):
        return _rms(o_ref[...].astype(F32), gn[:, c0:c0 + w], w).astype(BF16)

    def gnorm_t(ot_ref, c0, w):
        ot = ot_ref[...].astype(F32)
        ms = jnp.sum(ot * ot, axis=0, keepdims=True) * (1.0 / w)
        return ((ot * lax.rsqrt(ms + EPS)).T * gn[:, c0:c0 + w]).astype(BF16)

    merged = jnp.concatenate([gnorm_t(osb_ref, 0, D_SB), gnorm_t(omla_ref, D_SB, D_MLA),
                              gnorm(ock_ref, D_SB + D_MLA, D_CK)], axis=-1)
    y = jnp.dot(merged, wout_ref[...], preferred_element_type=F32)
    xn = x_ref[...] + mod_ref[2:3, :] * y
    xo_ref[...] = xn
    h2 = _rms(xn, nffn_ref[...], D_MODEL) * (1.0 + mod_ref[4:5, :]) + mod_ref[3:4, :]
    h2_ref[...] = h2.astype(h2_ref.dtype)
    if moe:
        logits = jnp.dot(h2, wr_ref[...], preferred_element_type=F32,
                         precision=lax.Precision.HIGHEST)
        lane = lax.broadcasted_iota(jnp.int32, logits.shape, 1).astype(F32)
        logits = jnp.where(lane < N_EXPERTS, logits, -jnp.inf)
        m1 = jnp.max(logits, axis=-1, keepdims=True)
        i1 = jnp.min(jnp.where(logits == m1, lane, float(LANES)), axis=-1, keepdims=True)
        rest_l = jnp.where(lane == i1, -jnp.inf, logits)
        m2 = jnp.max(rest_l, axis=-1, keepdims=True)
        i2 = jnp.min(jnp.where(rest_l == m2, lane, float(LANES)), axis=-1, keepdims=True)
        e2 = jnp.exp(m2 - m1)
        w1 = 1.0 / (1.0 + e2)
        w2 = e2 / (1.0 + e2)
        gates_ref[...] = jnp.where(lane == 0.0, i1, jnp.where(lane == 1.0, i2,
                                   jnp.where(lane == 2.0, w1, jnp.where(lane == 3.0, w2, 0.0))))


def _outproj(osb, omla, ock, x2, mod_l, gn, wout, nffn, wr, seq):
    t = x2.shape[0]
    tm = 512
    tpb = seq // tm
    moe = wr is not None
    full = lambda a: pl.BlockSpec(a.shape, lambda i: (0,) * a.ndim)
    row = lambda w: pl.BlockSpec((tm, w), lambda i: (i, 0))
    colt = lambda w: pl.BlockSpec((None, w, tm), lambda i: (i // tpb, 0, i % tpb))
    ins = [osb, omla, ock, x2, mod_l, gn, wout, nffn]
    in_specs = [colt(D_SB), colt(D_MLA), row(D_CK), row(D_MODEL),
                pl.BlockSpec((None, 6, D_MODEL), lambda i: (i // tpb, 0, 0)),
                full(gn), full(wout), full(nffn)]
    out_shape = [jax.ShapeDtypeStruct((t, D_MODEL), F32),
                 jax.ShapeDtypeStruct((t, D_MODEL), F32 if moe else BF16)]
    out_specs = [row(D_MODEL), row(D_MODEL)]
    if moe:
        ins.append(wr)
        in_specs.append(full(wr))
        out_shape.append(jax.ShapeDtypeStruct((t, LANES), F32))
        out_specs.append(row(LANES))
    return pl.pallas_call(
        functools.partial(_outproj_kernel, moe=moe),
        out_shape=tuple(out_shape),
        grid=(t // tm,),
        in_specs=in_specs,
        out_specs=tuple(out_specs),
        compiler_params=_cparams(("parallel",)),
        name="outproj_moe" if moe else "outproj",
    )(*ins)


FFN_TM = 512
FFN_TF = 1408


def _swiglu_step(h, wg_ref, wu_ref, wd_ref):
    g = jnp.dot(h, wg_ref[...], preferred_element_type=F32)
    u = jnp.dot(h, wu_ref[...], preferred_element_type=F32)
    a = g * jax.nn.sigmoid(g) * u
    return jnp.dot(a.astype(BF16), wd_ref[...], preferred_element_type=F32)


def _ffn_kernel(h_ref, x_ref, mod_ref, wg_ref, wu_ref, wd_ref, o_ref, acc_ref):
    f = pl.program_id(1)

    @pl.when(f == 0)
    def _():
        acc_ref[...] = jnp.zeros_like(acc_ref)

    acc_ref[...] += _swiglu_step(h_ref[...], wg_ref, wu_ref, wd_ref)

    @pl.when(f == pl.num_programs(1) - 1)
    def _():
        o_ref[...] = x_ref[...] + mod_ref[5:6, :] * acc_ref[...]


def _ffn(h2, x2, mod_l, wg, wu, wd, seq):
    t = h2.shape[0]
    tm, tf = FFN_TM, FFN_TF
    tpb = seq // tm
    rmap = lambda i, f: (i, 0)
    return pl.pallas_call(
        _ffn_kernel,
        out_shape=jax.ShapeDtypeStruct((t, D_MODEL), F32),
        grid=(t // tm, D_FF // tf),
        in_specs=[pl.BlockSpec((tm, D_MODEL), rmap), pl.BlockSpec((tm, D_MODEL), rmap),
                  pl.BlockSpec((None, 6, D_MODEL), lambda i, f: (i // tpb, 0, 0)),
                  pl.BlockSpec((D_MODEL, tf), lambda i, f: (0, f)),
                  pl.BlockSpec((D_MODEL, tf), lambda i, f: (0, f)),
                  pl.BlockSpec((tf, D_MODEL), lambda i, f: (f, 0))],
        out_specs=pl.BlockSpec((tm, D_MODEL), rmap),
        scratch_shapes=[pltpu.VMEM((tm, D_MODEL), F32)],
        compiler_params=_cparams(("parallel", "arbitrary")),
        name="dense_ffn",
    )(h2, x2, mod_l, wg, wu, wd)


ROUTE_TM = 512


def _rank_kernel(route_ref, rank_ref, count_ref, base_ref):
    i = pl.program_id(0)

    @pl.when(i == 0)
    def _():
        base_ref[...] = jnp.zeros_like(base_ref)

    tm = route_ref.shape[0]
    route = route_ref[...]
    lane = lax.broadcasted_iota(jnp.int32, (tm, LANES), 1).astype(F32)
    sel1 = lane == route[:, 0:1]
    sel2 = lane == route[:, 1:2]
    hot = jnp.where(sel1 | sel2, 1.0, 0.0)
    r = lax.broadcasted_iota(jnp.int32, (tm, tm), 0)
    c = lax.broadcasted_iota(jnp.int32, (tm, tm), 1)
    before = jnp.where(c < r, 1.0, 0.0).astype(BF16)
    seen = jnp.dot(before, hot.astype(BF16), preferred_element_type=F32) + base_ref[...]
    r1 = jnp.sum(jnp.where(sel1, seen, 0.0), axis=-1, keepdims=True)
    r2 = jnp.sum(jnp.where(sel2, seen, 0.0), axis=-1, keepdims=True)
    rank_ref[...] = jnp.where(lane == 0.0, r1, jnp.where(lane == 1.0, r2, 0.0))
    base_ref[...] += jnp.sum(hot, axis=0, keepdims=True)
    count_ref[...] = base_ref[...]


def _moe_rank(route):
    t = route.shape[0]
    tm = ROUTE_TM
    return pl.pallas_call(
        _rank_kernel,
        out_shape=(jax.ShapeDtypeStruct((t, LANES), F32), jax.ShapeDtypeStruct((1, LANES), F32)),
        grid=(t // tm,),
        in_specs=[pl.BlockSpec((tm, LANES), lambda i: (i, 0))],
        out_specs=(pl.BlockSpec((tm, LANES), lambda i: (i, 0)),
                   pl.BlockSpec((1, LANES), lambda i: (0, 0))),
        scratch_shapes=[pltpu.VMEM((1, LANES), F32)],
        compiler_params=_cparams(("arbitrary",)),
        name="moe_rank",
    )(route)


def _dispatch_kernel(pos_ref, h_ref, xs_in_ref, xs_ref, sem):
    del xs_in_ref
    i = pl.program_id(0)
    n = pl.num_programs(0)
    td = pos_ref.shape[1]
    slot = lax.rem(i, 2)

    def issue(r, carry):
        src = h_ref.at[pl.ds(i * td + r, 1), :]
        for k in range(2):
            pltpu.make_async_copy(src, xs_ref.at[pl.ds(pos_ref[k, r], 1), :],
                                  sem.at[slot, k]).start()
        return carry

    lax.fori_loop(0, td, issue, 0, unroll=8)

    def drain(s):
        for k in range(2):
            pltpu.make_async_copy(h_ref.at[pl.ds(0, td), :], xs_ref.at[pl.ds(0, td), :],
                                  sem.at[s, k]).wait()

    @pl.when(i > 0)
    def _():
        drain(1 - slot)

    @pl.when(i == n - 1)
    def _():
        drain(slot)


def _moe_dispatch(pos, h2, n_rows):
    nt, _, td = pos.shape
    xs0 = jnp.zeros((n_rows, D_MODEL), F32)
    return pl.pallas_call(
        _dispatch_kernel,
        out_shape=jax.ShapeDtypeStruct((n_rows, D_MODEL), F32),
        grid=(nt,),
        in_specs=[pl.BlockSpec((None, 2, td), lambda i: (i, 0, 0), memory_space=pltpu.SMEM),
                  pl.BlockSpec(memory_space=pl.ANY),
                  pl.BlockSpec(memory_space=pl.ANY)],
        out_specs=pl.BlockSpec(memory_space=pl.ANY),
        scratch_shapes=[pltpu.SemaphoreType.DMA((2, 2))],
        input_output_aliases={2: 0},
        compiler_params=_cparams(("arbitrary",)),
        name="moe_dispatch",
    )(pos, h2, xs0)


def _group_ffn_kernel(te_ref, nu_ref, xs_ref, wg_ref, wu_ref, wd_ref, y_ref, xb_ref, acc_ref):
    del te_ref
    i = pl.program_id(0)
    f = pl.program_id(1)

    @pl.when(i < nu_ref[0])
    def _():
        @pl.when(f == 0)
        def _():
            xb_ref[...] = xs_ref[...].astype(BF16)
            acc_ref[...] = jnp.zeros_like(acc_ref)

        acc_ref[...] += _swiglu_step(xb_ref[...], wg_ref, wu_ref, wd_ref)

        @pl.when(f == pl.num_programs(1) - 1)
        def _():
            y_ref[...] = acc_ref[...]

    @pl.when((i >= nu_ref[0]) & (f == pl.num_programs(1) - 1))
    def _():
        y_ref[...] = jnp.zeros_like(y_ref)


def _moe_group_ffn(tile_expert, n_used, xs, wg, wu, wd):
    n_rows = xs.shape[0]
    tm, tf = FFN_TM, FFN_TF
    nf = D_FF // tf
    row = lambda i, f, te, nu: (jnp.minimum(i, nu[0] - 1), 0)
    fe = lambda i, f, nu: jnp.where(i < nu[0], f, nf - 1)
    return pl.pallas_call(
        _group_ffn_kernel,
        out_shape=jax.ShapeDtypeStruct((n_rows, D_MODEL), F32),
        grid_spec=pltpu.PrefetchScalarGridSpec(
            num_scalar_prefetch=2,
            grid=(n_rows // tm, nf),
            in_specs=[pl.BlockSpec((tm, D_MODEL), row),
                      pl.BlockSpec((None, D_MODEL, tf), lambda i, f, te, nu: (te[i], 0, fe(i, f, nu))),
                      pl.BlockSpec((None, D_MODEL, tf), lambda i, f, te, nu: (te[i], 0, fe(i, f, nu))),
                      pl.BlockSpec((None, tf, D_MODEL), lambda i, f, te, nu: (te[i], fe(i, f, nu), 0))],
            out_specs=pl.BlockSpec((tm, D_MODEL), lambda i, f, te, nu: (i, 0)),
            scratch_shapes=[pltpu.VMEM((tm, D_MODEL), BF16), pltpu.VMEM((tm, D_MODEL), F32)]),
        compiler_params=_cparams(("arbitrary", "arbitrary")),
        name="moe_group_ffn",
    )(tile_expert, n_used, xs, wg, wu, wd)


def _combine_kernel(pos_ref, route_ref, x_ref, mod_ref, y_ref, o_ref, buf_ref, sem):
    tc = pos_ref.shape[1]

    def issue(r, carry):
        for k in range(2):
            pltpu.make_async_copy(y_ref.at[pl.ds(pos_ref[k, r], 1), :],
                                  buf_ref.at[k, pl.ds(r, 1), :], sem.at[k]).start()
        return carry

    lax.fori_loop(0, tc, issue, 0, unroll=8)
    for k in range(2):
        pltpu.make_async_copy(y_ref.at[pl.ds(0, tc), :], buf_ref.at[k], sem.at[k]).wait()
    route = route_ref[...]
    y = route[:, 2:3] * buf_ref[0] + route[:, 3:4] * buf_ref[1]
    o_ref[...] = x_ref[...] + mod_ref[5:6, :] * y


def _moe_combine(pos, route, x2, mod_l, y, seq):
    nt, _, tc = pos.shape
    t = x2.shape[0]
    tpb = seq // tc
    return pl.pallas_call(
        _combine_kernel,
        out_shape=jax.ShapeDtypeStruct((t, D_MODEL), F32),
        grid=(nt,),
        in_specs=[pl.BlockSpec((None, 2, tc), lambda i: (i, 0, 0), memory_space=pltpu.SMEM),
                  pl.BlockSpec((tc, LANES), lambda i: (i, 0)),
                  pl.BlockSpec((tc, D_MODEL), lambda i: (i, 0)),
                  pl.BlockSpec((None, 6, D_MODEL), lambda i: (i // tpb, 0, 0)),
                  pl.BlockSpec(memory_space=pl.ANY)],
        out_specs=pl.BlockSpec((tc, D_MODEL), lambda i: (i, 0)),
        scratch_shapes=[pltpu.VMEM((2, tc, D_MODEL), F32), pltpu.SemaphoreType.DMA((2,))],
        compiler_params=_cparams(("arbitrary",)),
        name="moe_combine",
    )(pos, route, x2, mod_l, y)


def _moe(h2, x2, mod_l, route, wg, wu, wd, seq):
    t = h2.shape[0]
    ne = wg.shape[0]
    tm = FFN_TM
    n_tiles = (2 * t) // tm + ne
    n_rows = n_tiles * tm
    rank, count = _moe_rank(route)
    counts = count[0, :ne].astype(jnp.int32)
    tiles_per = (counts + tm - 1) // tm
    tile_end = jnp.cumsum(tiles_per)
    start = (tile_end - tiles_per) * tm
    n_used = tile_end[-1:]
    tile_ids = jnp.minimum(jnp.arange(n_tiles, dtype=jnp.int32), n_used[0] - 1)
    tile_expert = jnp.sum((tile_ids[:, None] >= tile_end[None, :]).astype(jnp.int32), axis=1)
    experts = route[:, 0:2].astype(jnp.int32)
    pos = jnp.take(start, experts) + rank[:, 0:2].astype(jnp.int32)
    pos = pos.reshape(t // ROUTE_TM, ROUTE_TM, 2).transpose(0, 2, 1)
    xs = _moe_dispatch(pos, h2, n_rows)
    y = _moe_group_ffn(tile_expert, n_used, xs, wg, wu, wd)
    return _moe_combine(pos, route, x2, mod_l, y, seq)


def _rot_half_cols(w):
    half = w.shape[-1] // 2
    return jnp.concatenate([-w[..., half:], w[..., :half]], axis=-1)


def _pad_cols(w, left, total):
    return jnp.pad(w, ((0, 0), (left, total - left - w.shape[-1])))


def _layout_w_in(w_in):
    sizes = (D_SB, D_SB, D_SB, MLA_Q_RANK, MLA_KV_RANK, MLA_ROPE, D_CK, D_CK, D_CK)
    splits = [int(v) for v in np.cumsum(sizes)[:-1]]
    sbq, sbk, sbv, cq, ckv, kpe, ckq, ckk, ckv2 = jnp.split(w_in, splits, axis=-1)
    kpa = _pad_cols(kpe, MLA_NOPE, LANES)
    kpb = _pad_cols(_rot_half_cols(kpe), MLA_NOPE, LANES)
    main = jnp.concatenate([sbq, sbk, cq, ckv, kpa, kpb, ckq, ckk, ckv2], axis=-1)
    return main.astype(BF16), sbv.T.astype(BF16)


def _layout_w_q_up(w):
    r = w.shape[0]
    w3 = w.reshape(r, MLA_HEADS, MLA_QK)
    nope, pe = w3[..., :MLA_NOPE], w3[..., MLA_NOPE:]
    zpad = jnp.zeros((r, MLA_HEADS, LANES - MLA_QK), w.dtype)
    a = jnp.concatenate([nope, pe, zpad], axis=-1).reshape(r, MLA_HEADS * LANES)
    b = jnp.concatenate([jnp.zeros_like(nope), _rot_half_cols(pe), zpad], axis=-1)
    return jnp.concatenate([a, b.reshape(r, MLA_HEADS * LANES)], axis=-1).astype(BF16)


def _layout_w_kv_up(w):
    r = w.shape[0]
    w3 = w.reshape(r, MLA_HEADS, MLA_NOPE + MLA_V)
    kn = jnp.pad(w3[..., :MLA_NOPE], ((0, 0), (0, 0), (0, LANES - MLA_NOPE)))
    return (kn.reshape(r, MLA_HEADS * LANES).astype(BF16),
            w3[..., MLA_NOPE:].reshape(r, D_MLA).T.astype(BF16))


def _ck_bias_table(rel_bias):
    q_pos = np.arange(CHUNK) + CK_LEFT_CHUNKS * CHUNK
    k_pos = np.arange(CK_WINDOW)
    rel = np.clip(q_pos[:, None] - k_pos[None, :], -REL_CLIP, REL_CLIP) + REL_CLIP
    return rel_bias.astype(F32)[:, rel]


def kernel(x, c, positions, ada_w, ada_b, norm_mix, norm_ffn, w_in, mla_q_norm, w_q_up, mla_kv_norm, w_kv_up, mla_q_qknorm, mla_k_qknorm, ck_q_qknorm, ck_k_qknorm, ck_rel_bias, group_out_norm, w_out, ffn_w_gate, ffn_w_up, ffn_w_down, moe_router, moe_w_gate, moe_w_up, moe_w_down):
    bsz, seq, d = x.shape
    depth = ada_w.shape[0]
    t = bsz * seq
    x2 = x.reshape(t, d)

    mod = _ada(c, ada_w, ada_b).reshape(depth, bsz, 6, d)
    rc, rs = _rope_tables(positions)
    pad128 = lambda g: jnp.pad(g, (0, LANES - g.shape[0])).reshape(1, LANES)
    pair = lambda g: jnp.concatenate([g, g]).reshape(1, LANES)

    for layer in range(depth):
        win, wsbvt = _layout_w_in(w_in[layer])
        wq = _layout_w_q_up(w_q_up[layer])
        wk, wvt = _layout_w_kv_up(w_kv_up[layer])
        sbq, sbk, sbvt, mq, mk, mvt, cq, ck, cv = _inproj(
            x2, mod[layer], norm_mix[layer].reshape(1, d), win, wsbvt, rc, rs,
            mla_q_norm[layer].reshape(1, -1), wq, mla_kv_norm[layer].reshape(1, -1), wk, wvt,
            pad128(mla_q_qknorm[layer]), pad128(mla_k_qknorm[layer]),
            pair(ck_q_qknorm[layer]), pair(ck_k_qknorm[layer]), seq)

        r3 = lambda a: a.reshape(bsz, seq, a.shape[-1])
        o_sb = _sb_attention(r3(sbq), r3(sbk), sbvt)
        o_mla = _mla_attention(r3(mq), r3(mk), mvt)
        lpad = ((0, 0), (CK_LEFT_CHUNKS * CHUNK, 0), (0, 0))
        o_ck = _ck_attention(r3(cq), jnp.pad(r3(ck), lpad), jnp.pad(r3(cv), lpad),
                             _ck_bias_table(ck_rel_bias[layer]))

        i = layer // 2
        is_moe = layer % 2 == 1
        wr = None
        if is_moe:
            wr = jnp.pad(moe_router[i], ((0, 0), (0, LANES - N_EXPERTS)))
        res = _outproj(o_sb, o_mla, o_ck.reshape(t, -1), x2,
                       mod[layer], group_out_norm[layer].reshape(1, -1),
                       w_out[layer].astype(BF16), norm_ffn[layer].reshape(1, d), wr, seq)
        if is_moe:
            x2, h2, route = res
            x2 = _moe(h2, x2, mod[layer], route, moe_w_gate[i].astype(BF16),
                      moe_w_up[i].astype(BF16), moe_w_down[i].astype(BF16), seq)
        else:
            x2, h2 = res
            x2 = _ffn(h2, x2, mod[layer], ffn_w_gate[i].astype(BF16),
                      ffn_w_up[i].astype(BF16), ffn_w_down[i].astype(BF16), seq)
    return x2.reshape(bsz, seq, d)
```

```python
import functools

import numpy as np
import jax
import jax.numpy as jnp
from jax import lax
from jax.experimental import pallas as pl
from jax.experimental.pallas import tpu as pltpu

F32 = jnp.float32
BF16 = jnp.bfloat16

D_MODEL = 1024
CHUNK = 64
HEAD_DIM = 64
SB_HEADS = 4
MLA_HEADS = 8
MLA_Q_RANK = 256
MLA_KV_RANK = 128
MLA_NOPE = 64
MLA_ROPE = 32
MLA_V = 64
MLA_QK = MLA_NOPE + MLA_ROPE
ROPE_THETA = 10000.0
CK_HEADS = 4
CK_LEFT_CHUNKS = 8
CK_WINDOW = (CK_LEFT_CHUNKS + 1) * CHUNK
REL_CLIP = 128
D_SB = SB_HEADS * HEAD_DIM
D_MLA = MLA_HEADS * MLA_V
D_CK = CK_HEADS * HEAD_DIM
D_FF = 2816
N_EXPERTS = 8
EPS = 1e-6
NEG = -1e30

LANES = 128
VMEM_LIMIT = 56 * 1024 * 1024

C_SBQ, C_SBK = 0, 256
C_CQ = 512
C_CKV = 768
C_KPA = 896
C_KPB = 1024
C_CKQ, C_CKK = 1152, 1408
IN_EXT = 1664

NT_DIMS = (((1,), (1,)), ((), ()))
TN_DIMS = (((0,), (0,)), ((), ()))
TKV = 256
LOG2E = 1.4426950408889634


def _cparams(sem, vmem=VMEM_LIMIT):
    return pltpu.CompilerParams(dimension_semantics=sem, vmem_limit_bytes=vmem)


def _ada_kernel(c_ref, w_ref, b_ref, o_ref):
    c = c_ref[...]
    ca = c * jax.nn.sigmoid(c)
    o_ref[...] = jnp.dot(ca, w_ref[...], preferred_element_type=F32,
                         precision=lax.Precision.HIGHEST) + b_ref[...]


def _ada(c, ada_w, ada_b):
    depth, d, n = ada_w.shape
    b = c.shape[0]
    tn = 1536
    return pl.pallas_call(
        _ada_kernel,
        out_shape=jax.ShapeDtypeStruct((depth, b, n), F32),
        grid=(depth, n // tn),
        in_specs=[pl.BlockSpec((b, d), lambda l, j: (0, 0)),
                  pl.BlockSpec((None, d, tn), lambda l, j: (l, 0, j)),
                  pl.BlockSpec((None, 1, tn), lambda l, j: (l, 0, j))],
        out_specs=pl.BlockSpec((None, b, tn), lambda l, j: (l, 0, j)),
        compiler_params=_cparams(("parallel", "parallel")),
        name="ada_mod",
    )(c, ada_w, ada_b.reshape(depth, 1, n))


def _rope_kernel(pos_ref, invf_ref, c_ref, s_ref):
    ang = pos_ref[...] * invf_ref[...]
    lane = lax.broadcasted_iota(jnp.int32, ang.shape, 1)
    rope = (lane >= MLA_NOPE) & (lane < MLA_QK)
    c_ref[...] = jnp.where(rope, jnp.cos(ang), jnp.where(lane < MLA_NOPE, 1.0, 0.0))
    s_ref[...] = jnp.where(rope, jnp.sin(ang), 0.0)


def _rope_tables(positions):
    t = positions.size
    tm = 1024
    inv_freq = ROPE_THETA ** (-jnp.arange(0, MLA_ROPE, 2, dtype=F32) / MLA_ROPE)
    invf = jnp.zeros((1, LANES), F32)
    invf = invf.at[0, MLA_NOPE:MLA_NOPE + 16].set(inv_freq)
    invf = invf.at[0, MLA_NOPE + 16:MLA_QK].set(inv_freq)
    pos_b = jnp.broadcast_to(positions.reshape(t, 1).astype(F32), (t, LANES))
    return pl.pallas_call(
        _rope_kernel,
        out_shape=(jax.ShapeDtypeStruct((t, LANES), F32),
                   jax.ShapeDtypeStruct((t, LANES), F32)),
        grid=(t // tm,),
        in_specs=[pl.BlockSpec((tm, LANES), lambda i: (i, 0)),
                  pl.BlockSpec((1, LANES), lambda i: (0, 0))],
        out_specs=(pl.BlockSpec((tm, LANES), lambda i: (i, 0)),
                   pl.BlockSpec((tm, LANES), lambda i: (i, 0))),
        compiler_params=_cparams(("parallel",)),
        name="rope_tables",
    )(pos_b, invf)


def _rms(x, g, n):
    ms = jnp.sum(x * x, axis=-1, keepdims=True) * (1.0 / n)
    return x * lax.rsqrt(ms + EPS) * g


def _pair_rms(blk, g, scale):
    lane = lax.broadcasted_iota(jnp.int32, blk.shape, 1)
    lo = lane < HEAD_DIM
    sq = blk * blk
    s_all = jnp.sum(sq, axis=-1, keepdims=True)
    s_lo = jnp.sum(jnp.where(lo, sq, 0.0), axis=-1, keepdims=True)
    ms = jnp.where(lo, s_lo, s_all - s_lo) * (1.0 / HEAD_DIM)
    return blk * lax.rsqrt(ms + EPS) * (g * scale)


def _store_kv_blocks(ref, xt):
    slab = ref.shape[-1]
    for cblk in range(xt.shape[1] // slab):
        ref[cblk] = xt[:, cblk * slab:(cblk + 1) * slab].astype(ref.dtype)


def _inproj_kernel(x_ref, mod_ref, nw_ref, win_ref, wsbvt_ref, rc_ref, rs_ref, qn_ref, wq_ref,
                   kvn_ref, wk_ref, wvt_ref, gq_ref, gk_ref, gcq_ref, gck_ref,
                   sbq_ref, sbk_ref, sbvt_ref, mq_ref, mk_ref, mvt_ref,
                   cq_ref, ck_ref, cvt_ref):
    x = x_ref[...]
    xn = _rms(x, nw_ref[...], D_MODEL)
    h = (xn * (1.0 + mod_ref[1:2, :]) + mod_ref[0:1, :]).astype(BF16)
    proj = jnp.dot(h, win_ref[...], preferred_element_type=F32)

    sbq_ref[...] = (proj[:, C_SBQ:C_SBQ + D_SB] * (HEAD_DIM ** -0.5 * LOG2E)).astype(BF16)
    sbk_ref[...] = proj[:, C_SBK:C_SBK + D_SB].astype(BF16)
    vt = lax.dot_general(wsbvt_ref[...], h, NT_DIMS, preferred_element_type=F32)
    _store_kv_blocks(sbvt_ref, vt[:D_SB])
    _store_kv_blocks(cvt_ref, vt[D_SB:])

    rc = rc_ref[...]
    rs = rs_ref[...]
    cqn = _rms(proj[:, C_CQ:C_CQ + MLA_Q_RANK], qn_ref[...], MLA_Q_RANK).astype(BF16)
    qq = jnp.dot(cqn, wq_ref[...], preferred_element_type=F32)
    gq = gq_ref[...] * (MLA_QK ** -0.5 * LOG2E)
    nq = MLA_HEADS * LANES
    for hh in range(MLA_HEADS):
        a = qq[:, hh * LANES:(hh + 1) * LANES]
        b = qq[:, nq + hh * LANES:nq + (hh + 1) * LANES]
        qh = a * rc + b * rs
        mq_ref[:, hh * LANES:(hh + 1) * LANES] = _rms(qh, gq, MLA_QK).astype(BF16)

    ckvn = _rms(proj[:, C_CKV:C_CKV + MLA_KV_RANK], kvn_ref[...], MLA_KV_RANK).astype(BF16)
    kn = jnp.dot(ckvn, wk_ref[...], preferred_element_type=F32)
    _store_kv_blocks(mvt_ref, lax.dot_general(wvt_ref[...], ckvn, NT_DIMS,
                                              preferred_element_type=F32))
    kpe = proj[:, C_KPA:C_KPA + LANES] * rc + proj[:, C_KPB:C_KPB + LANES] * rs
    gk = gk_ref[...]
    for hh in range(MLA_HEADS):
        kh = kn[:, hh * LANES:(hh + 1) * LANES] + kpe
        mk_ref[:, hh * LANES:(hh + 1) * LANES] = _rms(kh, gk, MLA_QK).astype(BF16)

    gcq = gcq_ref[...]
    gck = gck_ref[...]
    for p in range(D_CK // LANES):
        qb = proj[:, C_CKQ + p * LANES:C_CKQ + (p + 1) * LANES]
        kb = proj[:, C_CKK + p * LANES:C_CKK + (p + 1) * LANES]
        cq_ref[:, p * LANES:(p + 1) * LANES] = _pair_rms(
            qb, gcq, HEAD_DIM ** -0.5 * LOG2E).astype(BF16)
        ck_ref[:, p * LANES:(p + 1) * LANES] = _pair_rms(kb, gck, 1.0).astype(BF16)


def _inproj(x2, mod_l, nw, win, wsbvt, rc, rs, qn, wq, kvn, wk, wvt, gq, gk, gcq, gck, seq):
    t = x2.shape[0]
    tm = 512
    tpb = seq // tm
    bsz = t // seq
    full = lambda a: pl.BlockSpec(a.shape, lambda i: (0,) * a.ndim)
    row = lambda w: pl.BlockSpec((tm, w), lambda i: (i, 0))
    rows = lambda w: (jax.ShapeDtypeStruct((t, w), BF16), row(w))
    kvt = lambda w, slab=TKV: (
        jax.ShapeDtypeStruct((bsz, seq // slab, w, slab), BF16),
        pl.BlockSpec((None, tm // slab, w, slab), lambda i: (i // tpb, i % tpb, 0, 0)))
    outs = [rows(D_SB), rows(D_SB), kvt(D_SB), rows(MLA_HEADS * LANES), rows(MLA_HEADS * LANES),
            kvt(D_MLA), rows(D_CK), rows(D_CK), kvt(D_CK, CK_QB)]
    return pl.pallas_call(
        _inproj_kernel,
        out_shape=tuple(o[0] for o in outs),
        grid=(t // tm,),
        in_specs=[row(D_MODEL),
                  pl.BlockSpec((None, 6, D_MODEL), lambda i: (i // tpb, 0, 0)),
                  full(nw), full(win), full(wsbvt), row(LANES), row(LANES), full(qn), full(wq),
                  full(kvn), full(wk), full(wvt), full(gq), full(gk), full(gcq), full(gck)],
        out_specs=tuple(o[1] for o in outs),
        compiler_params=_cparams(("parallel",)),
        name="inproj",
    )(x2, mod_l, nw, win, wsbvt, rc, rs, qn, wq, kvn, wk, wvt, gq, gk, gcq, gck)


def _kv_iotas(tk, tq):
    return (lax.broadcasted_iota(jnp.int32, (tk, tq), 0),
            lax.broadcasted_iota(jnp.int32, (tk, tq), 1))


def _sb_kernel(q_ref, k_ref, vt_ref, o_ref, *, tq, heads):
    tk = TKV
    i = pl.program_id(2)
    lo = lax.broadcasted_iota(jnp.int32, (tq, LANES), 1) < HEAD_DIM
    qs = []
    for pb in range(heads // 2):
        q = q_ref[:, pb * LANES:(pb + 1) * LANES]
        zq = jnp.zeros_like(q)
        qs += [jnp.where(lo, q, zq), jnp.where(lo, zq, q)]
    krow, qcol = _kv_iotas(tk, tq)
    strict = krow < qcol
    ur, uc = _kv_iotas(tk, tk)
    later = jnp.where(uc > ur, 1.0, 0.0).astype(BF16)

    sign = jnp.uint32(0x80000000)

    def stage1(j, diag):
        r0 = pl.multiple_of(j * tk, tk)
        out = []
        for hh in range(heads):
            pb = hh // 2
            kb = k_ref[pl.ds(r0, tk), pb * LANES:(pb + 1) * LANES]
            y = lax.dot_general(kb, qs[hh], NT_DIMS, preferred_element_type=F32)
            neg_abs = pltpu.bitcast(pltpu.bitcast(y, jnp.uint32) | sign, F32)
            sp = jnp.maximum(y, 0.0) + jnp.log2(1.0 + jnp.exp2(neg_abs))
            spm = jnp.where(strict, sp, 0.0) if diag else sp
            tail = jnp.dot(later, spm.astype(BF16), preferred_element_type=F32)
            pre = y - sp - tail
            if diag:
                pre = jnp.where(strict, pre, NEG)
            out.append((pre, tail[0:1, :] + spm[0:1, :]))
        return tuple(out)

    def stage2(j, pres, carry):
        out = []
        for hh in range(heads):
            acc, car = carry[hh]
            pre, inc = pres[hh]
            w = jnp.exp2(pre - car)
            vb = vt_ref[j, hh * HEAD_DIM:(hh + 1) * HEAD_DIM, :]
            acc = acc + jnp.dot(vb, w.astype(BF16), preferred_element_type=F32)
            out.append((acc, car + inc))
        return tuple(out)

    init = tuple((jnp.zeros((HEAD_DIM, tq), F32), jnp.zeros((1, tq), F32))
                 for _ in range(heads))

    def body(jj, state):
        carry, pres = state
        j = i - 1 - jj
        ahead = stage1(j, False)
        return stage2(j + 1, pres, carry), ahead

    carry, pres = lax.fori_loop(0, i, body, (init, stage1(i, True)))
    carry = stage2(0, pres, carry)
    for hh in range(heads):
        o_ref[hh * HEAD_DIM:(hh + 1) * HEAD_DIM, :] = carry[hh][0].astype(o_ref.dtype)


def _sb_attention(q, k, vt):
    b, s, d = q.shape
    tq = TKV
    heads = 4
    w = heads * HEAD_DIM
    return pl.pallas_call(
        functools.partial(_sb_kernel, tq=tq, heads=heads),
        out_shape=jax.ShapeDtypeStruct((b, d, s), BF16),
        grid=(b, d // w, s // tq),
        in_specs=[pl.BlockSpec((None, tq, w), lambda bi, p, i: (bi, i, p)),
                  pl.BlockSpec((None, s, w), lambda bi, p, i: (bi, 0, p)),
                  pl.BlockSpec((None, s // TKV, w, TKV), lambda bi, p, i: (bi, 0, p, 0))],
        out_specs=pl.BlockSpec((None, w, tq), lambda bi, p, i: (bi, p, i)),
        compiler_params=_cparams(("parallel", "parallel", "arbitrary")),
        name="sb_attn",
    )(q, k, vt)


def _mla_kernel(q_ref, k_ref, vt_ref, o_ref, *, tq):
    tk = TKV
    i = pl.program_id(2)
    nsub = tq // tk
    krow, qcol = _kv_iotas(tk, tq)
    shift = CHUNK.bit_length() - 1
    qs = tuple(q_ref[:, hh * LANES:(hh + 1) * LANES] for hh in range(2))

    def scores(j):
        r0 = pl.multiple_of(j * tk, tk)
        return tuple(lax.dot_general(k_ref[pl.ds(r0, tk), hh * LANES:(hh + 1) * LANES], qs[hh],
                                     NT_DIMS, preferred_element_type=F32) for hh in range(2))

    def update(j, ss, carry, diag):
        if diag is not None:
            allowed = (jnp.right_shift(krow + diag * tk, shift) <= jnp.right_shift(qcol, shift))
        out = []
        for hh in range(2):
            m, l, acc = carry[hh]
            s = jnp.where(allowed, ss[hh], NEG) if diag is not None else ss[hh]
            m_new = jnp.maximum(m, jnp.max(s, axis=0, keepdims=True))
            alpha = jnp.exp2(m - m_new)
            p = jnp.exp2(s - m_new)
            l = alpha * l + jnp.sum(p, axis=0, keepdims=True)
            vb = vt_ref[j, hh * MLA_V:(hh + 1) * MLA_V, :]
            acc = alpha * acc + jnp.dot(vb, p.astype(BF16), preferred_element_type=F32)
            out.append((m_new, l, acc))
        return tuple(out)

    init = tuple((jnp.full((1, tq), NEG, F32), jnp.zeros((1, tq), F32),
                  jnp.zeros((MLA_V, tq), F32)) for _ in range(2))
    base = i * nsub
    carry = init
    ss = scores(base + nsub - 1)
    for d in range(nsub - 1, -1, -1):
        s_ahead = scores(jnp.maximum(base + d - 1, 0))
        carry = update(base + d, ss, carry, d)
        ss = s_ahead

    def body(jj, state):
        carry, ss = state
        j = base - 1 - jj
        s_ahead = scores(jnp.maximum(j - 1, 0))
        return update(j, ss, carry, None), s_ahead

    carry, _ = lax.fori_loop(0, base, body, (carry, ss))
    for hh in range(2):
        m, l, acc = carry[hh]
        o_ref[hh * MLA_V:(hh + 1) * MLA_V, :] = (acc / l).astype(o_ref.dtype)


def _mla_attention(q, k, vt):
    b, s, _ = q.shape
    tq = 2 * TKV
    return pl.pallas_call(
        functools.partial(_mla_kernel, tq=tq),
        out_shape=jax.ShapeDtypeStruct((b, D_MLA, s), BF16),
        grid=(b, MLA_HEADS // 2, s // tq),
        in_specs=[pl.BlockSpec((None, tq, 2 * LANES), lambda bi, p, i: (bi, i, p)),
                  pl.BlockSpec((None, s, 2 * LANES), lambda bi, p, i: (bi, 0, p)),
                  pl.BlockSpec((None, s // TKV, LANES, TKV), lambda bi, p, i: (bi, 0, p, 0))],
        out_specs=pl.BlockSpec((None, LANES, tq), lambda bi, p, i: (bi, p, i)),
        compiler_params=_cparams(("parallel", "parallel", "arbitrary")),
        name="mla_attn",
    )(q, k, vt)


CK_QB = 2 * CHUNK
CK_KEYS = CK_WINDOW + CHUNK
CK_PAD = CK_LEFT_CHUNKS * CHUNK


def _ck_kernel(q_ref, k_ref, vt_ref, bias_ref, o_ref, *, n_blocks, heads):
    lo = lax.broadcasted_iota(jnp.int32, (CK_QB, LANES), 1) < HEAD_DIM
    krow = lax.broadcasted_iota(jnp.int32, (CK_KEYS, CK_QB), 0)
    n_slabs = CK_KEYS // CK_QB

    def body(c2, carry, masked):
        r0 = pl.multiple_of(c2 * CK_QB, CK_QB)
        if masked:
            valid = krow >= CK_PAD - c2 * CK_QB
        for pb in range(heads // 2):
            q = q_ref[pl.ds(r0, CK_QB), pb * LANES:(pb + 1) * LANES]
            kw = k_ref[pl.ds(r0, CK_KEYS), pb * LANES:(pb + 1) * LANES]
            zq = jnp.zeros_like(q)
            for hl in range(2):
                hh = 2 * pb + hl
                qh = jnp.where(lo, q, zq) if hl == 0 else jnp.where(lo, zq, q)
                s = lax.dot_general(kw, qh, NT_DIMS, preferred_element_type=F32) + bias_ref[hh]
                if masked:
                    s = jnp.where(valid, s, NEG)
                m = jnp.max(s, axis=0, keepdims=True)
                p = jnp.exp2(s - m)
                l = jnp.sum(p, axis=0, keepdims=True)
                vw = jnp.concatenate([vt_ref[c2 + sb, hh * HEAD_DIM:(hh + 1) * HEAD_DIM, :]
                                      for sb in range(n_slabs)], axis=1)
                o = jnp.dot(vw, p.astype(BF16), preferred_element_type=F32) / l
                o_ref[c2, hh * HEAD_DIM:(hh + 1) * HEAD_DIM, :] = o.astype(o_ref.dtype)
        return carry

    n_masked = CK_PAD // CK_QB
    lax.fori_loop(0, n_masked, functools.partial(body, masked=True), 0)
    lax.fori_loop(n_masked, n_blocks, functools.partial(body, masked=False), 0)


def _ck_attention(q, kpad, vtpad, bias):
    b, s, d = q.shape
    sp = kpad.shape[1]
    nb = s // CK_QB
    heads = d // HEAD_DIM
    kern = functools.partial(_ck_kernel, n_blocks=nb, heads=heads)
    return pl.pallas_call(
        kern,
        out_shape=jax.ShapeDtypeStruct((b, nb, d, CK_QB), BF16),
        grid=(b,),
        in_specs=[pl.BlockSpec((None, s, d), lambda bi: (bi, 0, 0)),
                  pl.BlockSpec((None, sp, d), lambda bi: (bi, 0, 0)),
                  pl.BlockSpec((None, sp // CK_QB, d, CK_QB), lambda bi: (bi, 0, 0, 0)),
                  pl.BlockSpec((heads, CK_KEYS, CK_QB), lambda bi: (0, 0, 0))],
        out_specs=pl.BlockSpec((None, nb, d, CK_QB), lambda bi: (bi, 0, 0, 0)),
        compiler_params=_cparams(("parallel",)),
        name="ck_attn",
    )(q, kpad, vtpad, bias)


def _outproj_kernel(osb_ref, omla_ref, ock_ref, x_ref, mod_ref, gn_ref, wout_ref, nffn_ref,
                    *rest, moe):
    if moe:
        wr_ref, xo_ref, h2_ref, gates_ref = rest
    else:
        xo_ref, h2_ref = rest
    gn = gn_ref[...]

    def gnorm_t(ot, c0, w):
        ot = ot.astype(F32)
        ms = jnp.sum(ot * ot, axis=0, keepdims=True) * (1.0 / w)
        return ((ot * lax.rsqrt(ms + EPS)).T * gn[:, c0:c0 + w]).astype(BF16)

    ock = jnp.concatenate([ock_ref[sb] for sb in range(ock_ref.shape[0])], axis=1)
    merged = jnp.concatenate([gnorm_t(osb_ref[...], 0, D_SB), gnorm_t(omla_ref[...], D_SB, D_MLA),
                              gnorm_t(ock, D_SB + D_MLA, D_CK)], axis=-1)
    y = jnp.dot(merged, wout_ref[...], preferred_element_type=F32)
    xn = x_ref[...] + mod_ref[2:3, :] * y
    xo_ref[...] = xn
    h2 = _rms(xn, nffn_ref[...], D_MODEL) * (1.0 + mod_ref[4:5, :]) + mod_ref[3:4, :]
    h2_ref[...] = h2.astype(h2_ref.dtype)
    if moe:
        logits = jnp.dot(h2, wr_ref[...], preferred_element_type=F32,
                         precision=lax.Precision.HIGHEST)
        lane = lax.broadcasted_iota(jnp.int32, logits.shape, 1).astype(F32)
        logits = jnp.where(lane < N_EXPERTS, logits, -jnp.inf)
        m1 = jnp.max(logits, axis=-1, keepdims=True)
        i1 = jnp.min(jnp.where(logits == m1, lane, float(LANES)), axis=-1, keepdims=True)
        rest_l = jnp.where(lane == i1, -jnp.inf, logits)
        m2 = jnp.max(rest_l, axis=-1, keepdims=True)
        i2 = jnp.min(jnp.where(rest_l == m2, lane, float(LANES)), axis=-1, keepdims=True)
        e2 = jnp.exp(m2 - m1)
        w1 = 1.0 / (1.0 + e2)
        w2 = e2 / (1.0 + e2)
        gates_ref[...] = jnp.where(lane == 0.0, i1, jnp.where(lane == 1.0, i2,
                                   jnp.where(lane == 2.0, w1, jnp.where(lane == 3.0, w2, 0.0))))


def _outproj(osb, omla, ock, x2, mod_l, gn, wout, nffn, wr, seq):
    t = x2.shape[0]
    tm = 512
    tpb = seq // tm
    moe = wr is not None
    full = lambda a: pl.BlockSpec(a.shape, lambda i: (0,) * a.ndim)
    row = lambda w: pl.BlockSpec((tm, w), lambda i: (i, 0))
    colt = lambda w: pl.BlockSpec((None, w, tm), lambda i: (i // tpb, 0, i % tpb))
    ins = [osb, omla, ock, x2, mod_l, gn, wout, nffn]
    slabs = pl.BlockSpec((None, tm // CK_QB, D_CK, CK_QB), lambda i: (i // tpb, i % tpb, 0, 0))
    in_specs = [colt(D_SB), colt(D_MLA), slabs, row(D_MODEL),
                pl.BlockSpec((None, 6, D_MODEL), lambda i: (i // tpb, 0, 0)),
                full(gn), full(wout), full(nffn)]
    out_shape = [jax.ShapeDtypeStruct((t, D_MODEL), F32),
                 jax.ShapeDtypeStruct((t, D_MODEL), F32 if moe else BF16)]
    out_specs = [row(D_MODEL), row(D_MODEL)]
    if moe:
        ins.append(wr)
        in_specs.append(full(wr))
        out_shape.append(jax.ShapeDtypeStruct((t, LANES), F32))
        out_specs.append(row(LANES))
    return pl.pallas_call(
        functools.partial(_outproj_kernel, moe=moe),
        out_shape=tuple(out_shape),
        grid=(t // tm,),
        in_specs=in_specs,
        out_specs=tuple(out_specs),
        compiler_params=_cparams(("parallel",)),
        name="outproj_moe" if moe else "outproj",
    )(*ins)


FFN_TM = 512
FFN_TF = 1408


def _swiglu_step(h, wg_ref, wu_ref, wd_ref):
    g = jnp.dot(h, wg_ref[...], preferred_element_type=F32)
    u = jnp.dot(h, wu_ref[...], preferred_element_type=F32)
    a = g * jax.nn.sigmoid(g) * u
    return jnp.dot(a.astype(BF16), wd_ref[...], preferred_element_type=F32)


def _ffn_kernel(h_ref, x_ref, mod_ref, wg_ref, wu_ref, wd_ref, o_ref, acc_ref):
    f = pl.program_id(1)

    @pl.when(f == 0)
    def _():
        acc_ref[...] = jnp.zeros_like(acc_ref)

    acc_ref[...] += _swiglu_step(h_ref[...], wg_ref, wu_ref, wd_ref)

    @pl.when(f == pl.num_programs(1) - 1)
    def _():
        o_ref[...] = x_ref[...] + mod_ref[5:6, :] * acc_ref[...]


def _ffn(h2, x2, mod_l, wg, wu, wd, seq):
    t = h2.shape[0]
    tm, tf = FFN_TM, FFN_TF
    tpb = seq // tm
    rmap = lambda i, f: (i, 0)
    return pl.pallas_call(
        _ffn_kernel,
        out_shape=jax.ShapeDtypeStruct((t, D_MODEL), F32),
        grid=(t // tm, D_FF // tf),
        in_specs=[pl.BlockSpec((tm, D_MODEL), rmap), pl.BlockSpec((tm, D_MODEL), rmap),
                  pl.BlockSpec((None, 6, D_MODEL), lambda i, f: (i // tpb, 0, 0)),
                  pl.BlockSpec((D_MODEL, tf), lambda i, f: (0, f)),
                  pl.BlockSpec((D_MODEL, tf), lambda i, f: (0, f)),
                  pl.BlockSpec((tf, D_MODEL), lambda i, f: (f, 0))],
        out_specs=pl.BlockSpec((tm, D_MODEL), rmap),
        scratch_shapes=[pltpu.VMEM((tm, D_MODEL), F32)],
        compiler_params=_cparams(("parallel", "arbitrary")),
        name="dense_ffn",
    )(h2, x2, mod_l, wg, wu, wd)


ROUTE_TM = 512


def _rank_kernel(route_ref, rank_ref, count_ref, base_ref):
    i = pl.program_id(0)

    @pl.when(i == 0)
    def _():
        base_ref[...] = jnp.zeros_like(base_ref)

    tm = route_ref.shape[0]
    route = route_ref[...]
    lane = lax.broadcasted_iota(jnp.int32, (tm, LANES), 1).astype(F32)
    sel1 = lane == route[:, 0:1]
    sel2 = lane == route[:, 1:2]
    hot = jnp.where(sel1 | sel2, 1.0, 0.0)
    r = lax.broadcasted_iota(jnp.int32, (tm, tm), 0)
    c = lax.broadcasted_iota(jnp.int32, (tm, tm), 1)
    before = jnp.where(c < r, 1.0, 0.0).astype(BF16)
    seen = jnp.dot(before, hot.astype(BF16), preferred_element_type=F32) + base_ref[...]
    r1 = jnp.sum(jnp.where(sel1, seen, 0.0), axis=-1, keepdims=True)
    r2 = jnp.sum(jnp.where(sel2, seen, 0.0), axis=-1, keepdims=True)
    rank_ref[...] = jnp.where(lane == 0.0, r1, jnp.where(lane == 1.0, r2, 0.0))
    base_ref[...] += jnp.sum(hot, axis=0, keepdims=True)
    count_ref[...] = base_ref[...]


def _moe_rank(route):
    t = route.shape[0]
    tm = ROUTE_TM
    return pl.pallas_call(
        _rank_kernel,
        out_shape=(jax.ShapeDtypeStruct((t, LANES), F32), jax.ShapeDtypeStruct((1, LANES), F32)),
        grid=(t // tm,),
        in_specs=[pl.BlockSpec((tm, LANES), lambda i: (i, 0))],
        out_specs=(pl.BlockSpec((tm, LANES), lambda i: (i, 0)),
                   pl.BlockSpec((1, LANES), lambda i: (0, 0))),
        scratch_shapes=[pltpu.VMEM((1, LANES), F32)],
        compiler_params=_cparams(("arbitrary",)),
        name="moe_rank",
    )(route)


def _dispatch_kernel(pos_ref, h_ref, xs_in_ref, xs_ref, sem):
    del xs_in_ref
    td = pos_ref.shape[1]

    def issue(r, carry):
        src = h_ref.at[pl.ds(r, 1), :]
        for k in range(2):
            pltpu.make_async_copy(src, xs_ref.at[pl.ds(pos_ref[k, r], 1), :],
                                  sem.at[k]).start(priority=k)
        return carry

    lax.fori_loop(0, td, issue, 0, unroll=8)
    for k in range(2):
        pltpu.make_async_copy(h_ref, xs_ref.at[pl.ds(0, td), :], sem.at[k]).wait()


def _moe_dispatch(pos, h2, n_rows):
    nt, _, td = pos.shape
    xs0 = jnp.zeros((n_rows, D_MODEL), F32)
    return pl.pallas_call(
        _dispatch_kernel,
        out_shape=jax.ShapeDtypeStruct((n_rows, D_MODEL), F32),
        grid=(nt,),
        in_specs=[pl.BlockSpec((None, 2, td), lambda i: (i, 0, 0), memory_space=pltpu.SMEM),
                  pl.BlockSpec((td, D_MODEL), lambda i: (i, 0)),
                  pl.BlockSpec(memory_space=pl.ANY)],
        out_specs=pl.BlockSpec(memory_space=pl.ANY),
        scratch_shapes=[pltpu.SemaphoreType.DMA((2,))],
        input_output_aliases={2: 0},
        compiler_params=_cparams(("arbitrary",)),
        name="moe_dispatch",
    )(pos, h2, xs0)


def _group_ffn_kernel(te_ref, nu_ref, xs_ref, wg_ref, wu_ref, wd_ref, y_ref, xb_ref, acc_ref):
    del te_ref
    i = pl.program_id(0)
    f = pl.program_id(1)

    @pl.when(i < nu_ref[0])
    def _():
        @pl.when(f == 0)
        def _():
            xb_ref[...] = xs_ref[...].astype(BF16)
            acc_ref[...] = jnp.zeros_like(acc_ref)

        acc_ref[...] += _swiglu_step(xb_ref[...], wg_ref, wu_ref, wd_ref)

        @pl.when(f == pl.num_programs(1) - 1)
        def _():
            y_ref[...] = acc_ref[...]

    @pl.when((i >= nu_ref[0]) & (f == pl.num_programs(1) - 1))
    def _():
        y_ref[...] = jnp.zeros_like(y_ref)


def _moe_group_ffn(tile_expert, n_used, xs, wg, wu, wd):
    n_rows = xs.shape[0]
    tm, tf = FFN_TM, FFN_TF
    nf = D_FF // tf
    row = lambda i, f, te, nu: (jnp.minimum(i, nu[0] - 1), 0)
    fe = lambda i, f, nu: jnp.where(i < nu[0], f, nf - 1)
    return pl.pallas_call(
        _group_ffn_kernel,
        out_shape=jax.ShapeDtypeStruct((n_rows, D_MODEL), F32),
        grid_spec=pltpu.PrefetchScalarGridSpec(
            num_scalar_prefetch=2,
            grid=(n_rows // tm, nf),
            in_specs=[pl.BlockSpec((tm, D_MODEL), row),
                      pl.BlockSpec((None, D_MODEL, tf), lambda i, f, te, nu: (te[i], 0, fe(i, f, nu))),
                      pl.BlockSpec((None, D_MODEL, tf), lambda i, f, te, nu: (te[i], 0, fe(i, f, nu))),
                      pl.BlockSpec((None, tf, D_MODEL), lambda i, f, te, nu: (te[i], fe(i, f, nu), 0))],
            out_specs=pl.BlockSpec((tm, D_MODEL), lambda i, f, te, nu: (i, 0)),
            scratch_shapes=[pltpu.VMEM((tm, D_MODEL), BF16), pltpu.VMEM((tm, D_MODEL), F32)]),
        compiler_params=_cparams(("arbitrary", "arbitrary")),
        name="moe_group_ffn",
    )(tile_expert, n_used, xs, wg, wu, wd)


def _combine_kernel(pos_ref, route_ref, x_ref, mod_ref, y_ref, o_ref, buf_ref, sem):
    tc = pos_ref.shape[1]

    def issue(r, carry):
        for k in range(2):
            pltpu.make_async_copy(y_ref.at[pl.ds(pos_ref[k, r], 1), :],
                                  buf_ref.at[k, pl.ds(r, 1), :], sem.at[k]).start(priority=k)
        return carry

    lax.fori_loop(0, tc, issue, 0, unroll=8)
    for k in range(2):
        pltpu.make_async_copy(y_ref.at[pl.ds(0, tc), :], buf_ref.at[k], sem.at[k]).wait()
    route = route_ref[...]
    y = route[:, 2:3] * buf_ref[0] + route[:, 3:4] * buf_ref[1]
    o_ref[...] = x_ref[...] + mod_ref[5:6, :] * y


def _moe_combine(pos, route, x2, mod_l, y, seq):
    nt, _, tc = pos.shape
    t = x2.shape[0]
    tpb = seq // tc
    return pl.pallas_call(
        _combine_kernel,
        out_shape=jax.ShapeDtypeStruct((t, D_MODEL), F32),
        grid=(nt,),
        in_specs=[pl.BlockSpec((None, 2, tc), lambda i: (i, 0, 0), memory_space=pltpu.SMEM),
                  pl.BlockSpec((tc, LANES), lambda i: (i, 0)),
                  pl.BlockSpec((tc, D_MODEL), lambda i: (i, 0)),
                  pl.BlockSpec((None, 6, D_MODEL), lambda i: (i // tpb, 0, 0)),
                  pl.BlockSpec(memory_space=pl.ANY)],
        out_specs=pl.BlockSpec((tc, D_MODEL), lambda i: (i, 0)),
        scratch_shapes=[pltpu.VMEM((2, tc, D_MODEL), F32), pltpu.SemaphoreType.DMA((2,))],
        compiler_params=_cparams(("arbitrary",)),
        name="moe_combine",
    )(pos, route, x2, mod_l, y)


def _moe(h2, x2, mod_l, route, wg, wu, wd, seq):
    t = h2.shape[0]
    ne = wg.shape[0]
    tm = FFN_TM
    n_tiles = (2 * t) // tm + ne
    n_rows = n_tiles * tm
    rank, count = _moe_rank(route)
    counts = count[0, :ne].astype(jnp.int32)
    tiles_per = (counts + tm - 1) // tm
    tile_end = jnp.cumsum(tiles_per)
    start = (tile_end - tiles_per) * tm
    n_used = tile_end[-1:]
    tile_ids = jnp.minimum(jnp.arange(n_tiles, dtype=jnp.int32), n_used[0] - 1)
    tile_expert = jnp.sum((tile_ids[:, None] >= tile_end[None, :]).astype(jnp.int32), axis=1)
    experts = route[:, 0:2].astype(jnp.int32)
    pos = jnp.take(start, experts) + rank[:, 0:2].astype(jnp.int32)
    pos = pos.reshape(t // ROUTE_TM, ROUTE_TM, 2).transpose(0, 2, 1)
    xs = _moe_dispatch(pos, h2, n_rows)
    y = _moe_group_ffn(tile_expert, n_used, xs, wg, wu, wd)
    return _moe_combine(pos, route, x2, mod_l, y, seq)


def _rot_half_cols(w):
    half = w.shape[-1] // 2
    return jnp.concatenate([-w[..., half:], w[..., :half]], axis=-1)


def _pad_cols(w, left, total):
    return jnp.pad(w, ((0, 0), (left, total - left - w.shape[-1])))


def _layout_w_in(w_in):
    sizes = (D_SB, D_SB, D_SB, MLA_Q_RANK, MLA_KV_RANK, MLA_ROPE, D_CK, D_CK, D_CK)
    splits = [int(v) for v in np.cumsum(sizes)[:-1]]
    sbq, sbk, sbv, cq, ckv, kpe, ckq, ckk, ckv2 = jnp.split(w_in, splits, axis=-1)
    kpa = _pad_cols(kpe, MLA_NOPE, LANES)
    kpb = _pad_cols(_rot_half_cols(kpe), MLA_NOPE, LANES)
    main = jnp.concatenate([sbq, sbk, cq, ckv, kpa, kpb, ckq, ckk], axis=-1)
    return main.astype(BF16), jnp.concatenate([sbv, ckv2], axis=-1).T.astype(BF16)


def _layout_w_q_up(w):
    r = w.shape[0]
    w3 = w.reshape(r, MLA_HEADS, MLA_QK)
    nope, pe = w3[..., :MLA_NOPE], w3[..., MLA_NOPE:]
    zpad = jnp.zeros((r, MLA_HEADS, LANES - MLA_QK), w.dtype)
    a = jnp.concatenate([nope, pe, zpad], axis=-1).reshape(r, MLA_HEADS * LANES)
    b = jnp.concatenate([jnp.zeros_like(nope), _rot_half_cols(pe), zpad], axis=-1)
    return jnp.concatenate([a, b.reshape(r, MLA_HEADS * LANES)], axis=-1).astype(BF16)


def _layout_w_kv_up(w):
    r = w.shape[0]
    w3 = w.reshape(r, MLA_HEADS, MLA_NOPE + MLA_V)
    kn = jnp.pad(w3[..., :MLA_NOPE], ((0, 0), (0, 0), (0, LANES - MLA_NOPE)))
    return (kn.reshape(r, MLA_HEADS * LANES).astype(BF16),
            w3[..., MLA_NOPE:].reshape(r, D_MLA).T.astype(BF16))


def _ck_bias_table(rel_bias):
    heads = rel_bias.shape[0]
    n_g = CHUNK + CK_KEYS - 1
    blocks = []
    for u in range(CK_QB // CHUNK):
        d = np.arange(n_g)
        idx = np.clip(d - (CK_KEYS - 1) + CK_PAD + CHUNK * u, -REL_CLIP, REL_CLIP) + REL_CLIP
        g = rel_bias.astype(F32)[:, idx] * LOG2E
        hank = jnp.tile(g, (1, CHUNK + 1))[:, :CHUNK * (n_g + 1)]
        hank = hank.reshape(heads, CHUNK, n_g + 1)[:, :, :CK_KEYS]
        blk = jnp.transpose(hank[:, :, ::-1], (0, 2, 1))
        j = np.arange(CK_KEYS) - CHUNK * u
        in_band = ((j >= 0) & (j < CK_WINDOW))[None, :, None]
        blocks.append(jnp.where(in_band, blk, NEG))
    return jnp.concatenate(blocks, axis=-1)


def kernel(x, c, positions, ada_w, ada_b, norm_mix, norm_ffn, w_in, mla_q_norm, w_q_up, mla_kv_norm, w_kv_up, mla_q_qknorm, mla_k_qknorm, ck_q_qknorm, ck_k_qknorm, ck_rel_bias, group_out_norm, w_out, ffn_w_gate, ffn_w_up, ffn_w_down, moe_router, moe_w_gate, moe_w_up, moe_w_down):
    bsz, seq, d = x.shape
    depth = ada_w.shape[0]
    t = bsz * seq
    x2 = x.reshape(t, d)

    mod = _ada(c, ada_w, ada_b).reshape(depth, bsz, 6, d)
    rc, rs = _rope_tables(positions)
    pad128 = lambda g: jnp.pad(g, (0, LANES - g.shape[0])).reshape(1, LANES)
    pair = lambda g: jnp.concatenate([g, g]).reshape(1, LANES)

    for layer in range(depth):
        win, wsbvt = _layout_w_in(w_in[layer])
        wq = _layout_w_q_up(w_q_up[layer])
        wk, wvt = _layout_w_kv_up(w_kv_up[layer])
        sbq, sbk, sbvt, mq, mk, mvt, cq, ck, cvt = _inproj(
            x2, mod[layer], norm_mix[layer].reshape(1, d), win, wsbvt, rc, rs,
            mla_q_norm[layer].reshape(1, -1), wq, mla_kv_norm[layer].reshape(1, -1), wk, wvt,
            pad128(mla_q_qknorm[layer]), pad128(mla_k_qknorm[layer]),
            pair(ck_q_qknorm[layer]), pair(ck_k_qknorm[layer]), seq)

        r3 = lambda a: a.reshape(bsz, seq, a.shape[-1])
        o_sb = _sb_attention(r3(sbq), r3(sbk), sbvt)
        o_mla = _mla_attention(r3(mq), r3(mk), mvt)
        kpad = jnp.pad(r3(ck), ((0, 0), (CK_PAD, 0), (0, 0)))
        vtpad = jnp.pad(cvt, ((0, 0), (CK_PAD // CK_QB, 0), (0, 0), (0, 0)))
        o_ck = _ck_attention(r3(cq), kpad, vtpad, _ck_bias_table(ck_rel_bias[layer]))

        i = layer // 2
        is_moe = layer % 2 == 1
        wr = None
        if is_moe:
            wr = jnp.pad(moe_router[i], ((0, 0), (0, LANES - N_EXPERTS)))
        res = _outproj(o_sb, o_mla, o_ck, x2,
                       mod[layer], group_out_norm[layer].reshape(1, -1),
                       w_out[layer].astype(BF16), norm_ffn[layer].reshape(1, d), wr, seq)
        if is_moe:
            x2, h2, route = res
            x2 = _moe(h2, x2, mod[layer], route, moe_w_gate[i].astype(BF16),
                      moe_w_up[i].astype(BF16), moe_w_down[i].astype(BF16), seq)
        else:
            x2, h2 = res
            x2 = _ffn(h2, x2, mod[layer], ffn_w_gate[i].astype(BF16),
                      ffn_w_up[i].astype(BF16), ffn_w_down[i].astype(BF16), seq)
    return x2.reshape(bsz, seq, d)
```

```python
import functools

import numpy as np
import jax
import jax.numpy as jnp
from jax import lax
from jax.experimental import pallas as pl
from jax.experimental.pallas import tpu as pltpu

F32 = jnp.float32
BF16 = jnp.bfloat16

D_MODEL = 1024
CHUNK = 64
HEAD_DIM = 64
SB_HEADS = 4
MLA_HEADS = 8
MLA_Q_RANK = 256
MLA_KV_RANK = 128
MLA_NOPE = 64
MLA_ROPE = 32
MLA_V = 64
MLA_QK = MLA_NOPE + MLA_ROPE
ROPE_THETA = 10000.0
CK_HEADS = 4
CK_LEFT_CHUNKS = 8
CK_WINDOW = (CK_LEFT_CHUNKS + 1) * CHUNK
REL_CLIP = 128
D_SB = SB_HEADS * HEAD_DIM
D_MLA = MLA_HEADS * MLA_V
D_CK = CK_HEADS * HEAD_DIM
D_FF = 2816
N_EXPERTS = 8
EPS = 1e-6
NEG = -1e30

LANES = 128
VMEM_LIMIT = 56 * 1024 * 1024

C_SBQ, C_SBK = 0, 256
C_CQ = 512
C_CKV = 768
C_KPA = 896
C_KPB = 1024
C_CKQ, C_CKK = 1152, 1408
IN_EXT = 1664

NT_DIMS = (((1,), (1,)), ((), ()))
TN_DIMS = (((0,), (0,)), ((), ()))
TKV = 256
LOG2E = 1.4426950408889634


def _cparams(sem, vmem=VMEM_LIMIT):
    return pltpu.CompilerParams(dimension_semantics=sem, vmem_limit_bytes=vmem)


def _ada_kernel(c_ref, w_ref, b_ref, o_ref):
    c = c_ref[...]
    ca = c * jax.nn.sigmoid(c)
    o_ref[...] = jnp.dot(ca, w_ref[...], preferred_element_type=F32,
                         precision=lax.Precision.HIGHEST) + b_ref[...]


def _ada(c, ada_w, ada_b):
    depth, d, n = ada_w.shape
    b = c.shape[0]
    tn = 1536
    return pl.pallas_call(
        _ada_kernel,
        out_shape=jax.ShapeDtypeStruct((depth, b, n), F32),
        grid=(depth, n // tn),
        in_specs=[pl.BlockSpec((b, d), lambda l, j: (0, 0)),
                  pl.BlockSpec((None, d, tn), lambda l, j: (l, 0, j)),
                  pl.BlockSpec((None, 1, tn), lambda l, j: (l, 0, j))],
        out_specs=pl.BlockSpec((None, b, tn), lambda l, j: (l, 0, j)),
        compiler_params=_cparams(("parallel", "parallel")),
        name="ada_mod",
    )(c, ada_w, ada_b.reshape(depth, 1, n))


def _rope_kernel(pos_ref, invf_ref, c_ref, s_ref):
    ang = pos_ref[...] * invf_ref[...]
    lane = lax.broadcasted_iota(jnp.int32, ang.shape, 1)
    rope = (lane >= MLA_NOPE) & (lane < MLA_QK)
    c_ref[...] = jnp.where(rope, jnp.cos(ang), jnp.where(lane < MLA_NOPE, 1.0, 0.0))
    s_ref[...] = jnp.where(rope, jnp.sin(ang), 0.0)


def _rope_tables(positions):
    t = positions.size
    tm = 1024
    inv_freq = ROPE_THETA ** (-jnp.arange(0, MLA_ROPE, 2, dtype=F32) / MLA_ROPE)
    invf = jnp.zeros((1, LANES), F32)
    invf = invf.at[0, MLA_NOPE:MLA_NOPE + 16].set(inv_freq)
    invf = invf.at[0, MLA_NOPE + 16:MLA_QK].set(inv_freq)
    pos_b = jnp.broadcast_to(positions.reshape(t, 1).astype(F32), (t, LANES))
    return pl.pallas_call(
        _rope_kernel,
        out_shape=(jax.ShapeDtypeStruct((t, LANES), F32),
                   jax.ShapeDtypeStruct((t, LANES), F32)),
        grid=(t // tm,),
        in_specs=[pl.BlockSpec((tm, LANES), lambda i: (i, 0)),
                  pl.BlockSpec((1, LANES), lambda i: (0, 0))],
        out_specs=(pl.BlockSpec((tm, LANES), lambda i: (i, 0)),
                   pl.BlockSpec((tm, LANES), lambda i: (i, 0))),
        compiler_params=_cparams(("parallel",)),
        name="rope_tables",
    )(pos_b, invf)


def _rms(x, g, n):
    ms = jnp.sum(x * x, axis=-1, keepdims=True) * (1.0 / n)
    return x * lax.rsqrt(ms + EPS) * g


def _pair_rms(blk, g, scale):
    lane = lax.broadcasted_iota(jnp.int32, blk.shape, 1)
    lo = lane < HEAD_DIM
    sq = blk * blk
    s_all = jnp.sum(sq, axis=-1, keepdims=True)
    s_lo = jnp.sum(jnp.where(lo, sq, 0.0), axis=-1, keepdims=True)
    ms = jnp.where(lo, s_lo, s_all - s_lo) * (1.0 / HEAD_DIM)
    return blk * lax.rsqrt(ms + EPS) * (g * scale)


def _store_kv_blocks(ref, xt):
    slab = ref.shape[-1]
    for cblk in range(xt.shape[1] // slab):
        ref[cblk] = xt[:, cblk * slab:(cblk + 1) * slab].astype(ref.dtype)


def _inproj_kernel(x_ref, mod_ref, nw_ref, win_ref, wsbvt_ref, rc_ref, rs_ref, qn_ref, wq_ref,
                   kvn_ref, wk_ref, wvt_ref, gq_ref, gk_ref, gcq_ref, gck_ref,
                   sbq_ref, sbk_ref, sbvt_ref, mq_ref, mk_ref, mvt_ref,
                   cq_ref, ck_ref, cvt_ref):
    x = x_ref[...]
    xn = _rms(x, nw_ref[...], D_MODEL)
    h = (xn * (1.0 + mod_ref[1:2, :]) + mod_ref[0:1, :]).astype(BF16)
    proj = jnp.dot(h, win_ref[...], preferred_element_type=F32)

    sbq_ref[...] = (proj[:, C_SBQ:C_SBQ + D_SB] * (HEAD_DIM ** -0.5 * LOG2E)).astype(BF16)
    sbk_ref[...] = proj[:, C_SBK:C_SBK + D_SB].astype(BF16)
    vt = lax.dot_general(wsbvt_ref[...], h, NT_DIMS, preferred_element_type=F32)
    _store_kv_blocks(sbvt_ref, vt[:D_SB])
    _store_kv_blocks(cvt_ref, vt[D_SB:])

    rc = rc_ref[...]
    rs = rs_ref[...]
    cqn = _rms(proj[:, C_CQ:C_CQ + MLA_Q_RANK], qn_ref[...], MLA_Q_RANK).astype(BF16)
    qq = jnp.dot(cqn, wq_ref[...], preferred_element_type=F32)
    gq = gq_ref[...] * (MLA_QK ** -0.5 * LOG2E)
    nq = MLA_HEADS * LANES
    for hh in range(MLA_HEADS):
        a = qq[:, hh * LANES:(hh + 1) * LANES]
        b = qq[:, nq + hh * LANES:nq + (hh + 1) * LANES]
        qh = a * rc + b * rs
        mq_ref[:, hh * LANES:(hh + 1) * LANES] = _rms(qh, gq, MLA_QK).astype(BF16)

    ckvn = _rms(proj[:, C_CKV:C_CKV + MLA_KV_RANK], kvn_ref[...], MLA_KV_RANK).astype(BF16)
    kn = jnp.dot(ckvn, wk_ref[...], preferred_element_type=F32)
    _store_kv_blocks(mvt_ref, lax.dot_general(wvt_ref[...], ckvn, NT_DIMS,
                                              preferred_element_type=F32))
    kpe = proj[:, C_KPA:C_KPA + LANES] * rc + proj[:, C_KPB:C_KPB + LANES] * rs
    gk = gk_ref[...]
    for hh in range(MLA_HEADS):
        kh = kn[:, hh * LANES:(hh + 1) * LANES] + kpe
        mk_ref[:, hh * LANES:(hh + 1) * LANES] = _rms(kh, gk, MLA_QK).astype(BF16)

    gcq = gcq_ref[...]
    gck = gck_ref[...]
    for p in range(D_CK // LANES):
        qb = proj[:, C_CKQ + p * LANES:C_CKQ + (p + 1) * LANES]
        kb = proj[:, C_CKK + p * LANES:C_CKK + (p + 1) * LANES]
        cq_ref[:, p * LANES:(p + 1) * LANES] = _pair_rms(
            qb, gcq, HEAD_DIM ** -0.5 * LOG2E).astype(BF16)
        ck_ref[:, p * LANES:(p + 1) * LANES] = _pair_rms(kb, gck, 1.0).astype(BF16)


def _inproj(x2, mod_l, nw, win, wsbvt, rc, rs, qn, wq, kvn, wk, wvt, gq, gk, gcq, gck, seq):
    t = x2.shape[0]
    tm = 512
    tpb = seq // tm
    bsz = t // seq
    full = lambda a: pl.BlockSpec(a.shape, lambda i: (0,) * a.ndim)
    row = lambda w: pl.BlockSpec((tm, w), lambda i: (i, 0))
    rows = lambda w: (jax.ShapeDtypeStruct((t, w), BF16), row(w))
    kvt = lambda w, slab=TKV: (
        jax.ShapeDtypeStruct((bsz, seq // slab, w, slab), BF16),
        pl.BlockSpec((None, tm // slab, w, slab), lambda i: (i // tpb, i % tpb, 0, 0)))
    outs = [rows(D_SB), rows(D_SB), kvt(D_SB), rows(MLA_HEADS * LANES), rows(MLA_HEADS * LANES),
            kvt(D_MLA), rows(D_CK), rows(D_CK), kvt(D_CK, CK_QB)]
    return pl.pallas_call(
        _inproj_kernel,
        out_shape=tuple(o[0] for o in outs),
        grid=(t // tm,),
        in_specs=[row(D_MODEL),
                  pl.BlockSpec((None, 6, D_MODEL), lambda i: (i // tpb, 0, 0)),
                  full(nw), full(win), full(wsbvt), row(LANES), row(LANES), full(qn), full(wq),
                  full(kvn), full(wk), full(wvt), full(gq), full(gk), full(gcq), full(gck)],
        out_specs=tuple(o[1] for o in outs),
        compiler_params=_cparams(("parallel",)),
        name="inproj",
    )(x2, mod_l, nw, win, wsbvt, rc, rs, qn, wq, kvn, wk, wvt, gq, gk, gcq, gck)


def _kv_iotas(tk, tq):
    return (lax.broadcasted_iota(jnp.int32, (tk, tq), 0),
            lax.broadcasted_iota(jnp.int32, (tk, tq), 1))


SB_DEAD = 160.0


def _sb_kernel(q_ref, k_ref, vt_ref, o_ref, *, tq, heads):
    tk = TKV
    i = pl.program_id(2)
    lo = lax.broadcasted_iota(jnp.int32, (tq, LANES), 1) < HEAD_DIM
    qs = []
    for pb in range(heads // 2):
        q = q_ref[:, pb * LANES:(pb + 1) * LANES]
        zq = jnp.zeros_like(q)
        qs += [jnp.where(lo, q, zq), jnp.where(lo, zq, q)]
    krow, qcol = _kv_iotas(tk, tq)
    strict = krow < qcol
    ur, uc = _kv_iotas(tk, tk)
    later = jnp.where(uc > ur, 1.0, 0.0).astype(BF16)

    sign = jnp.uint32(0x80000000)

    def stage1(j, diag):
        r0 = pl.multiple_of(j * tk, tk)
        out = []
        for hh in range(heads):
            pb = hh // 2
            kb = k_ref[pl.ds(r0, tk), pb * LANES:(pb + 1) * LANES]
            y = lax.dot_general(kb, qs[hh], NT_DIMS, preferred_element_type=F32)
            neg_abs = pltpu.bitcast(pltpu.bitcast(y, jnp.uint32) | sign, F32)
            sp = jnp.maximum(y, 0.0) + jnp.log2(1.0 + jnp.exp2(neg_abs))
            spm = jnp.where(strict, sp, 0.0) if diag else sp
            tail = jnp.dot(later, spm.astype(BF16), preferred_element_type=F32)
            pre = y - sp - tail
            if diag:
                pre = jnp.where(strict, pre, NEG)
            out.append((pre, tail[0:1, :] + spm[0:1, :]))
        return tuple(out)

    def stage2(j, pres, carry):
        out = []
        for hh in range(heads):
            acc, car = carry[hh]
            pre, inc = pres[hh]
            w = jnp.exp2(pre - car)
            vb = vt_ref[j, hh * HEAD_DIM:(hh + 1) * HEAD_DIM, :]
            acc = acc + jnp.dot(vb, w.astype(BF16), preferred_element_type=F32)
            out.append((acc, car + inc))
        return tuple(out)

    init = tuple((jnp.zeros((HEAD_DIM, tq), F32), jnp.zeros((1, tq), F32))
                 for _ in range(heads))

    def cond(state):
        jj, _, _, more = state
        return (jj < i) & (more > 0)

    def body(state):
        jj, carry, pres, _ = state
        j = i - jj
        mass = functools.reduce(jnp.minimum, [carry[hh][1] + pres[hh][1] for hh in range(heads)])
        more = (jnp.min(mass) < SB_DEAD).astype(jnp.int32)
        ahead = stage1(j - 1, False)
        return jj + 1, stage2(j, pres, carry), ahead, more

    jj, carry, pres, _ = lax.while_loop(cond, body, (jnp.int32(0), init, stage1(i, True),
                                                     jnp.int32(1)))
    carry = stage2(i - jj, pres, carry)
    for hh in range(heads):
        o_ref[hh * HEAD_DIM:(hh + 1) * HEAD_DIM, :] = carry[hh][0].astype(o_ref.dtype)


def _sb_attention(q, k, vt):
    b, s, d = q.shape
    tq = TKV
    heads = 4
    w = heads * HEAD_DIM
    return pl.pallas_call(
        functools.partial(_sb_kernel, tq=tq, heads=heads),
        out_shape=jax.ShapeDtypeStruct((b, d, s), BF16),
        grid=(b, d // w, s // tq),
        in_specs=[pl.BlockSpec((None, tq, w), lambda bi, p, i: (bi, i, p)),
                  pl.BlockSpec((None, s, w), lambda bi, p, i: (bi, 0, p)),
                  pl.BlockSpec((None, s // TKV, w, TKV), lambda bi, p, i: (bi, 0, p, 0))],
        out_specs=pl.BlockSpec((None, w, tq), lambda bi, p, i: (bi, p, i)),
        compiler_params=_cparams(("parallel", "parallel", "arbitrary")),
        name="sb_attn",
    )(q, k, vt)


DEN_ROWS = 16


def _mla_kernel(q_ref, k_ref, vt_ref, o_ref, *, tq):
    tk = TKV
    i = pl.program_id(2)
    nsub = tq // tk
    krow, qcol = _kv_iotas(tk, tq)
    shift = CHUNK.bit_length() - 1
    qs = tuple(q_ref[:, hh * LANES:(hh + 1) * LANES] for hh in range(2))

    def scores(j):
        r0 = pl.multiple_of(j * tk, tk)
        return tuple(lax.dot_general(k_ref[pl.ds(r0, tk), hh * LANES:(hh + 1) * LANES], qs[hh],
                                     NT_DIMS, preferred_element_type=F32) for hh in range(2))

    def update(j, ss, carry, diag):
        if diag is not None:
            allowed = (jnp.right_shift(krow + diag * tk, shift) <= jnp.right_shift(qcol, shift))
        out = []
        for hh in range(2):
            m, acc = carry[hh]
            s = jnp.where(allowed, ss[hh], NEG) if diag is not None else ss[hh]
            m_new = jnp.maximum(m, jnp.max(s, axis=0, keepdims=True))
            alpha = jnp.exp2(m - m_new)
            p = jnp.exp2(s - m_new).astype(BF16)
            vb = jnp.concatenate([vt_ref[j, hh * MLA_V:(hh + 1) * MLA_V, :], ones], axis=0)
            acc = alpha * acc + jnp.dot(vb, p, preferred_element_type=F32)
            out.append((m_new, acc))
        return tuple(out)

    ones = jnp.ones((DEN_ROWS, tk), BF16)
    init = tuple((jnp.full((1, tq), NEG, F32), jnp.zeros((MLA_V + DEN_ROWS, tq), F32))
                 for _ in range(2))
    base = i * nsub
    carry = init
    ss = scores(base + nsub - 1)
    for d in range(nsub - 1, -1, -1):
        s_ahead = scores(jnp.maximum(base + d - 1, 0))
        carry = update(base + d, ss, carry, d)
        ss = s_ahead

    def body(jj, state):
        carry, ss = state
        j = base - 1 - jj
        s_ahead = scores(jnp.maximum(j - 1, 0))
        return update(j, ss, carry, None), s_ahead

    carry, _ = lax.fori_loop(0, base, body, (carry, ss))
    for hh in range(2):
        acc = carry[hh][1]
        o_ref[hh * MLA_V:(hh + 1) * MLA_V, :] = (
            acc[:MLA_V] / acc[MLA_V:MLA_V + 1]).astype(o_ref.dtype)


def _mla_attention(q, k, vt):
    b, s, _ = q.shape
    tq = 2 * TKV
    return pl.pallas_call(
        functools.partial(_mla_kernel, tq=tq),
        out_shape=jax.ShapeDtypeStruct((b, D_MLA, s), BF16),
        grid=(b, MLA_HEADS // 2, s // tq),
        in_specs=[pl.BlockSpec((None, tq, 2 * LANES), lambda bi, p, i: (bi, i, p)),
                  pl.BlockSpec((None, s, 2 * LANES), lambda bi, p, i: (bi, 0, p)),
                  pl.BlockSpec((None, s // TKV, LANES, TKV), lambda bi, p, i: (bi, 0, p, 0))],
        out_specs=pl.BlockSpec((None, LANES, tq), lambda bi, p, i: (bi, p, i)),
        compiler_params=_cparams(("parallel", "parallel", "arbitrary")),
        name="mla_attn",
    )(q, k, vt)


CK_QB = 2 * CHUNK
CK_KEYS = CK_WINDOW + CHUNK
CK_PAD = CK_LEFT_CHUNKS * CHUNK


def _ck_kernel(q_ref, k_ref, vt_ref, bias_ref, o_ref, *, n_blocks, heads):
    lo = lax.broadcasted_iota(jnp.int32, (CK_QB, LANES), 1) < HEAD_DIM
    krow = lax.broadcasted_iota(jnp.int32, (CK_KEYS, CK_QB), 0)
    n_slabs = CK_KEYS // CK_QB

    def body(c2, carry, masked):
        r0 = pl.multiple_of(c2 * CK_QB, CK_QB)
        if masked:
            valid = krow >= CK_PAD - c2 * CK_QB
        for pb in range(heads // 2):
            q = q_ref[pl.ds(r0, CK_QB), pb * LANES:(pb + 1) * LANES]
            kw = k_ref[pl.ds(r0, CK_KEYS), pb * LANES:(pb + 1) * LANES]
            zq = jnp.zeros_like(q)
            for hl in range(2):
                hh = 2 * pb + hl
                qh = jnp.where(lo, q, zq) if hl == 0 else jnp.where(lo, zq, q)
                s = lax.dot_general(kw, qh, NT_DIMS, preferred_element_type=F32) + bias_ref[hh]
                if masked:
                    s = jnp.where(valid, s, NEG)
                m = jnp.max(s, axis=0, keepdims=True)
                p = jnp.exp2(s - m)
                l = jnp.sum(p, axis=0, keepdims=True)
                vw = jnp.concatenate([vt_ref[c2 + sb, hh * HEAD_DIM:(hh + 1) * HEAD_DIM, :]
                                      for sb in range(n_slabs)], axis=1)
                o = jnp.dot(vw, p.astype(BF16), preferred_element_type=F32) / l
                o_ref[c2, hh * HEAD_DIM:(hh + 1) * HEAD_DIM, :] = o.astype(o_ref.dtype)
        return carry

    n_masked = CK_PAD // CK_QB
    lax.fori_loop(0, n_masked, functools.partial(body, masked=True), 0)
    lax.fori_loop(n_masked, n_blocks, functools.partial(body, masked=False), 0)


def _ck_attention(q, kpad, vtpad, bias):
    b, s, d = q.shape
    sp = kpad.shape[1]
    nb = s // CK_QB
    heads = d // HEAD_DIM
    kern = functools.partial(_ck_kernel, n_blocks=nb, heads=heads)
    return pl.pallas_call(
        kern,
        out_shape=jax.ShapeDtypeStruct((b, nb, d, CK_QB), BF16),
        grid=(b,),
        in_specs=[pl.BlockSpec((None, s, d), lambda bi: (bi, 0, 0)),
                  pl.BlockSpec((None, sp, d), lambda bi: (bi, 0, 0)),
                  pl.BlockSpec((None, sp // CK_QB, d, CK_QB), lambda bi: (bi, 0, 0, 0)),
                  pl.BlockSpec((heads, CK_KEYS, CK_QB), lambda bi: (0, 0, 0))],
        out_specs=pl.BlockSpec((None, nb, d, CK_QB), lambda bi: (bi, 0, 0, 0)),
        compiler_params=_cparams(("parallel",)),
        name="ck_attn",
    )(q, kpad, vtpad, bias)


def _outproj_kernel(osb_ref, omla_ref, ock_ref, x_ref, mod_ref, gn_ref, wout_ref, nffn_ref,
                    *rest, moe):
    if moe:
        wr_ref, xo_ref, h2_ref, gates_ref = rest
    else:
        xo_ref, h2_ref = rest
    gn = gn_ref[...]

    def gnorm_t(ot, c0, w):
        ot = ot.astype(F32)
        ms = jnp.sum(ot * ot, axis=0, keepdims=True) * (1.0 / w)
        return ((ot * lax.rsqrt(ms + EPS)).T * gn[:, c0:c0 + w]).astype(BF16)

    ock = jnp.concatenate([ock_ref[sb] for sb in range(ock_ref.shape[0])], axis=1)
    merged = jnp.concatenate([gnorm_t(osb_ref[...], 0, D_SB), gnorm_t(omla_ref[...], D_SB, D_MLA),
                              gnorm_t(ock, D_SB + D_MLA, D_CK)], axis=-1)
    y = jnp.dot(merged, wout_ref[...], preferred_element_type=F32)
    xn = x_ref[...] + mod_ref[2:3, :] * y
    xo_ref[...] = xn
    h2 = _rms(xn, nffn_ref[...], D_MODEL) * (1.0 + mod_ref[4:5, :]) + mod_ref[3:4, :]
    h2_ref[...] = h2.astype(h2_ref.dtype)
    if moe:
        wr = wr_ref[...]
        w_hi = wr.astype(BF16)
        w_lo = (wr - w_hi.astype(F32)).astype(BF16)
        h_hi = h2.astype(BF16)
        h_lo = (h2 - h_hi.astype(F32)).astype(BF16)
        logits = (jnp.dot(h_hi, w_hi, preferred_element_type=F32)
                  + (jnp.dot(h_hi, w_lo, preferred_element_type=F32)
                     + jnp.dot(h_lo, w_hi, preferred_element_type=F32)))
        lane = lax.broadcasted_iota(jnp.int32, logits.shape, 1).astype(F32)
        logits = jnp.where(lane < N_EXPERTS, logits, -jnp.inf)
        m1 = jnp.max(logits, axis=-1, keepdims=True)
        i1 = jnp.min(jnp.where(logits == m1, lane, float(LANES)), axis=-1, keepdims=True)
        rest_l = jnp.where(lane == i1, -jnp.inf, logits)
        m2 = jnp.max(rest_l, axis=-1, keepdims=True)
        i2 = jnp.min(jnp.where(rest_l == m2, lane, float(LANES)), axis=-1, keepdims=True)
        e2 = jnp.exp(m2 - m1)
        w1 = 1.0 / (1.0 + e2)
        w2 = e2 / (1.0 + e2)
        gates_ref[...] = jnp.where(lane == 0.0, i1, jnp.where(lane == 1.0, i2,
                                   jnp.where(lane == 2.0, w1, jnp.where(lane == 3.0, w2, 0.0))))


def _outproj(osb, omla, ock, x2, mod_l, gn, wout, nffn, wr, seq):
    t = x2.shape[0]
    tm = 512
    tpb = seq // tm
    moe = wr is not None
    full = lambda a: pl.BlockSpec(a.shape, lambda i: (0,) * a.ndim)
    row = lambda w: pl.BlockSpec((tm, w), lambda i: (i, 0))
    colt = lambda w: pl.BlockSpec((None, w, tm), lambda i: (i // tpb, 0, i % tpb))
    ins = [osb, omla, ock, x2, mod_l, gn, wout, nffn]
    slabs = pl.BlockSpec((None, tm // CK_QB, D_CK, CK_QB), lambda i: (i // tpb, i % tpb, 0, 0))
    in_specs = [colt(D_SB), colt(D_MLA), slabs, row(D_MODEL),
                pl.BlockSpec((None, 6, D_MODEL), lambda i: (i // tpb, 0, 0)),
                full(gn), full(wout), full(nffn)]
    out_shape = [jax.ShapeDtypeStruct((t, D_MODEL), F32),
                 jax.ShapeDtypeStruct((t, D_MODEL), F32 if moe else BF16)]
    out_specs = [row(D_MODEL), row(D_MODEL)]
    if moe:
        ins.append(wr)
        in_specs.append(full(wr))
        out_shape.append(jax.ShapeDtypeStruct((t, LANES), F32))
        out_specs.append(row(LANES))
    return pl.pallas_call(
        functools.partial(_outproj_kernel, moe=moe),
        out_shape=tuple(out_shape),
        grid=(t // tm,),
        in_specs=in_specs,
        out_specs=tuple(out_specs),
        compiler_params=_cparams(("parallel",)),
        name="outproj_moe" if moe else "outproj",
    )(*ins)


FFN_TM = 512
FFN_TF = 1408


def _swiglu_step(h, wg_ref, wu_ref, wd_ref):
    g = jnp.dot(h, wg_ref[...], preferred_element_type=F32)
    u = jnp.dot(h, wu_ref[...], preferred_element_type=F32)
    a = g * jax.nn.sigmoid(g) * u
    return jnp.dot(a.astype(BF16), wd_ref[...], preferred_element_type=F32)


def _ffn_kernel(h_ref, x_ref, mod_ref, wg_ref, wu_ref, wd_ref, o_ref, acc_ref):
    f = pl.program_id(1)

    @pl.when(f == 0)
    def _():
        acc_ref[...] = jnp.zeros_like(acc_ref)

    acc_ref[...] += _swiglu_step(h_ref[...], wg_ref, wu_ref, wd_ref)

    @pl.when(f == pl.num_programs(1) - 1)
    def _():
        o_ref[...] = x_ref[...] + mod_ref[5:6, :] * acc_ref[...]


def _ffn(h2, x2, mod_l, wg, wu, wd, seq):
    t = h2.shape[0]
    tm, tf = FFN_TM, FFN_TF
    tpb = seq // tm
    rmap = lambda i, f: (i, 0)
    return pl.pallas_call(
        _ffn_kernel,
        out_shape=jax.ShapeDtypeStruct((t, D_MODEL), F32),
        grid=(t // tm, D_FF // tf),
        in_specs=[pl.BlockSpec((tm, D_MODEL), rmap), pl.BlockSpec((tm, D_MODEL), rmap),
                  pl.BlockSpec((None, 6, D_MODEL), lambda i, f: (i // tpb, 0, 0)),
                  pl.BlockSpec((D_MODEL, tf), lambda i, f: (0, f)),
                  pl.BlockSpec((D_MODEL, tf), lambda i, f: (0, f)),
                  pl.BlockSpec((tf, D_MODEL), lambda i, f: (f, 0))],
        out_specs=pl.BlockSpec((tm, D_MODEL), rmap),
        scratch_shapes=[pltpu.VMEM((tm, D_MODEL), F32)],
        compiler_params=_cparams(("parallel", "arbitrary")),
        name="dense_ffn",
    )(h2, x2, mod_l, wg, wu, wd)


ROUTE_TM = 512


def _rank_kernel(route_ref, rank_ref, count_ref, base_ref):
    i = pl.program_id(0)

    @pl.when(i == 0)
    def _():
        base_ref[...] = jnp.zeros_like(base_ref)

    tm = route_ref.shape[0]
    route = route_ref[...]
    lane = lax.broadcasted_iota(jnp.int32, (tm, LANES), 1).astype(F32)
    sel1 = lane == route[:, 0:1]
    sel2 = lane == route[:, 1:2]
    hot = jnp.where(sel1 | sel2, 1.0, 0.0)
    r = lax.broadcasted_iota(jnp.int32, (tm, tm), 0)
    c = lax.broadcasted_iota(jnp.int32, (tm, tm), 1)
    before = jnp.where(c < r, 1.0, 0.0).astype(BF16)
    seen = jnp.dot(before, hot.astype(BF16), preferred_element_type=F32) + base_ref[...]
    r1 = jnp.sum(jnp.where(sel1, seen, 0.0), axis=-1, keepdims=True)
    r2 = jnp.sum(jnp.where(sel2, seen, 0.0), axis=-1, keepdims=True)
    rank_ref[...] = jnp.where(lane == 0.0, r1, jnp.where(lane == 1.0, r2, 0.0))
    base_ref[...] += jnp.sum(hot, axis=0, keepdims=True)
    count_ref[...] = base_ref[...]


def _moe_rank(route):
    t = route.shape[0]
    tm = ROUTE_TM
    return pl.pallas_call(
        _rank_kernel,
        out_shape=(jax.ShapeDtypeStruct((t, LANES), F32), jax.ShapeDtypeStruct((1, LANES), F32)),
        grid=(t // tm,),
        in_specs=[pl.BlockSpec((tm, LANES), lambda i: (i, 0))],
        out_specs=(pl.BlockSpec((tm, LANES), lambda i: (i, 0)),
                   pl.BlockSpec((1, LANES), lambda i: (0, 0))),
        scratch_shapes=[pltpu.VMEM((1, LANES), F32)],
        compiler_params=_cparams(("arbitrary",)),
        name="moe_rank",
    )(route)


def _dispatch_kernel(pos_ref, h_ref, xs_in_ref, xs_ref, sem):
    del xs_in_ref
    td = pos_ref.shape[1]

    def issue(r, carry):
        src = h_ref.at[pl.ds(r, 1), :]
        for k in range(2):
            pltpu.make_async_copy(src, xs_ref.at[pl.ds(pos_ref[k, r], 1), :],
                                  sem.at[k]).start(priority=k)
        return carry

    lax.fori_loop(0, td, issue, 0, unroll=8)
    for k in range(2):
        pltpu.make_async_copy(h_ref, xs_ref.at[pl.ds(0, td), :], sem.at[k]).wait()


def _moe_dispatch(pos, h2, n_rows):
    nt, _, td = pos.shape
    xs0 = jnp.zeros((n_rows, D_MODEL), F32)
    return pl.pallas_call(
        _dispatch_kernel,
        out_shape=jax.ShapeDtypeStruct((n_rows, D_MODEL), F32),
        grid=(nt,),
        in_specs=[pl.BlockSpec((None, 2, td), lambda i: (i, 0, 0), memory_space=pltpu.SMEM),
                  pl.BlockSpec((td, D_MODEL), lambda i: (i, 0)),
                  pl.BlockSpec(memory_space=pl.ANY)],
        out_specs=pl.BlockSpec(memory_space=pl.ANY),
        scratch_shapes=[pltpu.SemaphoreType.DMA((2,))],
        input_output_aliases={2: 0},
        compiler_params=_cparams(("arbitrary",)),
        name="moe_dispatch",
    )(pos, h2, xs0)


def _group_ffn_kernel(te_ref, nu_ref, xs_ref, wg_ref, wu_ref, wd_ref, y_ref, xb_ref, acc_ref):
    del te_ref
    i = pl.program_id(0)
    f = pl.program_id(1)

    @pl.when(i < nu_ref[0])
    def _():
        @pl.when(f == 0)
        def _():
            xb_ref[...] = xs_ref[...].astype(BF16)
            acc_ref[...] = jnp.zeros_like(acc_ref)

        acc_ref[...] += _swiglu_step(xb_ref[...], wg_ref, wu_ref, wd_ref)

        @pl.when(f == pl.num_programs(1) - 1)
        def _():
            y_ref[...] = acc_ref[...]

    @pl.when((i >= nu_ref[0]) & (f == pl.num_programs(1) - 1))
    def _():
        y_ref[...] = jnp.zeros_like(y_ref)


def _moe_group_ffn(tile_expert, n_used, xs, wg, wu, wd):
    n_rows = xs.shape[0]
    tm, tf = FFN_TM, FFN_TF
    nf = D_FF // tf
    row = lambda i, f, te, nu: (jnp.minimum(i, nu[0] - 1), 0)
    fe = lambda i, f, nu: jnp.where(i < nu[0], f, nf - 1)
    return pl.pallas_call(
        _group_ffn_kernel,
        out_shape=jax.ShapeDtypeStruct((n_rows, D_MODEL), F32),
        grid_spec=pltpu.PrefetchScalarGridSpec(
            num_scalar_prefetch=2,
            grid=(n_rows // tm, nf),
            in_specs=[pl.BlockSpec((tm, D_MODEL), row),
                      pl.BlockSpec((None, D_MODEL, tf), lambda i, f, te, nu: (te[i], 0, fe(i, f, nu))),
                      pl.BlockSpec((None, D_MODEL, tf), lambda i, f, te, nu: (te[i], 0, fe(i, f, nu))),
                      pl.BlockSpec((None, tf, D_MODEL), lambda i, f, te, nu: (te[i], fe(i, f, nu), 0))],
            out_specs=pl.BlockSpec((tm, D_MODEL), lambda i, f, te, nu: (i, 0)),
            scratch_shapes=[pltpu.VMEM((tm, D_MODEL), BF16), pltpu.VMEM((tm, D_MODEL), F32)]),
        compiler_params=_cparams(("arbitrary", "arbitrary")),
        name="moe_group_ffn",
    )(tile_expert, n_used, xs, wg, wu, wd)


def _combine_kernel(pos_ref, route_ref, x_ref, mod_ref, y_ref, o_ref, buf_ref, sem):
    tc = pos_ref.shape[1]

    def issue(r, carry):
        for k in range(2):
            pltpu.make_async_copy(y_ref.at[pl.ds(pos_ref[k, r], 1), :],
                                  buf_ref.at[k, pl.ds(r, 1), :], sem.at[k]).start(priority=k)
        return carry

    lax.fori_loop(0, tc, issue, 0, unroll=8)
    for k in range(2):
        pltpu.make_async_copy(y_ref.at[pl.ds(0, tc), :], buf_ref.at[k], sem.at[k]).wait()
    route = route_ref[...]
    y = route[:, 2:3] * buf_ref[0] + route[:, 3:4] * buf_ref[1]
    o_ref[...] = x_ref[...] + mod_ref[5:6, :] * y


def _moe_combine(pos, route, x2, mod_l, y, seq):
    nt, _, tc = pos.shape
    t = x2.shape[0]
    tpb = seq // tc
    return pl.pallas_call(
        _combine_kernel,
        out_shape=jax.ShapeDtypeStruct((t, D_MODEL), F32),
        grid=(nt,),
        in_specs=[pl.BlockSpec((None, 2, tc), lambda i: (i, 0, 0), memory_space=pltpu.SMEM),
                  pl.BlockSpec((tc, LANES), lambda i: (i, 0)),
                  pl.BlockSpec((tc, D_MODEL), lambda i: (i, 0)),
                  pl.BlockSpec((None, 6, D_MODEL), lambda i: (i // tpb, 0, 0)),
                  pl.BlockSpec(memory_space=pl.ANY)],
        out_specs=pl.BlockSpec((tc, D_MODEL), lambda i: (i, 0)),
        scratch_shapes=[pltpu.VMEM((2, tc, D_MODEL), F32), pltpu.SemaphoreType.DMA((2,))],
        compiler_params=_cparams(("arbitrary",)),
        name="moe_combine",
    )(pos, route, x2, mod_l, y)


def _moe(h2, x2, mod_l, route, wg, wu, wd, seq):
    t = h2.shape[0]
    ne = wg.shape[0]
    tm = FFN_TM
    n_tiles = (2 * t) // tm + ne
    n_rows = n_tiles * tm
    rank, count = _moe_rank(route)
    counts = count[0, :ne].astype(jnp.int32)
    tiles_per = (counts + tm - 1) // tm
    tile_end = jnp.cumsum(tiles_per)
    start = (tile_end - tiles_per) * tm
    n_used = tile_end[-1:]
    tile_ids = jnp.minimum(jnp.arange(n_tiles, dtype=jnp.int32), n_used[0] - 1)
    tile_expert = jnp.sum((tile_ids[:, None] >= tile_end[None, :]).astype(jnp.int32), axis=1)
    experts = route[:, 0:2].astype(jnp.int32)
    pos = jnp.take(start, experts) + rank[:, 0:2].astype(jnp.int32)
    pos = pos.reshape(t // ROUTE_TM, ROUTE_TM, 2).transpose(0, 2, 1)
    xs = _moe_dispatch(pos, h2, n_rows)
    y = _moe_group_ffn(tile_expert, n_used, xs, wg, wu, wd)
    return _moe_combine(pos, route, x2, mod_l, y, seq)


def _rot_half_cols(w):
    half = w.shape[-1] // 2
    return jnp.concatenate([-w[..., half:], w[..., :half]], axis=-1)


def _pad_cols(w, left, total):
    return jnp.pad(w, ((0, 0), (left, total - left - w.shape[-1])))


def _layout_w_in(w_in):
    sizes = (D_SB, D_SB, D_SB, MLA_Q_RANK, MLA_KV_RANK, MLA_ROPE, D_CK, D_CK, D_CK)
    splits = [int(v) for v in np.cumsum(sizes)[:-1]]
    sbq, sbk, sbv, cq, ckv, kpe, ckq, ckk, ckv2 = jnp.split(w_in, splits, axis=-1)
    kpa = _pad_cols(kpe, MLA_NOPE, LANES)
    kpb = _pad_cols(_rot_half_cols(kpe), MLA_NOPE, LANES)
    main = jnp.concatenate([sbq, sbk, cq, ckv, kpa, kpb, ckq, ckk], axis=-1)
    return main.astype(BF16), jnp.concatenate([sbv, ckv2], axis=-1).T.astype(BF16)


def _layout_w_q_up(w):
    r = w.shape[0]
    w3 = w.reshape(r, MLA_HEADS, MLA_QK)
    nope, pe = w3[..., :MLA_NOPE], w3[..., MLA_NOPE:]
    zpad = jnp.zeros((r, MLA_HEADS, LANES - MLA_QK), w.dtype)
    a = jnp.concatenate([nope, pe, zpad], axis=-1).reshape(r, MLA_HEADS * LANES)
    b = jnp.concatenate([jnp.zeros_like(nope), _rot_half_cols(pe), zpad], axis=-1)
    return jnp.concatenate([a, b.reshape(r, MLA_HEADS * LANES)], axis=-1).astype(BF16)


def _layout_w_kv_up(w):
    r = w.shape[0]
    w3 = w.reshape(r, MLA_HEADS, MLA_NOPE + MLA_V)
    kn = jnp.pad(w3[..., :MLA_NOPE], ((0, 0), (0, 0), (0, LANES - MLA_NOPE)))
    return (kn.reshape(r, MLA_HEADS * LANES).astype(BF16),
            w3[..., MLA_NOPE:].reshape(r, D_MLA).T.astype(BF16))


def _ck_bias_table(rel_bias):
    heads = rel_bias.shape[0]
    n_g = CHUNK + CK_KEYS - 1
    blocks = []
    for u in range(CK_QB // CHUNK):
        d = np.arange(n_g)
        idx = np.clip(d - (CK_KEYS - 1) + CK_PAD + CHUNK * u, -REL_CLIP, REL_CLIP) + REL_CLIP
        g = rel_bias.astype(F32)[:, idx] * LOG2E
        hank = jnp.tile(g, (1, CHUNK + 1))[:, :CHUNK * (n_g + 1)]
        hank = hank.reshape(heads, CHUNK, n_g + 1)[:, :, :CK_KEYS]
        blk = jnp.transpose(hank[:, :, ::-1], (0, 2, 1))
        j = np.arange(CK_KEYS) - CHUNK * u
        in_band = ((j >= 0) & (j < CK_WINDOW))[None, :, None]
        blocks.append(jnp.where(in_band, blk, NEG))
    return jnp.concatenate(blocks, axis=-1)


def kernel(x, c, positions, ada_w, ada_b, norm_mix, norm_ffn, w_in, mla_q_norm, w_q_up, mla_kv_norm, w_kv_up, mla_q_qknorm, mla_k_qknorm, ck_q_qknorm, ck_k_qknorm, ck_rel_bias, group_out_norm, w_out, ffn_w_gate, ffn_w_up, ffn_w_down, moe_router, moe_w_gate, moe_w_up, moe_w_down):
    bsz, seq, d = x.shape
    depth = ada_w.shape[0]
    t = bsz * seq
    x2 = x.reshape(t, d)

    mod = _ada(c, ada_w, ada_b).reshape(depth, bsz, 6, d)
    rc, rs = _rope_tables(positions)
    pad128 = lambda g: jnp.pad(g, (0, LANES - g.shape[0])).reshape(1, LANES)
    pair = lambda g: jnp.concatenate([g, g]).reshape(1, LANES)

    for layer in range(depth):
        win, wsbvt = _layout_w_in(w_in[layer])
        wq = _layout_w_q_up(w_q_up[layer])
        wk, wvt = _layout_w_kv_up(w_kv_up[layer])
        sbq, sbk, sbvt, mq, mk, mvt, cq, ck, cvt = _inproj(
            x2, mod[layer], norm_mix[layer].reshape(1, d), win, wsbvt, rc, rs,
            mla_q_norm[layer].reshape(1, -1), wq, mla_kv_norm[layer].reshape(1, -1), wk, wvt,
            pad128(mla_q_qknorm[layer]), pad128(mla_k_qknorm[layer]),
            pair(ck_q_qknorm[layer]), pair(ck_k_qknorm[layer]), seq)

        r3 = lambda a: a.reshape(bsz, seq, a.shape[-1])
        o_sb = _sb_attention(r3(sbq), r3(sbk), sbvt)
        o_mla = _mla_attention(r3(mq), r3(mk), mvt)
        kpad = jnp.pad(r3(ck), ((0, 0), (CK_PAD, 0), (0, 0)))
        vtpad = jnp.pad(cvt, ((0, 0), (CK_PAD // CK_QB, 0), (0, 0), (0, 0)))
        o_ck = _ck_attention(r3(cq), kpad, vtpad, _ck_bias_table(ck_rel_bias[layer]))

        i = layer // 2
        is_moe = layer % 2 == 1
        wr = None
        if is_moe:
            wr = jnp.pad(moe_router[i], ((0, 0), (0, LANES - N_EXPERTS)))
        res = _outproj(o_sb, o_mla, o_ck, x2,
                       mod[layer], group_out_norm[layer].reshape(1, -1),
                       w_out[layer].astype(BF16), norm_ffn[layer].reshape(1, d), wr, seq)
        if is_moe:
            x2, h2, route = res
            x2 = _moe(h2, x2, mod[layer], route, moe_w_gate[i].astype(BF16),
                      moe_w_up[i].astype(BF16), moe_w_down[i].astype(BF16), seq)
        else:
            x2, h2 = res
            x2 = _ffn(h2, x2, mod[layer], ffn_w_gate[i].astype(BF16),
                      ffn_w_up[i].astype(BF16), ffn_w_down[i].astype(BF16), seq)
    return x2.reshape(bsz, seq, d)
```

```python
import functools

import numpy as np
import jax
import jax.numpy as jnp
from jax import lax
from jax.experimental import pallas as pl
from jax.experimental.pallas import tpu as pltpu

F32 = jnp.float32
BF16 = jnp.bfloat16

D_MODEL = 1024
CHUNK = 64
HEAD_DIM = 64
SB_HEADS = 4
MLA_HEADS = 8
MLA_Q_RANK = 256
MLA_KV_RANK = 128
MLA_NOPE = 64
MLA_ROPE = 32
MLA_V = 64
MLA_QK = MLA_NOPE + MLA_ROPE
ROPE_THETA = 10000.0
CK_HEADS = 4
CK_LEFT_CHUNKS = 8
CK_WINDOW = (CK_LEFT_CHUNKS + 1) * CHUNK
REL_CLIP = 128
D_SB = SB_HEADS * HEAD_DIM
D_MLA = MLA_HEADS * MLA_V
D_CK = CK_HEADS * HEAD_DIM
D_FF = 2816
N_EXPERTS = 8
EPS = 1e-6
NEG = -1e30

LANES = 128
VMEM_LIMIT = 56 * 1024 * 1024

C_SBQ, C_SBK = 0, 256
C_CQ = 512
C_CKV = 768
C_KPA = 896
C_KPB = 1024
C_CKQ, C_CKK = 1152, 1408
IN_EXT = 1664

NT_DIMS = (((1,), (1,)), ((), ()))
TN_DIMS = (((0,), (0,)), ((), ()))
TKV = 256
LOG2E = 1.4426950408889634


def _cparams(sem, vmem=VMEM_LIMIT):
    return pltpu.CompilerParams(dimension_semantics=sem, vmem_limit_bytes=vmem)


def _ada_kernel(c_ref, w_ref, b_ref, o_ref):
    c = c_ref[...]
    ca = c * jax.nn.sigmoid(c)
    o_ref[...] = jnp.dot(ca, w_ref[...], preferred_element_type=F32,
                         precision=lax.Precision.HIGHEST) + b_ref[...]


def _ada(c, ada_w, ada_b):
    depth, d, n = ada_w.shape
    b = c.shape[0]
    tn = 1536
    return pl.pallas_call(
        _ada_kernel,
        out_shape=jax.ShapeDtypeStruct((depth, b, n), F32),
        grid=(depth, n // tn),
        in_specs=[pl.BlockSpec((b, d), lambda l, j: (0, 0)),
                  pl.BlockSpec((None, d, tn), lambda l, j: (l, 0, j)),
                  pl.BlockSpec((None, 1, tn), lambda l, j: (l, 0, j))],
        out_specs=pl.BlockSpec((None, b, tn), lambda l, j: (l, 0, j)),
        compiler_params=_cparams(("parallel", "parallel")),
        name="ada_mod",
    )(c, ada_w, ada_b.reshape(depth, 1, n))


def _rope_kernel(pos_ref, invf_ref, c_ref, s_ref):
    ang = pos_ref[...] * invf_ref[...]
    lane = lax.broadcasted_iota(jnp.int32, ang.shape, 1)
    rope = (lane >= MLA_NOPE) & (lane < MLA_QK)
    c_ref[...] = jnp.where(rope, jnp.cos(ang), jnp.where(lane < MLA_NOPE, 1.0, 0.0))
    s_ref[...] = jnp.where(rope, jnp.sin(ang), 0.0)


def _rope_tables(positions):
    t = positions.size
    tm = 1024
    inv_freq = ROPE_THETA ** (-jnp.arange(0, MLA_ROPE, 2, dtype=F32) / MLA_ROPE)
    invf = jnp.zeros((1, LANES), F32)
    invf = invf.at[0, MLA_NOPE:MLA_NOPE + 16].set(inv_freq)
    invf = invf.at[0, MLA_NOPE + 16:MLA_QK].set(inv_freq)
    pos_b = jnp.broadcast_to(positions.reshape(t, 1).astype(F32), (t, LANES))
    return pl.pallas_call(
        _rope_kernel,
        out_shape=(jax.ShapeDtypeStruct((t, LANES), F32),
                   jax.ShapeDtypeStruct((t, LANES), F32)),
        grid=(t // tm,),
        in_specs=[pl.BlockSpec((tm, LANES), lambda i: (i, 0)),
                  pl.BlockSpec((1, LANES), lambda i: (0, 0))],
        out_specs=(pl.BlockSpec((tm, LANES), lambda i: (i, 0)),
                   pl.BlockSpec((tm, LANES), lambda i: (i, 0))),
        compiler_params=_cparams(("parallel",)),
        name="rope_tables",
    )(pos_b, invf)


def _rms(x, g, n):
    ms = jnp.sum(x * x, axis=-1, keepdims=True) * (1.0 / n)
    return x * lax.rsqrt(ms + EPS) * g


def _pair_rms(blk, g, scale):
    lane = lax.broadcasted_iota(jnp.int32, blk.shape, 1)
    lo = lane < HEAD_DIM
    sq = blk * blk
    s_all = jnp.sum(sq, axis=-1, keepdims=True)
    s_lo = jnp.sum(jnp.where(lo, sq, 0.0), axis=-1, keepdims=True)
    ms = jnp.where(lo, s_lo, s_all - s_lo) * (1.0 / HEAD_DIM)
    return blk * lax.rsqrt(ms + EPS) * (g * scale)


def _store_kv_blocks(ref, xt):
    slab = ref.shape[-1]
    for cblk in range(xt.shape[1] // slab):
        ref[cblk] = xt[:, cblk * slab:(cblk + 1) * slab].astype(ref.dtype)


def _inproj_kernel(x_ref, mod_ref, nw_ref, win_ref, wsbvt_ref, rc_ref, rs_ref, qn_ref, wq_ref,
                   kvn_ref, wk_ref, wvt_ref, gq_ref, gk_ref, gcq_ref, gck_ref,
                   sbq_ref, sbk_ref, sbvt_ref, mq_ref, mk_ref, mvt_ref,
                   cq_ref, ck_ref, cvt_ref):
    x = x_ref[...]
    xn = _rms(x, nw_ref[...], D_MODEL)
    h = (xn * (1.0 + mod_ref[1:2, :]) + mod_ref[0:1, :]).astype(BF16)
    proj = jnp.dot(h, win_ref[...], preferred_element_type=F32)

    sbq_ref[...] = (proj[:, C_SBQ:C_SBQ + D_SB] * (HEAD_DIM ** -0.5 * LOG2E)).astype(BF16)
    sbk_ref[...] = proj[:, C_SBK:C_SBK + D_SB].astype(BF16)
    vt = lax.dot_general(wsbvt_ref[...], h, NT_DIMS, preferred_element_type=F32)
    _store_kv_blocks(sbvt_ref, vt[:D_SB])
    _store_kv_blocks(cvt_ref, vt[D_SB:])

    rc = rc_ref[...]
    rs = rs_ref[...]
    cqn = _rms(proj[:, C_CQ:C_CQ + MLA_Q_RANK], qn_ref[...], MLA_Q_RANK).astype(BF16)
    qq = jnp.dot(cqn, wq_ref[...], preferred_element_type=F32)
    gq = gq_ref[...] * (MLA_QK ** -0.5 * LOG2E)
    nq = MLA_HEADS * LANES
    for hh in range(MLA_HEADS):
        a = qq[:, hh * LANES:(hh + 1) * LANES]
        b = qq[:, nq + hh * LANES:nq + (hh + 1) * LANES]
        qh = a * rc + b * rs
        mq_ref[:, hh * LANES:(hh + 1) * LANES] = _rms(qh, gq, MLA_QK).astype(BF16)

    ckvn = _rms(proj[:, C_CKV:C_CKV + MLA_KV_RANK], kvn_ref[...], MLA_KV_RANK).astype(BF16)
    kn = jnp.dot(ckvn, wk_ref[...], preferred_element_type=F32)
    _store_kv_blocks(mvt_ref, lax.dot_general(wvt_ref[...], ckvn, NT_DIMS,
                                              preferred_element_type=F32))
    kpe = proj[:, C_KPA:C_KPA + LANES] * rc + proj[:, C_KPB:C_KPB + LANES] * rs
    gk = gk_ref[...]
    for hh in range(MLA_HEADS):
        kh = kn[:, hh * LANES:(hh + 1) * LANES] + kpe
        mk_ref[:, hh * LANES:(hh + 1) * LANES] = _rms(kh, gk, MLA_QK).astype(BF16)

    gcq = gcq_ref[...]
    gck = gck_ref[...]
    for p in range(D_CK // LANES):
        qb = proj[:, C_CKQ + p * LANES:C_CKQ + (p + 1) * LANES]
        kb = proj[:, C_CKK + p * LANES:C_CKK + (p + 1) * LANES]
        cq_ref[:, p * LANES:(p + 1) * LANES] = _pair_rms(
            qb, gcq, HEAD_DIM ** -0.5 * LOG2E).astype(BF16)
        ck_ref[:, p * LANES:(p + 1) * LANES] = _pair_rms(kb, gck, 1.0).astype(BF16)


def _inproj(x2, mod_l, nw, win, wsbvt, rc, rs, qn, wq, kvn, wk, wvt, gq, gk, gcq, gck, seq):
    t = x2.shape[0]
    tm = 512
    tpb = seq // tm
    bsz = t // seq
    full = lambda a: pl.BlockSpec(a.shape, lambda i: (0,) * a.ndim)
    row = lambda w: pl.BlockSpec((tm, w), lambda i: (i, 0))
    rows = lambda w: (jax.ShapeDtypeStruct((t, w), BF16), row(w))
    kvt = lambda w, slab=TKV: (
        jax.ShapeDtypeStruct((bsz, seq // slab, w, slab), BF16),
        pl.BlockSpec((None, tm // slab, w, slab), lambda i: (i // tpb, i % tpb, 0, 0)))
    outs = [rows(D_SB), rows(D_SB), kvt(D_SB), rows(MLA_HEADS * LANES), rows(MLA_HEADS * LANES),
            kvt(D_MLA), rows(D_CK), rows(D_CK), kvt(D_CK, CK_QB)]
    return pl.pallas_call(
        _inproj_kernel,
        out_shape=tuple(o[0] for o in outs),
        grid=(t // tm,),
        in_specs=[row(D_MODEL),
                  pl.BlockSpec((None, 6, D_MODEL), lambda i: (i // tpb, 0, 0)),
                  full(nw), full(win), full(wsbvt), row(LANES), row(LANES), full(qn), full(wq),
                  full(kvn), full(wk), full(wvt), full(gq), full(gk), full(gcq), full(gck)],
        out_specs=tuple(o[1] for o in outs),
        compiler_params=_cparams(("parallel",)),
        name="inproj",
    )(x2, mod_l, nw, win, wsbvt, rc, rs, qn, wq, kvn, wk, wvt, gq, gk, gcq, gck)


def _kv_iotas(tk, tq):
    return (lax.broadcasted_iota(jnp.int32, (tk, tq), 0),
            lax.broadcasted_iota(jnp.int32, (tk, tq), 1))


SB_DEAD = 160.0


def _sb_kernel(q_ref, k_ref, vt_ref, o_ref, *, tq, heads):
    tk = TKV
    i = pl.program_id(2)
    lo = lax.broadcasted_iota(jnp.int32, (tq, LANES), 1) < HEAD_DIM
    qs = []
    for pb in range(heads // 2):
        q = q_ref[:, pb * LANES:(pb + 1) * LANES]
        zq = jnp.zeros_like(q)
        qs += [jnp.where(lo, q, zq), jnp.where(lo, zq, q)]
    krow, qcol = _kv_iotas(tk, tq)
    strict = krow < qcol
    ur, uc = _kv_iotas(tk, tk)
    later = jnp.where(uc > ur, 1.0, 0.0).astype(BF16)

    sign = jnp.uint32(0x80000000)

    def stage1(j, diag):
        r0 = pl.multiple_of(j * tk, tk)
        out = []
        for hh in range(heads):
            pb = hh // 2
            kb = k_ref[pl.ds(r0, tk), pb * LANES:(pb + 1) * LANES]
            y = lax.dot_general(kb, qs[hh], NT_DIMS, preferred_element_type=F32)
            neg_abs = pltpu.bitcast(pltpu.bitcast(y, jnp.uint32) | sign, F32)
            sp = jnp.maximum(y, 0.0) + jnp.log2(1.0 + jnp.exp2(neg_abs))
            spm = jnp.where(strict, sp, 0.0) if diag else sp
            tail = jnp.dot(later, spm.astype(BF16), preferred_element_type=F32)
            pre = y - sp - tail
            if diag:
                pre = jnp.where(strict, pre, NEG)
            out.append((pre, tail[0:1, :] + spm[0:1, :]))
        return tuple(out)

    def stage2(j, pres, carry):
        out = []
        for hh in range(heads):
            acc, car = carry[hh]
            pre, inc = pres[hh]
            w = jnp.exp2(pre - car)
            vb = vt_ref[j, hh * HEAD_DIM:(hh + 1) * HEAD_DIM, :]
            acc = acc + jnp.dot(vb, w.astype(BF16), preferred_element_type=F32)
            out.append((acc, car + inc))
        return tuple(out)

    init = tuple((jnp.zeros((HEAD_DIM, tq), F32), jnp.zeros((1, tq), F32))
                 for _ in range(heads))

    def cond(state):
        jj, _, _, more = state
        return (jj < i) & (more > 0)

    def body(state):
        jj, carry, pres, _ = state
        j = i - jj
        mass = functools.reduce(jnp.minimum, [carry[hh][1] + pres[hh][1] for hh in range(heads)])
        more = (jnp.min(mass) < SB_DEAD).astype(jnp.int32)
        ahead = stage1(j - 1, False)
        return jj + 1, stage2(j, pres, carry), ahead, more

    jj, carry, pres, _ = lax.while_loop(cond, body, (jnp.int32(0), init, stage1(i, True),
                                                     jnp.int32(1)))
    carry = stage2(i - jj, pres, carry)
    for hh in range(heads):
        o_ref[hh * HEAD_DIM:(hh + 1) * HEAD_DIM, :] = carry[hh][0].astype(o_ref.dtype)


def _sb_attention(q, k, vt):
    b, s, d = q.shape
    tq = TKV
    heads = 4
    w = heads * HEAD_DIM
    return pl.pallas_call(
        functools.partial(_sb_kernel, tq=tq, heads=heads),
        out_shape=jax.ShapeDtypeStruct((b, d, s), BF16),
        grid=(b, d // w, s // tq),
        in_specs=[pl.BlockSpec((None, tq, w), lambda bi, p, i: (bi, i, p)),
                  pl.BlockSpec((None, s, w), lambda bi, p, i: (bi, 0, p)),
                  pl.BlockSpec((None, s // TKV, w, TKV), lambda bi, p, i: (bi, 0, p, 0))],
        out_specs=pl.BlockSpec((None, w, tq), lambda bi, p, i: (bi, p, i)),
        compiler_params=_cparams(("parallel", "parallel", "arbitrary")),
        name="sb_attn",
    )(q, k, vt)


DEN_ROWS = 16


def _mla_kernel(q_ref, k_ref, vt_ref, o_ref, *, tq):
    tk = TKV
    i = pl.program_id(2)
    nsub = tq // tk
    krow, qcol = _kv_iotas(tk, tq)
    shift = CHUNK.bit_length() - 1
    qs = tuple(q_ref[:, hh * LANES:(hh + 1) * LANES] for hh in range(2))

    def scores(j, diag):
        r0 = pl.multiple_of(j * tk, tk)
        if diag is not None:
            allowed = (jnp.right_shift(krow + diag * tk, shift) <= jnp.right_shift(qcol, shift))
        out = []
        for hh in range(2):
            s = lax.dot_general(k_ref[pl.ds(r0, tk), hh * LANES:(hh + 1) * LANES], qs[hh],
                                NT_DIMS, preferred_element_type=F32)
            if diag is not None:
                s = jnp.where(allowed, s, NEG)
            out.append((s, jnp.max(s, axis=0, keepdims=True)))
        return tuple(out)

    def update(j, ss, carry):
        out = []
        for hh in range(2):
            m, acc = carry[hh]
            s, smax = ss[hh]
            m_new = jnp.maximum(m, smax)
            alpha = jnp.exp2(m - m_new)
            p = jnp.exp2(s - m_new).astype(BF16)
            vb = jnp.concatenate([vt_ref[j, hh * MLA_V:(hh + 1) * MLA_V, :], ones], axis=0)
            acc = alpha * acc + jnp.dot(vb, p, preferred_element_type=F32)
            out.append((m_new, acc))
        return tuple(out)

    ones = jnp.ones((DEN_ROWS, tk), BF16)
    init = tuple((jnp.full((1, tq), NEG, F32), jnp.zeros((MLA_V + DEN_ROWS, tq), F32))
                 for _ in range(2))
    base = i * nsub
    carry = init
    ss = scores(base + nsub - 1, nsub - 1)
    for d in range(nsub - 1, -1, -1):
        s_ahead = scores(jnp.maximum(base + d - 1, 0), d - 1 if d > 0 else None)
        carry = update(base + d, ss, carry)
        ss = s_ahead

    def body(jj, state):
        carry, ss = state
        j = base - 1 - jj
        s_ahead = scores(jnp.maximum(j - 1, 0), None)
        return update(j, ss, carry), s_ahead

    carry, _ = lax.fori_loop(0, base, body, (carry, ss))
    for hh in range(2):
        acc = carry[hh][1]
        o_ref[hh * MLA_V:(hh + 1) * MLA_V, :] = (
            acc[:MLA_V] / acc[MLA_V:MLA_V + 1]).astype(o_ref.dtype)


def _mla_attention(q, k, vt):
    b, s, _ = q.shape
    tq = 2 * TKV
    return pl.pallas_call(
        functools.partial(_mla_kernel, tq=tq),
        out_shape=jax.ShapeDtypeStruct((b, D_MLA, s), BF16),
        grid=(b, MLA_HEADS // 2, s // tq),
        in_specs=[pl.BlockSpec((None, tq, 2 * LANES), lambda bi, p, i: (bi, i, p)),
                  pl.BlockSpec((None, s, 2 * LANES), lambda bi, p, i: (bi, 0, p)),
                  pl.BlockSpec((None, s // TKV, LANES, TKV), lambda bi, p, i: (bi, 0, p, 0))],
        out_specs=pl.BlockSpec((None, LANES, tq), lambda bi, p, i: (bi, p, i)),
        compiler_params=_cparams(("parallel", "parallel", "arbitrary")),
        name="mla_attn",
    )(q, k, vt)


CK_QB = 2 * CHUNK
CK_KEYS = CK_WINDOW + CHUNK
CK_PAD = CK_LEFT_CHUNKS * CHUNK


def _ck_kernel(q_ref, k_ref, vt_ref, bias_ref, o_ref, *, n_blocks, heads):
    lo = lax.broadcasted_iota(jnp.int32, (CK_QB, LANES), 1) < HEAD_DIM
    krow = lax.broadcasted_iota(jnp.int32, (CK_KEYS, CK_QB), 0)
    n_slabs = CK_KEYS // CK_QB

    ones = jnp.ones((DEN_ROWS, CK_KEYS), BF16)

    def scores(c2, masked):
        r0 = pl.multiple_of(c2 * CK_QB, CK_QB)
        if masked:
            valid = krow >= CK_PAD - c2 * CK_QB
        out = []
        for pb in range(heads // 2):
            q = q_ref[pl.ds(r0, CK_QB), pb * LANES:(pb + 1) * LANES]
            kw = k_ref[pl.ds(r0, CK_KEYS), pb * LANES:(pb + 1) * LANES]
            zq = jnp.zeros_like(q)
            for hl in range(2):
                qh = jnp.where(lo, q, zq) if hl == 0 else jnp.where(lo, zq, q)
                s = (lax.dot_general(kw, qh, NT_DIMS, preferred_element_type=F32)
                     + bias_ref[2 * pb + hl])
                if masked:
                    s = jnp.where(valid, s, NEG)
                out.append((s, jnp.max(s, axis=0, keepdims=True)))
        return tuple(out)

    def finish(c2, ss):
        for hh in range(heads):
            s, m = ss[hh]
            p = jnp.exp2(s - m).astype(BF16)
            vw = jnp.concatenate(
                [jnp.concatenate([vt_ref[c2 + sb, hh * HEAD_DIM:(hh + 1) * HEAD_DIM, :]
                                  for sb in range(n_slabs)], axis=1), ones], axis=0)
            acc = jnp.dot(vw, p, preferred_element_type=F32)
            o_ref[c2, hh * HEAD_DIM:(hh + 1) * HEAD_DIM, :] = (
                acc[:HEAD_DIM] / acc[HEAD_DIM:HEAD_DIM + 1]).astype(o_ref.dtype)

    def sweep(lo_blk, hi_blk, masked):
        def body(c2, ss):
            ahead = scores(jnp.minimum(c2 + 1, hi_blk - 1), masked)
            finish(c2, ss)
            return ahead

        lax.fori_loop(lo_blk, hi_blk, body, scores(lo_blk, masked))

    n_masked = CK_PAD // CK_QB
    sweep(0, n_masked, True)
    sweep(n_masked, n_blocks, False)


def _ck_attention(q, kpad, vtpad, bias):
    b, s, d = q.shape
    sp = kpad.shape[1]
    nb = s // CK_QB
    heads = d // HEAD_DIM
    kern = functools.partial(_ck_kernel, n_blocks=nb, heads=heads)
    return pl.pallas_call(
        kern,
        out_shape=jax.ShapeDtypeStruct((b, nb, d, CK_QB), BF16),
        grid=(b,),
        in_specs=[pl.BlockSpec((None, s, d), lambda bi: (bi, 0, 0)),
                  pl.BlockSpec((None, sp, d), lambda bi: (bi, 0, 0)),
                  pl.BlockSpec((None, sp // CK_QB, d, CK_QB), lambda bi: (bi, 0, 0, 0)),
                  pl.BlockSpec((heads, CK_KEYS, CK_QB), lambda bi: (0, 0, 0))],
        out_specs=pl.BlockSpec((None, nb, d, CK_QB), lambda bi: (bi, 0, 0, 0)),
        compiler_params=_cparams(("parallel",)),
        name="ck_attn",
    )(q, kpad, vtpad, bias)


def _outproj_kernel(osb_ref, omla_ref, ock_ref, x_ref, mod_ref, gn_ref, wout_ref, nffn_ref,
                    *rest, moe):
    if moe:
        wr_ref, xo_ref, h2_ref, gates_ref = rest
    else:
        xo_ref, h2_ref = rest
    gn = gn_ref[...]

    def gnorm_t(ot, c0, w):
        ot = ot.astype(F32)
        ms = jnp.sum(ot * ot, axis=0, keepdims=True) * (1.0 / w)
        return ((ot * lax.rsqrt(ms + EPS)).T * gn[:, c0:c0 + w]).astype(BF16)

    ock = jnp.concatenate([ock_ref[sb] for sb in range(ock_ref.shape[0])], axis=1)
    merged = jnp.concatenate([gnorm_t(osb_ref[...], 0, D_SB), gnorm_t(omla_ref[...], D_SB, D_MLA),
                              gnorm_t(ock, D_SB + D_MLA, D_CK)], axis=-1)
    y = jnp.dot(merged, wout_ref[...], preferred_element_type=F32)
    xn = x_ref[...] + mod_ref[2:3, :] * y
    xo_ref[...] = xn
    h2 = _rms(xn, nffn_ref[...], D_MODEL) * (1.0 + mod_ref[4:5, :]) + mod_ref[3:4, :]
    h2_ref[...] = h2.astype(h2_ref.dtype)
    if moe:
        wr = wr_ref[...]
        w_hi = wr.astype(BF16)
        w_lo = (wr - w_hi.astype(F32)).astype(BF16)
        h_hi = h2.astype(BF16)
        h_lo = (h2 - h_hi.astype(F32)).astype(BF16)
        logits = (jnp.dot(h_hi, w_hi, preferred_element_type=F32)
                  + (jnp.dot(h_hi, w_lo, preferred_element_type=F32)
                     + jnp.dot(h_lo, w_hi, preferred_element_type=F32)))
        lane = lax.broadcasted_iota(jnp.int32, logits.shape, 1).astype(F32)
        logits = jnp.where(lane < N_EXPERTS, logits, -jnp.inf)
        m1 = jnp.max(logits, axis=-1, keepdims=True)
        i1 = jnp.min(jnp.where(logits == m1, lane, float(LANES)), axis=-1, keepdims=True)
        rest_l = jnp.where(lane == i1, -jnp.inf, logits)
        m2 = jnp.max(rest_l, axis=-1, keepdims=True)
        i2 = jnp.min(jnp.where(rest_l == m2, lane, float(LANES)), axis=-1, keepdims=True)
        e2 = jnp.exp(m2 - m1)
        w1 = 1.0 / (1.0 + e2)
        w2 = e2 / (1.0 + e2)
        gates_ref[...] = jnp.where(lane == 0.0, i1, jnp.where(lane == 1.0, i2,
                                   jnp.where(lane == 2.0, w1, jnp.where(lane == 3.0, w2, 0.0))))


def _outproj(osb, omla, ock, x2, mod_l, gn, wout, nffn, wr, seq):
    t = x2.shape[0]
    tm = 512
    tpb = seq // tm
    moe = wr is not None
    full = lambda a: pl.BlockSpec(a.shape, lambda i: (0,) * a.ndim)
    row = lambda w: pl.BlockSpec((tm, w), lambda i: (i, 0))
    colt = lambda w: pl.BlockSpec((None, w, tm), lambda i: (i // tpb, 0, i % tpb))
    ins = [osb, omla, ock, x2, mod_l, gn, wout, nffn]
    slabs = pl.BlockSpec((None, tm // CK_QB, D_CK, CK_QB), lambda i: (i // tpb, i % tpb, 0, 0))
    in_specs = [colt(D_SB), colt(D_MLA), slabs, row(D_MODEL),
                pl.BlockSpec((None, 6, D_MODEL), lambda i: (i // tpb, 0, 0)),
                full(gn), full(wout), full(nffn)]
    out_shape = [jax.ShapeDtypeStruct((t, D_MODEL), F32),
                 jax.ShapeDtypeStruct((t, D_MODEL), F32 if moe else BF16)]
    out_specs = [row(D_MODEL), row(D_MODEL)]
    if moe:
        ins.append(wr)
        in_specs.append(full(wr))
        out_shape.append(jax.ShapeDtypeStruct((t, LANES), F32))
        out_specs.append(row(LANES))
    return pl.pallas_call(
        functools.partial(_outproj_kernel, moe=moe),
        out_shape=tuple(out_shape),
        grid=(t // tm,),
        in_specs=in_specs,
        out_specs=tuple(out_specs),
        compiler_params=_cparams(("parallel",)),
        name="outproj_moe" if moe else "outproj",
    )(*ins)


FFN_TM = 512
FFN_TF = 1408


def _swiglu_step(h, wg_ref, wu_ref, wd_ref):
    g = jnp.dot(h, wg_ref[...], preferred_element_type=F32)
    u = jnp.dot(h, wu_ref[...], preferred_element_type=F32)
    a = g * jax.nn.sigmoid(g) * u
    return jnp.dot(a.astype(BF16), wd_ref[...], preferred_element_type=F32)


def _ffn_kernel(h_ref, x_ref, mod_ref, wg_ref, wu_ref, wd_ref, o_ref, acc_ref):
    f = pl.program_id(1)

    @pl.when(f == 0)
    def _():
        acc_ref[...] = jnp.zeros_like(acc_ref)

    acc_ref[...] += _swiglu_step(h_ref[...], wg_ref, wu_ref, wd_ref)

    @pl.when(f == pl.num_programs(1) - 1)
    def _():
        o_ref[...] = x_ref[...] + mod_ref[5:6, :] * acc_ref[...]


def _ffn(h2, x2, mod_l, wg, wu, wd, seq):
    t = h2.shape[0]
    tm, tf = FFN_TM, FFN_TF
    tpb = seq // tm
    rmap = lambda i, f: (i, 0)
    return pl.pallas_call(
        _ffn_kernel,
        out_shape=jax.ShapeDtypeStruct((t, D_MODEL), F32),
        grid=(t // tm, D_FF // tf),
        in_specs=[pl.BlockSpec((tm, D_MODEL), rmap), pl.BlockSpec((tm, D_MODEL), rmap),
                  pl.BlockSpec((None, 6, D_MODEL), lambda i, f: (i // tpb, 0, 0)),
                  pl.BlockSpec((D_MODEL, tf), lambda i, f: (0, f)),
                  pl.BlockSpec((D_MODEL, tf), lambda i, f: (0, f)),
                  pl.BlockSpec((tf, D_MODEL), lambda i, f: (f, 0))],
        out_specs=pl.BlockSpec((tm, D_MODEL), rmap),
        scratch_shapes=[pltpu.VMEM((tm, D_MODEL), F32)],
        compiler_params=_cparams(("parallel", "arbitrary")),
        name="dense_ffn",
    )(h2, x2, mod_l, wg, wu, wd)


ROUTE_TM = 512


def _rank_kernel(route_ref, rank_ref, count_ref, base_ref):
    i = pl.program_id(0)

    @pl.when(i == 0)
    def _():
        base_ref[...] = jnp.zeros_like(base_ref)

    tm = route_ref.shape[0]
    route = route_ref[...]
    lane = lax.broadcasted_iota(jnp.int32, (tm, LANES), 1).astype(F32)
    sel1 = lane == route[:, 0:1]
    sel2 = lane == route[:, 1:2]
    hot = jnp.where(sel1 | sel2, 1.0, 0.0)
    r = lax.broadcasted_iota(jnp.int32, (tm, tm), 0)
    c = lax.broadcasted_iota(jnp.int32, (tm, tm), 1)
    before = jnp.where(c < r, 1.0, 0.0).astype(BF16)
    seen = jnp.dot(before, hot.astype(BF16), preferred_element_type=F32) + base_ref[...]
    r1 = jnp.sum(jnp.where(sel1, seen, 0.0), axis=-1, keepdims=True)
    r2 = jnp.sum(jnp.where(sel2, seen, 0.0), axis=-1, keepdims=True)
    rank_ref[...] = jnp.where(lane == 0.0, r1, jnp.where(lane == 1.0, r2, 0.0))
    base_ref[...] += jnp.sum(hot, axis=0, keepdims=True)
    count_ref[...] = base_ref[...]


def _moe_rank(route):
    t = route.shape[0]
    tm = ROUTE_TM
    return pl.pallas_call(
        _rank_kernel,
        out_shape=(jax.ShapeDtypeStruct((t, LANES), F32), jax.ShapeDtypeStruct((1, LANES), F32)),
        grid=(t // tm,),
        in_specs=[pl.BlockSpec((tm, LANES), lambda i: (i, 0))],
        out_specs=(pl.BlockSpec((tm, LANES), lambda i: (i, 0)),
                   pl.BlockSpec((1, LANES), lambda i: (0, 0))),
        scratch_shapes=[pltpu.VMEM((1, LANES), F32)],
        compiler_params=_cparams(("arbitrary",)),
        name="moe_rank",
    )(route)


def _dispatch_kernel(pos_ref, h_ref, xs_in_ref, xs_ref, sem):
    del xs_in_ref
    td = pos_ref.shape[1]

    def issue(r, carry):
        src = h_ref.at[pl.ds(r, 1), :]
        for k in range(2):
            pltpu.make_async_copy(src, xs_ref.at[pl.ds(pos_ref[k, r], 1), :],
                                  sem.at[k]).start(priority=k)
        return carry

    lax.fori_loop(0, td, issue, 0, unroll=8)
    for k in range(2):
        pltpu.make_async_copy(h_ref, xs_ref.at[pl.ds(0, td), :], sem.at[k]).wait()


def _moe_dispatch(pos, h2, n_rows):
    nt, _, td = pos.shape
    xs0 = jnp.zeros((n_rows, D_MODEL), F32)
    return pl.pallas_call(
        _dispatch_kernel,
        out_shape=jax.ShapeDtypeStruct((n_rows, D_MODEL), F32),
        grid=(nt,),
        in_specs=[pl.BlockSpec((None, 2, td), lambda i: (i, 0, 0), memory_space=pltpu.SMEM),
                  pl.BlockSpec((td, D_MODEL), lambda i: (i, 0)),
                  pl.BlockSpec(memory_space=pl.ANY)],
        out_specs=pl.BlockSpec(memory_space=pl.ANY),
        scratch_shapes=[pltpu.SemaphoreType.DMA((2,))],
        input_output_aliases={2: 0},
        compiler_params=_cparams(("arbitrary",)),
        name="moe_dispatch",
    )(pos, h2, xs0)


def _group_ffn_kernel(te_ref, nu_ref, xs_ref, wg_ref, wu_ref, wd_ref, y_ref, xb_ref, acc_ref):
    del te_ref
    i = pl.program_id(0)
    f = pl.program_id(1)

    @pl.when(i < nu_ref[0])
    def _():
        @pl.when(f == 0)
        def _():
            xb_ref[...] = xs_ref[...].astype(BF16)
            acc_ref[...] = jnp.zeros_like(acc_ref)

        acc_ref[...] += _swiglu_step(xb_ref[...], wg_ref, wu_ref, wd_ref)

        @pl.when(f == pl.num_programs(1) - 1)
        def _():
            y_ref[...] = acc_ref[...]

    @pl.when((i >= nu_ref[0]) & (f == pl.num_programs(1) - 1))
    def _():
        y_ref[...] = jnp.zeros_like(y_ref)


def _moe_group_ffn(tile_expert, n_used, xs, wg, wu, wd):
    n_rows = xs.shape[0]
    tm, tf = FFN_TM, FFN_TF
    nf = D_FF // tf
    row = lambda i, f, te, nu: (jnp.minimum(i, nu[0] - 1), 0)
    fe = lambda i, f, nu: jnp.where(i < nu[0], f, nf - 1)
    return pl.pallas_call(
        _group_ffn_kernel,
        out_shape=jax.ShapeDtypeStruct((n_rows, D_MODEL), F32),
        grid_spec=pltpu.PrefetchScalarGridSpec(
            num_scalar_prefetch=2,
            grid=(n_rows // tm, nf),
            in_specs=[pl.BlockSpec((tm, D_MODEL), row),
                      pl.BlockSpec((None, D_MODEL, tf), lambda i, f, te, nu: (te[i], 0, fe(i, f, nu))),
                      pl.BlockSpec((None, D_MODEL, tf), lambda i, f, te, nu: (te[i], 0, fe(i, f, nu))),
                      pl.BlockSpec((None, tf, D_MODEL), lambda i, f, te, nu: (te[i], fe(i, f, nu), 0))],
            out_specs=pl.BlockSpec((tm, D_MODEL), lambda i, f, te, nu: (i, 0)),
            scratch_shapes=[pltpu.VMEM((tm, D_MODEL), BF16), pltpu.VMEM((tm, D_MODEL), F32)]),
        compiler_params=_cparams(("arbitrary", "arbitrary")),
        name="moe_group_ffn",
    )(tile_expert, n_used, xs, wg, wu, wd)


def _combine_kernel(pos_ref, route_ref, x_ref, mod_ref, y_ref, o_ref, buf_ref, sem):
    tc = pos_ref.shape[1]

    def issue(r, carry):
        for k in range(2):
            pltpu.make_async_copy(y_ref.at[pl.ds(pos_ref[k, r], 1), :],
                                  buf_ref.at[k, pl.ds(r, 1), :], sem.at[k]).start(priority=k)
        return carry

    lax.fori_loop(0, tc, issue, 0, unroll=8)
    for k in range(2):
        pltpu.make_async_copy(y_ref.at[pl.ds(0, tc), :], buf_ref.at[k], sem.at[k]).wait()
    route = route_ref[...]
    y = route[:, 2:3] * buf_ref[0] + route[:, 3:4] * buf_ref[1]
    o_ref[...] = x_ref[...] + mod_ref[5:6, :] * y


def _moe_combine(pos, route, x2, mod_l, y, seq):
    nt, _, tc = pos.shape
    t = x2.shape[0]
    tpb = seq // tc
    return pl.pallas_call(
        _combine_kernel,
        out_shape=jax.ShapeDtypeStruct((t, D_MODEL), F32),
        grid=(nt,),
        in_specs=[pl.BlockSpec((None, 2, tc), lambda i: (i, 0, 0), memory_space=pltpu.SMEM),
                  pl.BlockSpec((tc, LANES), lambda i: (i, 0)),
                  pl.BlockSpec((tc, D_MODEL), lambda i: (i, 0)),
                  pl.BlockSpec((None, 6, D_MODEL), lambda i: (i // tpb, 0, 0)),
                  pl.BlockSpec(memory_space=pl.ANY)],
        out_specs=pl.BlockSpec((tc, D_MODEL), lambda i: (i, 0)),
        scratch_shapes=[pltpu.VMEM((2, tc, D_MODEL), F32), pltpu.SemaphoreType.DMA((2,))],
        compiler_params=_cparams(("arbitrary",)),
        name="moe_combine",
    )(pos, route, x2, mod_l, y)


def _moe(h2, x2, mod_l, route, wg, wu, wd, seq):
    t = h2.shape[0]
    ne = wg.shape[0]
    tm = FFN_TM
    n_tiles = (2 * t) // tm + ne
    n_rows = n_tiles * tm
    rank, count = _moe_rank(route)
    counts = count[0, :ne].astype(jnp.int32)
    tiles_per = (counts + tm - 1) // tm
    tile_end = jnp.cumsum(tiles_per)
    start = (tile_end - tiles_per) * tm
    n_used = tile_end[-1:]
    tile_ids = jnp.minimum(jnp.arange(n_tiles, dtype=jnp.int32), n_used[0] - 1)
    tile_expert = jnp.sum((tile_ids[:, None] >= tile_end[None, :]).astype(jnp.int32), axis=1)
    experts = route[:, 0:2].astype(jnp.int32)
    pos = jnp.take(start, experts) + rank[:, 0:2].astype(jnp.int32)
    pos = pos.reshape(t // ROUTE_TM, ROUTE_TM, 2).transpose(0, 2, 1)
    xs = _moe_dispatch(pos, h2, n_rows)
    y = _moe_group_ffn(tile_expert, n_used, xs, wg, wu, wd)
    return _moe_combine(pos, route, x2, mod_l, y, seq)


def _rot_half_cols(w):
    half = w.shape[-1] // 2
    return jnp.concatenate([-w[..., half:], w[..., :half]], axis=-1)


def _pad_cols(w, left, total):
    return jnp.pad(w, ((0, 0), (left, total - left - w.shape[-1])))


def _layout_w_in(w_in):
    sizes = (D_SB, D_SB, D_SB, MLA_Q_RANK, MLA_KV_RANK, MLA_ROPE, D_CK, D_CK, D_CK)
    splits = [int(v) for v in np.cumsum(sizes)[:-1]]
    sbq, sbk, sbv, cq, ckv, kpe, ckq, ckk, ckv2 = jnp.split(w_in, splits, axis=-1)
    kpa = _pad_cols(kpe, MLA_NOPE, LANES)
    kpb = _pad_cols(_rot_half_cols(kpe), MLA_NOPE, LANES)
    main = jnp.concatenate([sbq, sbk, cq, ckv, kpa, kpb, ckq, ckk], axis=-1)
    return main.astype(BF16), jnp.concatenate([sbv, ckv2], axis=-1).T.astype(BF16)


def _layout_w_q_up(w):
    r = w.shape[0]
    w3 = w.reshape(r, MLA_HEADS, MLA_QK)
    nope, pe = w3[..., :MLA_NOPE], w3[..., MLA_NOPE:]
    zpad = jnp.zeros((r, MLA_HEADS, LANES - MLA_QK), w.dtype)
    a = jnp.concatenate([nope, pe, zpad], axis=-1).reshape(r, MLA_HEADS * LANES)
    b = jnp.concatenate([jnp.zeros_like(nope), _rot_half_cols(pe), zpad], axis=-1)
    return jnp.concatenate([a, b.reshape(r, MLA_HEADS * LANES)], axis=-1).astype(BF16)


def _layout_w_kv_up(w):
    r = w.shape[0]
    w3 = w.reshape(r, MLA_HEADS, MLA_NOPE + MLA_V)
    kn = jnp.pad(w3[..., :MLA_NOPE], ((0, 0), (0, 0), (0, LANES - MLA_NOPE)))
    return (kn.reshape(r, MLA_HEADS * LANES).astype(BF16),
            w3[..., MLA_NOPE:].reshape(r, D_MLA).T.astype(BF16))


def _ck_bias_table(rel_bias):
    heads = rel_bias.shape[0]
    n_g = CHUNK + CK_KEYS - 1
    blocks = []
    for u in range(CK_QB // CHUNK):
        d = np.arange(n_g)
        idx = np.clip(d - (CK_KEYS - 1) + CK_PAD + CHUNK * u, -REL_CLIP, REL_CLIP) + REL_CLIP
        g = rel_bias.astype(F32)[:, idx] * LOG2E
        hank = jnp.tile(g, (1, CHUNK + 1))[:, :CHUNK * (n_g + 1)]
        hank = hank.reshape(heads, CHUNK, n_g + 1)[:, :, :CK_KEYS]
        blk = jnp.transpose(hank[:, :, ::-1], (0, 2, 1))
        j = np.arange(CK_KEYS) - CHUNK * u
        in_band = ((j >= 0) & (j < CK_WINDOW))[None, :, None]
        blocks.append(jnp.where(in_band, blk, NEG))
    return jnp.concatenate(blocks, axis=-1)


def kernel(x, c, positions, ada_w, ada_b, norm_mix, norm_ffn, w_in, mla_q_norm, w_q_up, mla_kv_norm, w_kv_up, mla_q_qknorm, mla_k_qknorm, ck_q_qknorm, ck_k_qknorm, ck_rel_bias, group_out_norm, w_out, ffn_w_gate, ffn_w_up, ffn_w_down, moe_router, moe_w_gate, moe_w_up, moe_w_down):
    bsz, seq, d = x.shape
    depth = ada_w.shape[0]
    t = bsz * seq
    x2 = x.reshape(t, d)

    mod = _ada(c, ada_w, ada_b).reshape(depth, bsz, 6, d)
    rc, rs = _rope_tables(positions)
    pad128 = lambda g: jnp.pad(g, (0, LANES - g.shape[0])).reshape(1, LANES)
    pair = lambda g: jnp.concatenate([g, g]).reshape(1, LANES)

    for layer in range(depth):
        win, wsbvt = _layout_w_in(w_in[layer])
        wq = _layout_w_q_up(w_q_up[layer])
        wk, wvt = _layout_w_kv_up(w_kv_up[layer])
        sbq, sbk, sbvt, mq, mk, mvt, cq, ck, cvt = _inproj(
            x2, mod[layer], norm_mix[layer].reshape(1, d), win, wsbvt, rc, rs,
            mla_q_norm[layer].reshape(1, -1), wq, mla_kv_norm[layer].reshape(1, -1), wk, wvt,
            pad128(mla_q_qknorm[layer]), pad128(mla_k_qknorm[layer]),
            pair(ck_q_qknorm[layer]), pair(ck_k_qknorm[layer]), seq)

        r3 = lambda a: a.reshape(bsz, seq, a.shape[-1])
        o_sb = _sb_attention(r3(sbq), r3(sbk), sbvt)
        o_mla = _mla_attention(r3(mq), r3(mk), mvt)
        kpad = jnp.pad(r3(ck), ((0, 0), (CK_PAD, 0), (0, 0)))
        vtpad = jnp.pad(cvt, ((0, 0), (CK_PAD // CK_QB, 0), (0, 0), (0, 0)))
        o_ck = _ck_attention(r3(cq), kpad, vtpad, _ck_bias_table(ck_rel_bias[layer]))

        i = layer // 2
        is_moe = layer % 2 == 1
        wr = None
        if is_moe:
            wr = jnp.pad(moe_router[i], ((0, 0), (0, LANES - N_EXPERTS)))
        res = _outproj(o_sb, o_mla, o_ck, x2,
                       mod[layer], group_out_norm[layer].reshape(1, -1),
                       w_out[layer].astype(BF16), norm_ffn[layer].reshape(1, d), wr, seq)
        if is_moe:
            x2, h2, route = res
            x2 = _moe(h2, x2, mod[layer], route, moe_w_gate[i].astype(BF16),
                      moe_w_up[i].astype(BF16), moe_w_down[i].astype(BF16), seq)
        else:
            x2, h2 = res
            x2 = _ffn(h2, x2, mod[layer], ffn_w_gate[i].astype(BF16),
                      ffn_w_up[i].astype(BF16), ffn_w_down[i].astype(BF16), seq)
    return x2.reshape(bsz, seq, d)
```

```python
import functools

import numpy as np
import jax
import jax.numpy as jnp
from jax import lax
from jax.experimental import pallas as pl
from jax.experimental.pallas import tpu as pltpu

F32 = jnp.float32
BF16 = jnp.bfloat16

D_MODEL = 1024
CHUNK = 64
HEAD_DIM = 64
SB_HEADS = 4
MLA_HEADS = 8
MLA_Q_RANK = 256
MLA_KV_RANK = 128
MLA_NOPE = 64
MLA_ROPE = 32
MLA_V = 64
MLA_QK = MLA_NOPE + MLA_ROPE
ROPE_THETA = 10000.0
CK_HEADS = 4
CK_LEFT_CHUNKS = 8
CK_WINDOW = (CK_LEFT_CHUNKS + 1) * CHUNK
REL_CLIP = 128
D_SB = SB_HEADS * HEAD_DIM
D_MLA = MLA_HEADS * MLA_V
D_CK = CK_HEADS * HEAD_DIM
D_FF = 2816
N_EXPERTS = 8
EPS = 1e-6
NEG = -1e30

LANES = 128
VMEM_LIMIT = 56 * 1024 * 1024

C_SBQ, C_SBK = 0, 256
C_CQ = 512
C_CKV = 768
C_KPA = 896
C_KPB = 1024
C_CKQ, C_CKK = 1152, 1408
IN_EXT = 1664

NT_DIMS = (((1,), (1,)), ((), ()))
TN_DIMS = (((0,), (0,)), ((), ()))
TKV = 256
LOG2E = 1.4426950408889634


def _cparams(sem, vmem=VMEM_LIMIT):
    return pltpu.CompilerParams(dimension_semantics=sem, vmem_limit_bytes=vmem)


def _ada_kernel(c_ref, w_ref, b_ref, o_ref):
    c = c_ref[...]
    ca = c * jax.nn.sigmoid(c)
    o_ref[...] = jnp.dot(ca, w_ref[...], preferred_element_type=F32,
                         precision=lax.Precision.HIGHEST) + b_ref[...]


def _ada(c, ada_w, ada_b):
    depth, d, n = ada_w.shape
    b = c.shape[0]
    tn = 1536
    return pl.pallas_call(
        _ada_kernel,
        out_shape=jax.ShapeDtypeStruct((depth, b, n), F32),
        grid=(depth, n // tn),
        in_specs=[pl.BlockSpec((b, d), lambda l, j: (0, 0)),
                  pl.BlockSpec((None, d, tn), lambda l, j: (l, 0, j)),
                  pl.BlockSpec((None, 1, tn), lambda l, j: (l, 0, j))],
        out_specs=pl.BlockSpec((None, b, tn), lambda l, j: (l, 0, j)),
        compiler_params=_cparams(("parallel", "parallel")),
        name="ada_mod",
    )(c, ada_w, ada_b.reshape(depth, 1, n))


def _rope_kernel(pos_ref, invf_ref, c_ref, s_ref):
    ang = pos_ref[...] * invf_ref[...]
    lane = lax.broadcasted_iota(jnp.int32, ang.shape, 1)
    rope = (lane >= MLA_NOPE) & (lane < MLA_QK)
    c_ref[...] = jnp.where(rope, jnp.cos(ang), jnp.where(lane < MLA_NOPE, 1.0, 0.0))
    s_ref[...] = jnp.where(rope, jnp.sin(ang), 0.0)


def _rope_tables(positions):
    t = positions.size
    tm = 1024
    inv_freq = ROPE_THETA ** (-jnp.arange(0, MLA_ROPE, 2, dtype=F32) / MLA_ROPE)
    invf = jnp.zeros((1, LANES), F32)
    invf = invf.at[0, MLA_NOPE:MLA_NOPE + 16].set(inv_freq)
    invf = invf.at[0, MLA_NOPE + 16:MLA_QK].set(inv_freq)
    pos_b = jnp.broadcast_to(positions.reshape(t, 1).astype(F32), (t, LANES))
    return pl.pallas_call(
        _rope_kernel,
        out_shape=(jax.ShapeDtypeStruct((t, LANES), F32),
                   jax.ShapeDtypeStruct((t, LANES), F32)),
        grid=(t // tm,),
        in_specs=[pl.BlockSpec((tm, LANES), lambda i: (i, 0)),
                  pl.BlockSpec((1, LANES), lambda i: (0, 0))],
        out_specs=(pl.BlockSpec((tm, LANES), lambda i: (i, 0)),
                   pl.BlockSpec((tm, LANES), lambda i: (i, 0))),
        compiler_params=_cparams(("parallel",)),
        name="rope_tables",
    )(pos_b, invf)


def _rms(x, g, n):
    ms = jnp.sum(x * x, axis=-1, keepdims=True) * (1.0 / n)
    return x * lax.rsqrt(ms + EPS) * g


def _pair_rms(blk, g, scale):
    lane = lax.broadcasted_iota(jnp.int32, blk.shape, 1)
    lo = lane < HEAD_DIM
    sq = blk * blk
    s_all = jnp.sum(sq, axis=-1, keepdims=True)
    s_lo = jnp.sum(jnp.where(lo, sq, 0.0), axis=-1, keepdims=True)
    ms = jnp.where(lo, s_lo, s_all - s_lo) * (1.0 / HEAD_DIM)
    return blk * lax.rsqrt(ms + EPS) * (g * scale)


def _store_kv_blocks(ref, xt):
    slab = ref.shape[-1]
    for cblk in range(xt.shape[1] // slab):
        ref[cblk] = xt[:, cblk * slab:(cblk + 1) * slab].astype(ref.dtype)


def _inproj_kernel(x_ref, mod_ref, nw_ref, win_ref, wsbvt_ref, rc_ref, rs_ref, qn_ref, wq_ref,
                   kvn_ref, wk_ref, wvt_ref, gq_ref, gk_ref, gcq_ref, gck_ref,
                   sbq_ref, sbk_ref, sbvt_ref, mq_ref, mk_ref, mvt_ref,
                   cq_ref, ck_ref, cvt_ref):
    x = x_ref[...]
    xn = _rms(x, nw_ref[...], D_MODEL)
    h = (xn * (1.0 + mod_ref[1:2, :]) + mod_ref[0:1, :]).astype(BF16)
    proj = jnp.dot(h, win_ref[...], preferred_element_type=F32)

    sbq_ref[...] = (proj[:, C_SBQ:C_SBQ + D_SB] * (HEAD_DIM ** -0.5 * LOG2E)).astype(BF16)
    sbk_ref[...] = proj[:, C_SBK:C_SBK + D_SB].astype(BF16)
    vt = lax.dot_general(wsbvt_ref[...], h, NT_DIMS, preferred_element_type=F32)
    _store_kv_blocks(sbvt_ref, vt[:D_SB])
    _store_kv_blocks(cvt_ref, vt[D_SB:])

    rc = rc_ref[...]
    rs = rs_ref[...]
    cqn = _rms(proj[:, C_CQ:C_CQ + MLA_Q_RANK], qn_ref[...], MLA_Q_RANK).astype(BF16)
    qq = jnp.dot(cqn, wq_ref[...], preferred_element_type=F32)
    gq = gq_ref[...] * (MLA_QK ** -0.5 * LOG2E)
    nq = MLA_HEADS * LANES
    for hh in range(MLA_HEADS):
        a = qq[:, hh * LANES:(hh + 1) * LANES]
        b = qq[:, nq + hh * LANES:nq + (hh + 1) * LANES]
        qh = a * rc + b * rs
        mq_ref[:, hh * LANES:(hh + 1) * LANES] = _rms(qh, gq, MLA_QK).astype(BF16)

    ckvn = _rms(proj[:, C_CKV:C_CKV + MLA_KV_RANK], kvn_ref[...], MLA_KV_RANK).astype(BF16)
    kn = jnp.dot(ckvn, wk_ref[...], preferred_element_type=F32)
    _store_kv_blocks(mvt_ref, lax.dot_general(wvt_ref[...], ckvn, NT_DIMS,
                                              preferred_element_type=F32))
    kpe = proj[:, C_KPA:C_KPA + LANES] * rc + proj[:, C_KPB:C_KPB + LANES] * rs
    gk = gk_ref[...]
    for hh in range(MLA_HEADS):
        kh = kn[:, hh * LANES:(hh + 1) * LANES] + kpe
        mk_ref[:, hh * LANES:(hh + 1) * LANES] = _rms(kh, gk, MLA_QK).astype(BF16)

    gcq = gcq_ref[...]
    gck = gck_ref[...]
    for p in range(D_CK // LANES):
        qb = proj[:, C_CKQ + p * LANES:C_CKQ + (p + 1) * LANES]
        kb = proj[:, C_CKK + p * LANES:C_CKK + (p + 1) * LANES]
        cq_ref[:, p * LANES:(p + 1) * LANES] = _pair_rms(
            qb, gcq, HEAD_DIM ** -0.5 * LOG2E).astype(BF16)
        ck_ref[:, p * LANES:(p + 1) * LANES] = _pair_rms(kb, gck, 1.0).astype(BF16)


def _inproj(x2, mod_l, nw, win, wsbvt, rc, rs, qn, wq, kvn, wk, wvt, gq, gk, gcq, gck, seq):
    t = x2.shape[0]
    tm = 512
    tpb = seq // tm
    bsz = t // seq
    full = lambda a: pl.BlockSpec(a.shape, lambda i: (0,) * a.ndim)
    row = lambda w: pl.BlockSpec((tm, w), lambda i: (i, 0))
    rows = lambda w: (jax.ShapeDtypeStruct((t, w), BF16), row(w))
    kvt = lambda w, slab=TKV: (
        jax.ShapeDtypeStruct((bsz, seq // slab, w, slab), BF16),
        pl.BlockSpec((None, tm // slab, w, slab), lambda i: (i // tpb, i % tpb, 0, 0)))
    outs = [rows(D_SB), rows(D_SB), kvt(D_SB), rows(MLA_HEADS * LANES), rows(MLA_HEADS * LANES),
            kvt(D_MLA), rows(D_CK), rows(D_CK), kvt(D_CK, CK_QB)]
    return pl.pallas_call(
        _inproj_kernel,
        out_shape=tuple(o[0] for o in outs),
        grid=(t // tm,),
        in_specs=[row(D_MODEL),
                  pl.BlockSpec((None, 6, D_MODEL), lambda i: (i // tpb, 0, 0)),
                  full(nw), full(win), full(wsbvt), row(LANES), row(LANES), full(qn), full(wq),
                  full(kvn), full(wk), full(wvt), full(gq), full(gk), full(gcq), full(gck)],
        out_specs=tuple(o[1] for o in outs),
        compiler_params=_cparams(("parallel",)),
        name="inproj",
    )(x2, mod_l, nw, win, wsbvt, rc, rs, qn, wq, kvn, wk, wvt, gq, gk, gcq, gck)


def _kv_iotas(tk, tq):
    return (lax.broadcasted_iota(jnp.int32, (tk, tq), 0),
            lax.broadcasted_iota(jnp.int32, (tk, tq), 1))


SB_DEAD = 160.0


def _sb_kernel(q_ref, k_ref, vt_ref, o_ref, *, tq, heads):
    tk = TKV
    i = pl.program_id(2)
    lo = lax.broadcasted_iota(jnp.int32, (tq, LANES), 1) < HEAD_DIM
    qs = []
    for pb in range(heads // 2):
        q = q_ref[:, pb * LANES:(pb + 1) * LANES]
        zq = jnp.zeros_like(q)
        qs += [jnp.where(lo, q, zq), jnp.where(lo, zq, q)]
    krow, qcol = _kv_iotas(tk, tq)
    strict = krow < qcol
    ur, uc = _kv_iotas(tk, tk)
    later = jnp.where(uc > ur, 1.0, 0.0).astype(BF16)

    sign = jnp.uint32(0x80000000)

    def stage1(j, diag):
        r0 = pl.multiple_of(j * tk, tk)
        out = []
        for hh in range(heads):
            pb = hh // 2
            kb = k_ref[pl.ds(r0, tk), pb * LANES:(pb + 1) * LANES]
            y = lax.dot_general(kb, qs[hh], NT_DIMS, preferred_element_type=F32)
            neg_abs = pltpu.bitcast(pltpu.bitcast(y, jnp.uint32) | sign, F32)
            sp = jnp.maximum(y, 0.0) + jnp.log2(1.0 + jnp.exp2(neg_abs))
            spm = jnp.where(strict, sp, 0.0) if diag else sp
            tail = jnp.dot(later, spm.astype(BF16), preferred_element_type=F32)
            pre = y - sp - tail
            if diag:
                pre = jnp.where(strict, pre, NEG)
            out.append((pre, tail[0:1, :] + spm[0:1, :]))
        return tuple(out)

    def stage2(j, pres, carry):
        out = []
        for hh in range(heads):
            acc, car = carry[hh]
            pre, inc = pres[hh]
            w = jnp.exp2(pre - car)
            vb = vt_ref[j, hh * HEAD_DIM:(hh + 1) * HEAD_DIM, :]
            acc = acc + jnp.dot(vb, w.astype(BF16), preferred_element_type=F32)
            out.append((acc, car + inc))
        return tuple(out)

    init = tuple((jnp.zeros((HEAD_DIM, tq), F32), jnp.zeros((1, tq), F32))
                 for _ in range(heads))

    def cond(state):
        jj, _, _, more = state
        return (jj < i) & (more > 0)

    def body(state):
        jj, carry, pres, _ = state
        j = i - jj
        mass = functools.reduce(jnp.minimum, [carry[hh][1] + pres[hh][1] for hh in range(heads)])
        more = (jnp.min(mass) < SB_DEAD).astype(jnp.int32)
        ahead = stage1(j - 1, False)
        return jj + 1, stage2(j, pres, carry), ahead, more

    jj, carry, pres, _ = lax.while_loop(cond, body, (jnp.int32(0), init, stage1(i, True),
                                                     jnp.int32(1)))
    carry = stage2(i - jj, pres, carry)
    for hh in range(heads):
        o_ref[hh * HEAD_DIM:(hh + 1) * HEAD_DIM, :] = carry[hh][0].astype(o_ref.dtype)


def _sb_attention(q, k, vt):
    b, s, d = q.shape
    tq = TKV
    heads = 4
    w = heads * HEAD_DIM
    return pl.pallas_call(
        functools.partial(_sb_kernel, tq=tq, heads=heads),
        out_shape=jax.ShapeDtypeStruct((b, d, s), BF16),
        grid=(b, d // w, s // tq),
        in_specs=[pl.BlockSpec((None, tq, w), lambda bi, p, i: (bi, i, p)),
                  pl.BlockSpec((None, s, w), lambda bi, p, i: (bi, 0, p)),
                  pl.BlockSpec((None, s // TKV, w, TKV), lambda bi, p, i: (bi, 0, p, 0))],
        out_specs=pl.BlockSpec((None, w, tq), lambda bi, p, i: (bi, p, i)),
        compiler_params=_cparams(("parallel", "parallel", "arbitrary")),
        name="sb_attn",
    )(q, k, vt)


DEN_ROWS = 16


def _mla_kernel(q_ref, k_ref, vt_ref, o_ref, *, tq):
    tk = TKV
    i = pl.program_id(2)
    nsub = tq // tk
    krow, qcol = _kv_iotas(tk, tq)
    shift = CHUNK.bit_length() - 1
    qs = tuple(q_ref[:, hh * LANES:(hh + 1) * LANES] for hh in range(2))

    def scores(j, diag):
        r0 = pl.multiple_of(j * tk, tk)
        if diag is not None:
            allowed = (jnp.right_shift(krow + diag * tk, shift) <= jnp.right_shift(qcol, shift))
        out = []
        for hh in range(2):
            s = lax.dot_general(k_ref[pl.ds(r0, tk), hh * LANES:(hh + 1) * LANES], qs[hh],
                                NT_DIMS, preferred_element_type=F32)
            if diag is not None:
                s = jnp.where(allowed, s, NEG)
            out.append((s, jnp.max(s, axis=0, keepdims=True)))
        return tuple(out)

    def update(j, ss, carry):
        out = []
        for hh in range(2):
            m, acc = carry[hh]
            s, smax = ss[hh]
            m_new = jnp.maximum(m, smax)
            alpha = jnp.exp2(m - m_new)
            p = jnp.exp2(s - m_new).astype(BF16)
            vb = jnp.concatenate([vt_ref[j, hh * MLA_V:(hh + 1) * MLA_V, :], ones], axis=0)
            acc = alpha * acc + jnp.dot(vb, p, preferred_element_type=F32)
            out.append((m_new, acc))
        return tuple(out)

    ones = jnp.ones((DEN_ROWS, tk), BF16)
    init = tuple((jnp.full((1, tq), NEG, F32), jnp.zeros((MLA_V + DEN_ROWS, tq), F32))
                 for _ in range(2))
    base = i * nsub
    carry = init
    ss = scores(base + nsub - 1, nsub - 1)
    for d in range(nsub - 1, -1, -1):
        s_ahead = scores(jnp.maximum(base + d - 1, 0), d - 1 if d > 0 else None)
        carry = update(base + d, ss, carry)
        ss = s_ahead

    def body(jj, state):
        carry, ss = state
        j = base - 1 - jj
        s_ahead = scores(jnp.maximum(j - 1, 0), None)
        return update(j, ss, carry), s_ahead

    carry, _ = lax.fori_loop(0, base, body, (carry, ss))
    for hh in range(2):
        acc = carry[hh][1]
        o_ref[hh * MLA_V:(hh + 1) * MLA_V, :] = (
            acc[:MLA_V] / acc[MLA_V:MLA_V + 1]).astype(o_ref.dtype)


def _mla_attention(q, k, vt):
    b, s, _ = q.shape
    tq = 2 * TKV
    return pl.pallas_call(
        functools.partial(_mla_kernel, tq=tq),
        out_shape=jax.ShapeDtypeStruct((b, D_MLA, s), BF16),
        grid=(b, MLA_HEADS // 2, s // tq),
        in_specs=[pl.BlockSpec((None, tq, 2 * LANES), lambda bi, p, i: (bi, i, p)),
                  pl.BlockSpec((None, s, 2 * LANES), lambda bi, p, i: (bi, 0, p)),
                  pl.BlockSpec((None, s // TKV, LANES, TKV), lambda bi, p, i: (bi, 0, p, 0))],
        out_specs=pl.BlockSpec((None, LANES, tq), lambda bi, p, i: (bi, p, i)),
        compiler_params=_cparams(("parallel", "parallel", "arbitrary")),
        name="mla_attn",
    )(q, k, vt)


CK_QB = 2 * CHUNK
CK_KEYS = CK_WINDOW + CHUNK
CK_PAD = CK_LEFT_CHUNKS * CHUNK


def _ck_kernel(q_ref, k_ref, vt_ref, bias_ref, o_ref, *, n_blocks, heads):
    lo = lax.broadcasted_iota(jnp.int32, (CK_QB, LANES), 1) < HEAD_DIM
    krow = lax.broadcasted_iota(jnp.int32, (CK_KEYS, CK_QB), 0)
    n_slabs = CK_KEYS // CK_QB

    ones = jnp.ones((DEN_ROWS, CK_KEYS), BF16)

    def scores(c2, masked):
        r0 = pl.multiple_of(c2 * CK_QB, CK_QB)
        if masked:
            valid = krow >= CK_PAD - c2 * CK_QB
        out = []
        for pb in range(heads // 2):
            q = q_ref[pl.ds(r0, CK_QB), pb * LANES:(pb + 1) * LANES]
            kw = k_ref[pl.ds(r0, CK_KEYS), pb * LANES:(pb + 1) * LANES]
            zq = jnp.zeros_like(q)
            for hl in range(2):
                qh = jnp.where(lo, q, zq) if hl == 0 else jnp.where(lo, zq, q)
                s = (lax.dot_general(kw, qh, NT_DIMS, preferred_element_type=F32)
                     + bias_ref[2 * pb + hl])
                if masked:
                    s = jnp.where(valid, s, NEG)
                out.append((s, jnp.max(s, axis=0, keepdims=True)))
        return tuple(out)

    def finish(c2, ss):
        for hh in range(heads):
            s, m = ss[hh]
            p = jnp.exp2(s - m).astype(BF16)
            vw = jnp.concatenate(
                [jnp.concatenate([vt_ref[c2 + sb, hh * HEAD_DIM:(hh + 1) * HEAD_DIM, :]
                                  for sb in range(n_slabs)], axis=1), ones], axis=0)
            acc = jnp.dot(vw, p, preferred_element_type=F32)
            o_ref[c2, hh * HEAD_DIM:(hh + 1) * HEAD_DIM, :] = (
                acc[:HEAD_DIM] / acc[HEAD_DIM:HEAD_DIM + 1]).astype(o_ref.dtype)

    def sweep(lo_blk, hi_blk, masked):
        def body(c2, ss):
            ahead = scores(jnp.minimum(c2 + 1, hi_blk - 1), masked)
            finish(c2, ss)
            return ahead

        lax.fori_loop(lo_blk, hi_blk, body, scores(lo_blk, masked))

    n_masked = CK_PAD // CK_QB
    sweep(0, n_masked, True)
    sweep(n_masked, n_blocks, False)


def _ck_attention(q, kpad, vtpad, bias):
    b, s, d = q.shape
    sp = kpad.shape[1]
    nb = s // CK_QB
    heads = d // HEAD_DIM
    kern = functools.partial(_ck_kernel, n_blocks=nb, heads=heads)
    return pl.pallas_call(
        kern,
        out_shape=jax.ShapeDtypeStruct((b, nb, d, CK_QB), BF16),
        grid=(b,),
        in_specs=[pl.BlockSpec((None, s, d), lambda bi: (bi, 0, 0)),
                  pl.BlockSpec((None, sp, d), lambda bi: (bi, 0, 0)),
                  pl.BlockSpec((None, sp // CK_QB, d, CK_QB), lambda bi: (bi, 0, 0, 0)),
                  pl.BlockSpec((heads, CK_KEYS, CK_QB), lambda bi: (0, 0, 0))],
        out_specs=pl.BlockSpec((None, nb, d, CK_QB), lambda bi: (bi, 0, 0, 0)),
        compiler_params=_cparams(("parallel",)),
        name="ck_attn",
    )(q, kpad, vtpad, bias)


def _outproj_kernel(osb_ref, omla_ref, ock_ref, x_ref, mod_ref, gn_ref, wout_ref, nffn_ref,
                    *rest, moe):
    if moe:
        wr_ref, xo_ref, h2_ref, gates_ref = rest
    else:
        xo_ref, h2_ref = rest
    gn = gn_ref[...]

    def gnorm_t(ot, c0, w):
        ot = ot.astype(F32)
        ms = jnp.sum(ot * ot, axis=0, keepdims=True) * (1.0 / w)
        return ((ot * lax.rsqrt(ms + EPS)).T * gn[:, c0:c0 + w]).astype(BF16)

    ock = jnp.concatenate([ock_ref[sb] for sb in range(ock_ref.shape[0])], axis=1)
    merged = jnp.concatenate([gnorm_t(osb_ref[...], 0, D_SB), gnorm_t(omla_ref[...], D_SB, D_MLA),
                              gnorm_t(ock, D_SB + D_MLA, D_CK)], axis=-1)
    y = jnp.dot(merged, wout_ref[...], preferred_element_type=F32)
    xn = x_ref[...] + mod_ref[2:3, :] * y
    xo_ref[...] = xn
    h2 = _rms(xn, nffn_ref[...], D_MODEL) * (1.0 + mod_ref[4:5, :]) + mod_ref[3:4, :]
    h2_ref[...] = h2.astype(h2_ref.dtype)
    if moe:
        wr = wr_ref[...]
        w_hi = wr.astype(BF16)
        w_lo = (wr - w_hi.astype(F32)).astype(BF16)
        h_hi = h2.astype(BF16)
        h_lo = (h2 - h_hi.astype(F32)).astype(BF16)
        logits = (jnp.dot(h_hi, w_hi, preferred_element_type=F32)
                  + (jnp.dot(h_hi, w_lo, preferred_element_type=F32)
                     + jnp.dot(h_lo, w_hi, preferred_element_type=F32)))
        lane = lax.broadcasted_iota(jnp.int32, logits.shape, 1).astype(F32)
        logits = jnp.where(lane < N_EXPERTS, logits, -jnp.inf)
        m1 = jnp.max(logits, axis=-1, keepdims=True)
        i1 = jnp.min(jnp.where(logits == m1, lane, float(LANES)), axis=-1, keepdims=True)
        rest_l = jnp.where(lane == i1, -jnp.inf, logits)
        m2 = jnp.max(rest_l, axis=-1, keepdims=True)
        i2 = jnp.min(jnp.where(rest_l == m2, lane, float(LANES)), axis=-1, keepdims=True)
        e2 = jnp.exp(m2 - m1)
        w1 = 1.0 / (1.0 + e2)
        w2 = e2 / (1.0 + e2)
        gates_ref[...] = jnp.where(lane == 0.0, i1, jnp.where(lane == 1.0, i2,
                                   jnp.where(lane == 2.0, w1, jnp.where(lane == 3.0, w2, 0.0))))


def _outproj(osb, omla, ock, x2, mod_l, gn, wout, nffn, wr, seq):
    t = x2.shape[0]
    tm = 512
    tpb = seq // tm
    moe = wr is not None
    full = lambda a: pl.BlockSpec(a.shape, lambda i: (0,) * a.ndim)
    row = lambda w: pl.BlockSpec((tm, w), lambda i: (i, 0))
    colt = lambda w: pl.BlockSpec((None, w, tm), lambda i: (i // tpb, 0, i % tpb))
    ins = [osb, omla, ock, x2, mod_l, gn, wout, nffn]
    slabs = pl.BlockSpec((None, tm // CK_QB, D_CK, CK_QB), lambda i: (i // tpb, i % tpb, 0, 0))
    in_specs = [colt(D_SB), colt(D_MLA), slabs, row(D_MODEL),
                pl.BlockSpec((None, 6, D_MODEL), lambda i: (i // tpb, 0, 0)),
                full(gn), full(wout), full(nffn)]
    out_shape = [jax.ShapeDtypeStruct((t, D_MODEL), F32),
                 jax.ShapeDtypeStruct((t, D_MODEL), F32 if moe else BF16)]
    out_specs = [row(D_MODEL), row(D_MODEL)]
    if moe:
        ins.append(wr)
        in_specs.append(full(wr))
        out_shape.append(jax.ShapeDtypeStruct((t, LANES), F32))
        out_specs.append(row(LANES))
    return pl.pallas_call(
        functools.partial(_outproj_kernel, moe=moe),
        out_shape=tuple(out_shape),
        grid=(t // tm,),
        in_specs=in_specs,
        out_specs=tuple(out_specs),
        compiler_params=_cparams(("parallel",)),
        name="outproj_moe" if moe else "outproj",
    )(*ins)


FFN_TM = 512
MXU_DEPTH = 256
FFN_CHUNKS = ((0, 5 * MXU_DEPTH), (5 * MXU_DEPTH, 6 * MXU_DEPTH))
RESIDENT = pl.Buffered(1)


def _swiglu(h, wg_ref, wu_ref, wd_ref):
    y = None
    for c0, w in FFN_CHUNKS:
        g = jnp.dot(h, wg_ref[:, c0:c0 + w], preferred_element_type=F32)
        u = jnp.dot(h, wu_ref[:, c0:c0 + w], preferred_element_type=F32)
        a = (g * jax.nn.sigmoid(g) * u).astype(BF16)
        part = jnp.dot(a, wd_ref[c0:c0 + w, :], preferred_element_type=F32)
        y = part if y is None else y + part
    return y


def _ffn_kernel(h_ref, x_ref, mod_ref, wg_ref, wu_ref, wd_ref, o_ref):
    o_ref[...] = x_ref[...] + mod_ref[5:6, :] * _swiglu(h_ref[...], wg_ref, wu_ref, wd_ref)


def _ffn(h2, x2, mod_l, wg, wu, wd, seq):
    t = h2.shape[0]
    tm = FFN_TM
    tpb = seq // tm
    rmap = lambda i: (i, 0)
    const = lambda a: pl.BlockSpec(a.shape, lambda i: (0, 0), pipeline_mode=RESIDENT)
    return pl.pallas_call(
        _ffn_kernel,
        out_shape=jax.ShapeDtypeStruct((t, D_MODEL), F32),
        grid=(t // tm,),
        in_specs=[pl.BlockSpec((tm, D_MODEL), rmap), pl.BlockSpec((tm, D_MODEL), rmap),
                  pl.BlockSpec((None, 6, D_MODEL), lambda i: (i // tpb, 0, 0)),
                  const(wg), const(wu), const(wd)],
        out_specs=pl.BlockSpec((tm, D_MODEL), rmap),
        compiler_params=_cparams(("parallel",)),
        name="dense_ffn",
    )(h2, x2, mod_l, wg, wu, wd)


ROUTE_TM = 512


def _rank_kernel(route_ref, rank_ref, count_ref, base_ref):
    i = pl.program_id(0)

    @pl.when(i == 0)
    def _():
        base_ref[...] = jnp.zeros_like(base_ref)

    tm = route_ref.shape[0]
    route = route_ref[...]
    lane = lax.broadcasted_iota(jnp.int32, (tm, LANES), 1).astype(F32)
    sel1 = lane == route[:, 0:1]
    sel2 = lane == route[:, 1:2]
    hot = jnp.where(sel1 | sel2, 1.0, 0.0)
    r = lax.broadcasted_iota(jnp.int32, (tm, tm), 0)
    c = lax.broadcasted_iota(jnp.int32, (tm, tm), 1)
    before = jnp.where(c < r, 1.0, 0.0).astype(BF16)
    seen = jnp.dot(before, hot.astype(BF16), preferred_element_type=F32) + base_ref[...]
    r1 = jnp.sum(jnp.where(sel1, seen, 0.0), axis=-1, keepdims=True)
    r2 = jnp.sum(jnp.where(sel2, seen, 0.0), axis=-1, keepdims=True)
    rank_ref[...] = jnp.where(lane == 0.0, r1, jnp.where(lane == 1.0, r2, 0.0))
    base_ref[...] += jnp.sum(hot, axis=0, keepdims=True)
    count_ref[...] = base_ref[...]


def _moe_rank(route):
    t = route.shape[0]
    tm = ROUTE_TM
    return pl.pallas_call(
        _rank_kernel,
        out_shape=(jax.ShapeDtypeStruct((t, LANES), F32), jax.ShapeDtypeStruct((1, LANES), F32)),
        grid=(t // tm,),
        in_specs=[pl.BlockSpec((tm, LANES), lambda i: (i, 0))],
        out_specs=(pl.BlockSpec((tm, LANES), lambda i: (i, 0)),
                   pl.BlockSpec((1, LANES), lambda i: (0, 0))),
        scratch_shapes=[pltpu.VMEM((1, LANES), F32)],
        compiler_params=_cparams(("arbitrary",)),
        name="moe_rank",
    )(route)


def _dispatch_kernel(pos_ref, h_ref, xs_in_ref, xs_ref, sem):
    del xs_in_ref
    td = pos_ref.shape[1]

    def issue(r, carry):
        src = h_ref.at[pl.ds(r, 1), :]
        for k in range(2):
            pltpu.make_async_copy(src, xs_ref.at[pl.ds(pos_ref[k, r], 1), :],
                                  sem.at[k]).start(priority=k)
        return carry

    lax.fori_loop(0, td, issue, 0, unroll=8)
    for k in range(2):
        pltpu.make_async_copy(h_ref, xs_ref.at[pl.ds(0, td), :], sem.at[k]).wait()


def _moe_dispatch(pos, h2, n_rows):
    nt, _, td = pos.shape
    xs0 = jnp.zeros((n_rows, D_MODEL), F32)
    return pl.pallas_call(
        _dispatch_kernel,
        out_shape=jax.ShapeDtypeStruct((n_rows, D_MODEL), F32),
        grid=(nt,),
        in_specs=[pl.BlockSpec((None, 2, td), lambda i: (i, 0, 0), memory_space=pltpu.SMEM),
                  pl.BlockSpec((td, D_MODEL), lambda i: (i, 0)),
                  pl.BlockSpec(memory_space=pl.ANY)],
        out_specs=pl.BlockSpec(memory_space=pl.ANY),
        scratch_shapes=[pltpu.SemaphoreType.DMA((2,))],
        input_output_aliases={2: 0},
        compiler_params=_cparams(("arbitrary",)),
        name="moe_dispatch",
    )(pos, h2, xs0)


def _group_ffn_kernel(te_ref, nu_ref, xs_ref, wg_ref, wu_ref, wd_ref, y_ref):
    del te_ref
    i = pl.program_id(0)

    @pl.when(i < nu_ref[0])
    def _():
        y_ref[...] = _swiglu(xs_ref[...].astype(BF16), wg_ref, wu_ref, wd_ref)

    @pl.when(i >= nu_ref[0])
    def _():
        y_ref[...] = jnp.zeros_like(y_ref)


def _moe_group_ffn(tile_expert, n_used, xs, wg, wu, wd):
    n_rows = xs.shape[0]
    tm = FFN_TM
    row = lambda i, te, nu: (jnp.minimum(i, nu[0] - 1), 0)
    wspec = lambda a: pl.BlockSpec((None,) + a.shape[1:], lambda i, te, nu: (te[i], 0, 0))
    return pl.pallas_call(
        _group_ffn_kernel,
        out_shape=jax.ShapeDtypeStruct((n_rows, D_MODEL), F32),
        grid_spec=pltpu.PrefetchScalarGridSpec(
            num_scalar_prefetch=2,
            grid=(n_rows // tm,),
            in_specs=[pl.BlockSpec((tm, D_MODEL), row), wspec(wg), wspec(wu), wspec(wd)],
            out_specs=pl.BlockSpec((tm, D_MODEL), lambda i, te, nu: (i, 0))),
        compiler_params=_cparams(("arbitrary",)),
        name="moe_group_ffn",
    )(tile_expert, n_used, xs, wg, wu, wd)


def _combine_kernel(pos_ref, route_ref, x_ref, mod_ref, y_ref, o_ref, buf_ref, sem):
    tc = pos_ref.shape[1]

    def issue(r, carry):
        for k in range(2):
            pltpu.make_async_copy(y_ref.at[pl.ds(pos_ref[k, r], 1), :],
                                  buf_ref.at[k, pl.ds(r, 1), :], sem.at[k]).start(priority=k)
        return carry

    lax.fori_loop(0, tc, issue, 0, unroll=8)
    for k in range(2):
        pltpu.make_async_copy(y_ref.at[pl.ds(0, tc), :], buf_ref.at[k], sem.at[k]).wait()
    route = route_ref[...]
    y = route[:, 2:3] * buf_ref[0] + route[:, 3:4] * buf_ref[1]
    o_ref[...] = x_ref[...] + mod_ref[5:6, :] * y


def _moe_combine(pos, route, x2, mod_l, y, seq):
    nt, _, tc = pos.shape
    t = x2.shape[0]
    tpb = seq // tc
    return pl.pallas_call(
        _combine_kernel,
        out_shape=jax.ShapeDtypeStruct((t, D_MODEL), F32),
        grid=(nt,),
        in_specs=[pl.BlockSpec((None, 2, tc), lambda i: (i, 0, 0), memory_space=pltpu.SMEM),
                  pl.BlockSpec((tc, LANES), lambda i: (i, 0)),
                  pl.BlockSpec((tc, D_MODEL), lambda i: (i, 0)),
                  pl.BlockSpec((None, 6, D_MODEL), lambda i: (i // tpb, 0, 0)),
                  pl.BlockSpec(memory_space=pl.ANY)],
        out_specs=pl.BlockSpec((tc, D_MODEL), lambda i: (i, 0)),
        scratch_shapes=[pltpu.VMEM((2, tc, D_MODEL), F32), pltpu.SemaphoreType.DMA((2,))],
        compiler_params=_cparams(("arbitrary",)),
        name="moe_combine",
    )(pos, route, x2, mod_l, y)


def _moe(h2, x2, mod_l, route, wg, wu, wd, seq):
    t = h2.shape[0]
    ne = wg.shape[0]
    tm = FFN_TM
    n_tiles = (2 * t) // tm + ne
    n_rows = n_tiles * tm
    rank, count = _moe_rank(route)
    counts = count[0, :ne].astype(jnp.int32)
    tiles_per = (counts + tm - 1) // tm
    tile_end = jnp.cumsum(tiles_per)
    start = (tile_end - tiles_per) * tm
    n_used = tile_end[-1:]
    tile_ids = jnp.minimum(jnp.arange(n_tiles, dtype=jnp.int32), n_used[0] - 1)
    tile_expert = jnp.sum((tile_ids[:, None] >= tile_end[None, :]).astype(jnp.int32), axis=1)
    experts = route[:, 0:2].astype(jnp.int32)
    pos = jnp.take(start, experts) + rank[:, 0:2].astype(jnp.int32)
    pos = pos.reshape(t // ROUTE_TM, ROUTE_TM, 2).transpose(0, 2, 1)
    xs = _moe_dispatch(pos, h2, n_rows)
    y = _moe_group_ffn(tile_expert, n_used, xs, wg, wu, wd)
    return _moe_combine(pos, route, x2, mod_l, y, seq)


def _rot_half_cols(w):
    half = w.shape[-1] // 2
    return jnp.concatenate([-w[..., half:], w[..., :half]], axis=-1)


def _pad_cols(w, left, total):
    return jnp.pad(w, ((0, 0), (left, total - left - w.shape[-1])))


def _layout_w_in(w_in):
    sizes = (D_SB, D_SB, D_SB, MLA_Q_RANK, MLA_KV_RANK, MLA_ROPE, D_CK, D_CK, D_CK)
    splits = [int(v) for v in np.cumsum(sizes)[:-1]]
    sbq, sbk, sbv, cq, ckv, kpe, ckq, ckk, ckv2 = jnp.split(w_in, splits, axis=-1)
    kpa = _pad_cols(kpe, MLA_NOPE, LANES)
    kpb = _pad_cols(_rot_half_cols(kpe), MLA_NOPE, LANES)
    main = jnp.concatenate([sbq, sbk, cq, ckv, kpa, kpb, ckq, ckk], axis=-1)
    return main.astype(BF16), jnp.concatenate([sbv, ckv2], axis=-1).T.astype(BF16)


def _layout_w_q_up(w):
    r = w.shape[0]
    w3 = w.reshape(r, MLA_HEADS, MLA_QK)
    nope, pe = w3[..., :MLA_NOPE], w3[..., MLA_NOPE:]
    zpad = jnp.zeros((r, MLA_HEADS, LANES - MLA_QK), w.dtype)
    a = jnp.concatenate([nope, pe, zpad], axis=-1).reshape(r, MLA_HEADS * LANES)
    b = jnp.concatenate([jnp.zeros_like(nope), _rot_half_cols(pe), zpad], axis=-1)
    return jnp.concatenate([a, b.reshape(r, MLA_HEADS * LANES)], axis=-1).astype(BF16)


def _layout_w_kv_up(w):
    r = w.shape[0]
    w3 = w.reshape(r, MLA_HEADS, MLA_NOPE + MLA_V)
    kn = jnp.pad(w3[..., :MLA_NOPE], ((0, 0), (0, 0), (0, LANES - MLA_NOPE)))
    return (kn.reshape(r, MLA_HEADS * LANES).astype(BF16),
            w3[..., MLA_NOPE:].reshape(r, D_MLA).T.astype(BF16))


def _ck_bias_table(rel_bias):
    heads = rel_bias.shape[0]
    n_g = CHUNK + CK_KEYS - 1
    blocks = []
    for u in range(CK_QB // CHUNK):
        d = np.arange(n_g)
        idx = np.clip(d - (CK_KEYS - 1) + CK_PAD + CHUNK * u, -REL_CLIP, REL_CLIP) + REL_CLIP
        g = rel_bias.astype(F32)[:, idx] * LOG2E
        hank = jnp.tile(g, (1, CHUNK + 1))[:, :CHUNK * (n_g + 1)]
        hank = hank.reshape(heads, CHUNK, n_g + 1)[:, :, :CK_KEYS]
        blk = jnp.transpose(hank[:, :, ::-1], (0, 2, 1))
        j = np.arange(CK_KEYS) - CHUNK * u
        in_band = ((j >= 0) & (j < CK_WINDOW))[None, :, None]
        blocks.append(jnp.where(in_band, blk, NEG))
    return jnp.concatenate(blocks, axis=-1)


def kernel(x, c, positions, ada_w, ada_b, norm_mix, norm_ffn, w_in, mla_q_norm, w_q_up, mla_kv_norm, w_kv_up, mla_q_qknorm, mla_k_qknorm, ck_q_qknorm, ck_k_qknorm, ck_rel_bias, group_out_norm, w_out, ffn_w_gate, ffn_w_up, ffn_w_down, moe_router, moe_w_gate, moe_w_up, moe_w_down):
    bsz, seq, d = x.shape
    depth = ada_w.shape[0]
    t = bsz * seq
    x2 = x.reshape(t, d)

    mod = _ada(c, ada_w, ada_b).reshape(depth, bsz, 6, d)
    rc, rs = _rope_tables(positions)
    pad128 = lambda g: jnp.pad(g, (0, LANES - g.shape[0])).reshape(1, LANES)
    pair = lambda g: jnp.concatenate([g, g]).reshape(1, LANES)

    for layer in range(depth):
        win, wsbvt = _layout_w_in(w_in[layer])
        wq = _layout_w_q_up(w_q_up[layer])
        wk, wvt = _layout_w_kv_up(w_kv_up[layer])
        sbq, sbk, sbvt, mq, mk, mvt, cq, ck, cvt = _inproj(
            x2, mod[layer], norm_mix[layer].reshape(1, d), win, wsbvt, rc, rs,
            mla_q_norm[layer].reshape(1, -1), wq, mla_kv_norm[layer].reshape(1, -1), wk, wvt,
            pad128(mla_q_qknorm[layer]), pad128(mla_k_qknorm[layer]),
            pair(ck_q_qknorm[layer]), pair(ck_k_qknorm[layer]), seq)

        r3 = lambda a: a.reshape(bsz, seq, a.shape[-1])
        o_sb = _sb_attention(r3(sbq), r3(sbk), sbvt)
        o_mla = _mla_attention(r3(mq), r3(mk), mvt)
        kpad = jnp.pad(r3(ck), ((0, 0), (CK_PAD, 0), (0, 0)))
        vtpad = jnp.pad(cvt, ((0, 0), (CK_PAD // CK_QB, 0), (0, 0), (0, 0)))
        o_ck = _ck_attention(r3(cq), kpad, vtpad, _ck_bias_table(ck_rel_bias[layer]))

        i = layer // 2
        is_moe = layer % 2 == 1
        wr = None
        if is_moe:
            wr = jnp.pad(moe_router[i], ((0, 0), (0, LANES - N_EXPERTS)))
        res = _outproj(o_sb, o_mla, o_ck, x2,
                       mod[layer], group_out_norm[layer].reshape(1, -1),
                       w_out[layer].astype(BF16), norm_ffn[layer].reshape(1, d), wr, seq)
        if is_moe:
            x2, h2, route = res
            x2 = _moe(h2, x2, mod[layer], route, moe_w_gate[i].astype(BF16),
                      moe_w_up[i].astype(BF16), moe_w_down[i].astype(BF16), seq)
        else:
            x2, h2 = res
            x2 = _ffn(h2, x2, mod[layer], ffn_w_gate[i].astype(BF16),
                      ffn_w_up[i].astype(BF16), ffn_w_down[i].astype(BF16), seq)
    return x2.reshape(bsz, seq, d)
```

```python
import functools

import numpy as np
import jax
import jax.numpy as jnp
from jax import lax
from jax.experimental import pallas as pl
from jax.experimental.pallas import tpu as pltpu

F32 = jnp.float32
BF16 = jnp.bfloat16

D_MODEL = 1024
CHUNK = 64
HEAD_DIM = 64
SB_HEADS = 4
MLA_HEADS = 8
MLA_Q_RANK = 256
MLA_KV_RANK = 128
MLA_NOPE = 64
MLA_ROPE = 32
MLA_V = 64
MLA_QK = MLA_NOPE + MLA_ROPE
ROPE_THETA = 10000.0
CK_HEADS = 4
CK_LEFT_CHUNKS = 8
CK_WINDOW = (CK_LEFT_CHUNKS + 1) * CHUNK
REL_CLIP = 128
D_SB = SB_HEADS * HEAD_DIM
D_MLA = MLA_HEADS * MLA_V
D_CK = CK_HEADS * HEAD_DIM
D_FF = 2816
N_EXPERTS = 8
EPS = 1e-6
NEG = -1e30

LANES = 128
VMEM_LIMIT = 56 * 1024 * 1024

C_SBQ, C_SBK = 0, 256
C_CQ = 512
C_CKV = 768
C_KPA = 896
C_KPB = 1024
C_CKQ, C_CKK = 1152, 1408
IN_EXT = 1664

NT_DIMS = (((1,), (1,)), ((), ()))
TN_DIMS = (((0,), (0,)), ((), ()))
TKV = 256
LOG2E = 1.4426950408889634


def _cparams(sem, vmem=VMEM_LIMIT):
    return pltpu.CompilerParams(dimension_semantics=sem, vmem_limit_bytes=vmem)


def _ada_kernel(c_ref, w_ref, b_ref, o_ref):
    c = c_ref[...]
    ca = c * jax.nn.sigmoid(c)
    o_ref[...] = jnp.dot(ca, w_ref[...], preferred_element_type=F32,
                         precision=lax.Precision.HIGHEST) + b_ref[...]


def _ada(c, ada_w, ada_b):
    depth, d, n = ada_w.shape
    b = c.shape[0]
    tn = 1536
    return pl.pallas_call(
        _ada_kernel,
        out_shape=jax.ShapeDtypeStruct((depth, b, n), F32),
        grid=(depth, n // tn),
        in_specs=[pl.BlockSpec((b, d), lambda l, j: (0, 0)),
                  pl.BlockSpec((None, d, tn), lambda l, j: (l, 0, j)),
                  pl.BlockSpec((None, 1, tn), lambda l, j: (l, 0, j))],
        out_specs=pl.BlockSpec((None, b, tn), lambda l, j: (l, 0, j)),
        compiler_params=_cparams(("parallel", "parallel")),
        name="ada_mod",
    )(c, ada_w, ada_b.reshape(depth, 1, n))


def _rope_kernel(pos_ref, invf_ref, c_ref, s_ref):
    ang = pos_ref[...] * invf_ref[...]
    lane = lax.broadcasted_iota(jnp.int32, ang.shape, 1)
    rope = (lane >= MLA_NOPE) & (lane < MLA_QK)
    c_ref[...] = jnp.where(rope, jnp.cos(ang), jnp.where(lane < MLA_NOPE, 1.0, 0.0))
    s_ref[...] = jnp.where(rope, jnp.sin(ang), 0.0)


def _rope_tables(positions):
    t = positions.size
    tm = 1024
    inv_freq = ROPE_THETA ** (-jnp.arange(0, MLA_ROPE, 2, dtype=F32) / MLA_ROPE)
    invf = jnp.zeros((1, LANES), F32)
    invf = invf.at[0, MLA_NOPE:MLA_NOPE + 16].set(inv_freq)
    invf = invf.at[0, MLA_NOPE + 16:MLA_QK].set(inv_freq)
    pos_b = jnp.broadcast_to(positions.reshape(t, 1).astype(F32), (t, LANES))
    return pl.pallas_call(
        _rope_kernel,
        out_shape=(jax.ShapeDtypeStruct((t, LANES), F32),
                   jax.ShapeDtypeStruct((t, LANES), F32)),
        grid=(t // tm,),
        in_specs=[pl.BlockSpec((tm, LANES), lambda i: (i, 0)),
                  pl.BlockSpec((1, LANES), lambda i: (0, 0))],
        out_specs=(pl.BlockSpec((tm, LANES), lambda i: (i, 0)),
                   pl.BlockSpec((tm, LANES), lambda i: (i, 0))),
        compiler_params=_cparams(("parallel",)),
        name="rope_tables",
    )(pos_b, invf)


def _rms(x, g, n):
    ms = jnp.sum(x * x, axis=-1, keepdims=True) * (1.0 / n)
    return x * lax.rsqrt(ms + EPS) * g


def _pair_rms(blk, g, scale):
    lane = lax.broadcasted_iota(jnp.int32, blk.shape, 1)
    lo = lane < HEAD_DIM
    sq = blk * blk
    s_all = jnp.sum(sq, axis=-1, keepdims=True)
    s_lo = jnp.sum(jnp.where(lo, sq, 0.0), axis=-1, keepdims=True)
    ms = jnp.where(lo, s_lo, s_all - s_lo) * (1.0 / HEAD_DIM)
    return blk * lax.rsqrt(ms + EPS) * (g * scale)


def _store_kv_blocks(ref, xt):
    slab = ref.shape[-1]
    for cblk in range(xt.shape[1] // slab):
        ref[cblk] = xt[:, cblk * slab:(cblk + 1) * slab].astype(ref.dtype)


def _inproj_kernel(x_ref, mod_ref, nw_ref, win_ref, wsbvt_ref, rc_ref, rs_ref, qn_ref, wq_ref,
                   kvn_ref, wk_ref, wvt_ref, gq_ref, gk_ref, gcq_ref, gck_ref,
                   sbq_ref, sbk_ref, sbvt_ref, mq_ref, mk_ref, mvt_ref,
                   cq_ref, ck_ref, cvt_ref):
    x = x_ref[...]
    xn = _rms(x, nw_ref[...], D_MODEL)
    h = (xn * (1.0 + mod_ref[1:2, :]) + mod_ref[0:1, :]).astype(BF16)
    proj = jnp.dot(h, win_ref[...], preferred_element_type=F32)

    sbq_ref[...] = (proj[:, C_SBQ:C_SBQ + D_SB] * (HEAD_DIM ** -0.5 * LOG2E)).astype(BF16)
    sbk_ref[...] = proj[:, C_SBK:C_SBK + D_SB].astype(BF16)
    vt = lax.dot_general(wsbvt_ref[...], h, NT_DIMS, preferred_element_type=F32)
    _store_kv_blocks(sbvt_ref, vt[:D_SB])
    _store_kv_blocks(cvt_ref, vt[D_SB:])

    rc = rc_ref[...]
    rs = rs_ref[...]
    cqn = _rms(proj[:, C_CQ:C_CQ + MLA_Q_RANK], qn_ref[...], MLA_Q_RANK).astype(BF16)
    qq = jnp.dot(cqn, wq_ref[...], preferred_element_type=F32)
    gq = gq_ref[...] * (MLA_QK ** -0.5 * LOG2E)
    nq = MLA_HEADS * LANES
    for hh in range(MLA_HEADS):
        a = qq[:, hh * LANES:(hh + 1) * LANES]
        b = qq[:, nq + hh * LANES:nq + (hh + 1) * LANES]
        qh = a * rc + b * rs
        mq_ref[:, hh * LANES:(hh + 1) * LANES] = _rms(qh, gq, MLA_QK).astype(BF16)

    ckvn = _rms(proj[:, C_CKV:C_CKV + MLA_KV_RANK], kvn_ref[...], MLA_KV_RANK).astype(BF16)
    kn = jnp.dot(ckvn, wk_ref[...], preferred_element_type=F32)
    _store_kv_blocks(mvt_ref, lax.dot_general(wvt_ref[...], ckvn, NT_DIMS,
                                              preferred_element_type=F32))
    kpe = proj[:, C_KPA:C_KPA + LANES] * rc + proj[:, C_KPB:C_KPB + LANES] * rs
    gk = gk_ref[...]
    for hh in range(MLA_HEADS):
        kh = kn[:, hh * LANES:(hh + 1) * LANES] + kpe
        mk_ref[:, hh * LANES:(hh + 1) * LANES] = _rms(kh, gk, MLA_QK).astype(BF16)

    gcq = gcq_ref[...]
    gck = gck_ref[...]
    for p in range(D_CK // LANES):
        qb = proj[:, C_CKQ + p * LANES:C_CKQ + (p + 1) * LANES]
        kb = proj[:, C_CKK + p * LANES:C_CKK + (p + 1) * LANES]
        cq_ref[:, p * LANES:(p + 1) * LANES] = _pair_rms(
            qb, gcq, HEAD_DIM ** -0.5 * LOG2E).astype(BF16)
        ck_ref[:, p * LANES:(p + 1) * LANES] = _pair_rms(kb, gck, 1.0).astype(BF16)


def _inproj(x2, mod_l, nw, win, wsbvt, rc, rs, qn, wq, kvn, wk, wvt, gq, gk, gcq, gck, seq):
    t = x2.shape[0]
    tm = 512
    tpb = seq // tm
    bsz = t // seq
    full = lambda a: pl.BlockSpec(a.shape, lambda i: (0,) * a.ndim)
    row = lambda w: pl.BlockSpec((tm, w), lambda i: (i, 0))
    rows = lambda w: (jax.ShapeDtypeStruct((t, w), BF16), row(w))
    kvt = lambda w, slab=TKV: (
        jax.ShapeDtypeStruct((bsz, seq // slab, w, slab), BF16),
        pl.BlockSpec((None, tm // slab, w, slab), lambda i: (i // tpb, i % tpb, 0, 0)))
    outs = [rows(D_SB), rows(D_SB), kvt(D_SB), rows(MLA_HEADS * LANES), rows(MLA_HEADS * LANES),
            kvt(D_MLA), rows(D_CK), rows(D_CK), kvt(D_CK, CK_QB)]
    return pl.pallas_call(
        _inproj_kernel,
        out_shape=tuple(o[0] for o in outs),
        grid=(t // tm,),
        in_specs=[row(D_MODEL),
                  pl.BlockSpec((None, 6, D_MODEL), lambda i: (i // tpb, 0, 0)),
                  full(nw), full(win), full(wsbvt), row(LANES), row(LANES), full(qn), full(wq),
                  full(kvn), full(wk), full(wvt), full(gq), full(gk), full(gcq), full(gck)],
        out_specs=tuple(o[1] for o in outs),
        compiler_params=_cparams(("parallel",)),
        name="inproj",
    )(x2, mod_l, nw, win, wsbvt, rc, rs, qn, wq, kvn, wk, wvt, gq, gk, gcq, gck)


def _kv_iotas(tk, tq):
    return (lax.broadcasted_iota(jnp.int32, (tk, tq), 0),
            lax.broadcasted_iota(jnp.int32, (tk, tq), 1))


SB_DEAD = 160.0


def _sb_kernel(q_ref, k_ref, vt_ref, o_ref, *, tq, heads):
    tk = TKV
    i = pl.program_id(2)
    lo = lax.broadcasted_iota(jnp.int32, (tq, LANES), 1) < HEAD_DIM
    qs = []
    for pb in range(heads // 2):
        q = q_ref[:, pb * LANES:(pb + 1) * LANES]
        zq = jnp.zeros_like(q)
        qs += [jnp.where(lo, q, zq), jnp.where(lo, zq, q)]
    krow, qcol = _kv_iotas(tk, tq)
    strict = krow < qcol
    ur, uc = _kv_iotas(tk, tk)
    later = jnp.where(uc > ur, 1.0, 0.0).astype(BF16)

    sign = jnp.uint32(0x80000000)

    def stage1(j, diag):
        r0 = pl.multiple_of(j * tk, tk)
        out = []
        for hh in range(heads):
            pb = hh // 2
            kb = k_ref[pl.ds(r0, tk), pb * LANES:(pb + 1) * LANES]
            y = lax.dot_general(kb, qs[hh], NT_DIMS, preferred_element_type=F32)
            neg_abs = pltpu.bitcast(pltpu.bitcast(y, jnp.uint32) | sign, F32)
            sp = jnp.maximum(y, 0.0) + jnp.log2(1.0 + jnp.exp2(neg_abs))
            spm = jnp.where(strict, sp, 0.0) if diag else sp
            tail = jnp.dot(later, spm.astype(BF16), preferred_element_type=F32)
            pre = y - sp - tail
            if diag:
                pre = jnp.where(strict, pre, NEG)
            out.append((pre, tail[0:1, :] + spm[0:1, :]))
        return tuple(out)

    def stage2(j, pres, carry):
        out = []
        for hh in range(heads):
            acc, car = carry[hh]
            pre, inc = pres[hh]
            w = jnp.exp2(pre - car)
            vb = vt_ref[j, hh * HEAD_DIM:(hh + 1) * HEAD_DIM, :]
            acc = acc + jnp.dot(vb, w.astype(BF16), preferred_element_type=F32)
            out.append((acc, car + inc))
        return tuple(out)

    init = tuple((jnp.zeros((HEAD_DIM, tq), F32), jnp.zeros((1, tq), F32))
                 for _ in range(heads))

    def live(carry, pres):
        mass = functools.reduce(jnp.minimum, [carry[hh][1] + pres[hh][1] for hh in range(heads)])
        return (jnp.min(mass) < SB_DEAD).astype(jnp.int32)

    def cond(state):
        jj, _, _, more = state
        return (jj < i) & (more > 0)

    def body(state):
        jj, carry, pres, _ = state
        j = i - jj
        ahead = stage1(j - 1, False)
        carry = stage2(j, pres, carry)
        return jj + 1, carry, ahead, live(carry, ahead)

    first = stage1(i, True)
    jj, carry, pres, _ = lax.while_loop(cond, body, (jnp.int32(0), init, first, live(init, first)))
    carry = stage2(i - jj, pres, carry)
    for hh in range(heads):
        o_ref[hh * HEAD_DIM:(hh + 1) * HEAD_DIM, :] = carry[hh][0].astype(o_ref.dtype)


def _sb_attention(q, k, vt):
    b, s, d = q.shape
    tq = TKV
    heads = 4
    w = heads * HEAD_DIM
    return pl.pallas_call(
        functools.partial(_sb_kernel, tq=tq, heads=heads),
        out_shape=jax.ShapeDtypeStruct((b, d, s), BF16),
        grid=(b, d // w, s // tq),
        in_specs=[pl.BlockSpec((None, tq, w), lambda bi, p, i: (bi, i, p)),
                  pl.BlockSpec((None, s, w), lambda bi, p, i: (bi, 0, p)),
                  pl.BlockSpec((None, s // TKV, w, TKV), lambda bi, p, i: (bi, 0, p, 0))],
        out_specs=pl.BlockSpec((None, w, tq), lambda bi, p, i: (bi, p, i)),
        compiler_params=_cparams(("parallel", "parallel", "arbitrary")),
        name="sb_attn",
    )(q, k, vt)


DEN_ROWS = 16


def _mla_kernel(q_ref, k_ref, vt_ref, o_ref, *, tq, heads):
    tk = TKV
    i = pl.program_id(2)
    nsub = tq // tk
    krow, qcol = _kv_iotas(tk, tq)
    shift = CHUNK.bit_length() - 1
    qs = tuple(q_ref[:, hh * LANES:(hh + 1) * LANES] for hh in range(heads))

    def scores(j, diag):
        r0 = pl.multiple_of(j * tk, tk)
        if diag is not None:
            allowed = (jnp.right_shift(krow + diag * tk, shift) <= jnp.right_shift(qcol, shift))
        out = []
        for hh in range(heads):
            s = lax.dot_general(k_ref[pl.ds(r0, tk), hh * LANES:(hh + 1) * LANES], qs[hh],
                                NT_DIMS, preferred_element_type=F32)
            if diag is not None:
                s = jnp.where(allowed, s, NEG)
            out.append((s, jnp.max(s, axis=0, keepdims=True)))
        return tuple(out)

    def update(j, ss, carry):
        out = []
        for hh in range(heads):
            m, acc = carry[hh]
            s, smax = ss[hh]
            m_new = jnp.maximum(m, smax)
            alpha = jnp.exp2(m - m_new)
            p = jnp.exp2(s - m_new).astype(BF16)
            vb = jnp.concatenate([vt_ref[j, hh * MLA_V:(hh + 1) * MLA_V, :], ones], axis=0)
            acc = alpha * acc + jnp.dot(vb, p, preferred_element_type=F32)
            out.append((m_new, acc))
        return tuple(out)

    ones = jnp.ones((DEN_ROWS, tk), BF16)
    init = tuple((jnp.full((1, tq), NEG, F32), jnp.zeros((MLA_V + DEN_ROWS, tq), F32))
                 for _ in range(heads))
    base = i * nsub
    carry = init
    ss = scores(base + nsub - 1, nsub - 1)
    for d in range(nsub - 1, -1, -1):
        s_ahead = scores(jnp.maximum(base + d - 1, 0), d - 1 if d > 0 else None)
        carry = update(base + d, ss, carry)
        ss = s_ahead

    def body(jj, state):
        carry, ss = state
        j = base - 1 - jj
        s_ahead = scores(jnp.maximum(j - 1, 0), None)
        return update(j, ss, carry), s_ahead

    carry, _ = lax.fori_loop(0, base, body, (carry, ss))
    for hh in range(heads):
        acc = carry[hh][1]
        o_ref[hh * MLA_V:(hh + 1) * MLA_V, :] = (
            acc[:MLA_V] / acc[MLA_V:MLA_V + 1]).astype(o_ref.dtype)


def _mla_attention(q, k, vt):
    b, s, _ = q.shape
    tq = 2 * TKV
    heads = 2
    return pl.pallas_call(
        functools.partial(_mla_kernel, tq=tq, heads=heads),
        out_shape=jax.ShapeDtypeStruct((b, D_MLA, s), BF16),
        grid=(b, MLA_HEADS // heads, s // tq),
        in_specs=[pl.BlockSpec((None, tq, heads * LANES), lambda bi, p, i: (bi, i, p)),
                  pl.BlockSpec((None, s, heads * LANES), lambda bi, p, i: (bi, 0, p)),
                  pl.BlockSpec((None, s // TKV, heads * MLA_V, TKV),
                               lambda bi, p, i: (bi, 0, p, 0))],
        out_specs=pl.BlockSpec((None, heads * MLA_V, tq), lambda bi, p, i: (bi, p, i)),
        compiler_params=_cparams(("parallel", "parallel", "arbitrary")),
        name="mla_attn",
    )(q, k, vt)


CK_QB = 2 * CHUNK
CK_KEYS = CK_WINDOW + CHUNK
CK_PAD = CK_LEFT_CHUNKS * CHUNK


def _ck_kernel(q_ref, k_ref, vt_ref, bias_ref, o_ref, *, n_blocks, heads):
    lo = lax.broadcasted_iota(jnp.int32, (CK_QB, LANES), 1) < HEAD_DIM
    krow = lax.broadcasted_iota(jnp.int32, (CK_KEYS, CK_QB), 0)
    n_slabs = CK_KEYS // CK_QB

    ones = jnp.ones((DEN_ROWS, CK_KEYS), BF16)

    def scores(c2, masked):
        r0 = pl.multiple_of(c2 * CK_QB, CK_QB)
        if masked:
            valid = krow >= CK_PAD - c2 * CK_QB
        out = []
        for pb in range(heads // 2):
            q = q_ref[pl.ds(r0, CK_QB), pb * LANES:(pb + 1) * LANES]
            kw = k_ref[pl.ds(r0, CK_KEYS), pb * LANES:(pb + 1) * LANES]
            zq = jnp.zeros_like(q)
            for hl in range(2):
                qh = jnp.where(lo, q, zq) if hl == 0 else jnp.where(lo, zq, q)
                s = (lax.dot_general(kw, qh, NT_DIMS, preferred_element_type=F32)
                     + bias_ref[2 * pb + hl])
                if masked:
                    s = jnp.where(valid, s, NEG)
                out.append((s, jnp.max(s, axis=0, keepdims=True)))
        return tuple(out)

    def finish(c2, ss):
        for hh in range(heads):
            s, m = ss[hh]
            p = jnp.exp2(s - m).astype(BF16)
            vw = jnp.concatenate(
                [jnp.concatenate([vt_ref[c2 + sb, hh * HEAD_DIM:(hh + 1) * HEAD_DIM, :]
                                  for sb in range(n_slabs)], axis=1), ones], axis=0)
            acc = jnp.dot(vw, p, preferred_element_type=F32)
            o_ref[c2, hh * HEAD_DIM:(hh + 1) * HEAD_DIM, :] = (
                acc[:HEAD_DIM] / acc[HEAD_DIM:HEAD_DIM + 1]).astype(o_ref.dtype)

    def sweep(lo_blk, hi_blk, masked):
        def body(c2, ss):
            ahead = scores(jnp.minimum(c2 + 1, hi_blk - 1), masked)
            finish(c2, ss)
            return ahead

        lax.fori_loop(lo_blk, hi_blk, body, scores(lo_blk, masked))

    n_masked = CK_PAD // CK_QB
    sweep(0, n_masked, True)
    sweep(n_masked, n_blocks, False)


def _ck_attention(q, kpad, vtpad, bias):
    b, s, d = q.shape
    sp = kpad.shape[1]
    nb = s // CK_QB
    heads = d // HEAD_DIM
    kern = functools.partial(_ck_kernel, n_blocks=nb, heads=heads)
    return pl.pallas_call(
        kern,
        out_shape=jax.ShapeDtypeStruct((b, nb, d, CK_QB), BF16),
        grid=(b,),
        in_specs=[pl.BlockSpec((None, s, d), lambda bi: (bi, 0, 0)),
                  pl.BlockSpec((None, sp, d), lambda bi: (bi, 0, 0)),
                  pl.BlockSpec((None, sp // CK_QB, d, CK_QB), lambda bi: (bi, 0, 0, 0)),
                  pl.BlockSpec((heads, CK_KEYS, CK_QB), lambda bi: (0, 0, 0))],
        out_specs=pl.BlockSpec((None, nb, d, CK_QB), lambda bi: (bi, 0, 0, 0)),
        compiler_params=_cparams(("parallel",)),
        name="ck_attn",
    )(q, kpad, vtpad, bias)


def _outproj_kernel(osb_ref, omla_ref, ock_ref, x_ref, mod_ref, gn_ref, wout_ref, nffn_ref,
                    *rest, moe):
    if moe:
        wr_ref, xo_ref, h2_ref, gates_ref = rest
    else:
        xo_ref, h2_ref = rest
    gn = gn_ref[...]

    def gnorm_t(ot, c0, w):
        ot = ot.astype(F32)
        ms = jnp.sum(ot * ot, axis=0, keepdims=True) * (1.0 / w)
        return ((ot * lax.rsqrt(ms + EPS)).T * gn[:, c0:c0 + w]).astype(BF16)

    ock = jnp.concatenate([ock_ref[sb] for sb in range(ock_ref.shape[0])], axis=1)
    merged = jnp.concatenate([gnorm_t(osb_ref[...], 0, D_SB), gnorm_t(omla_ref[...], D_SB, D_MLA),
                              gnorm_t(ock, D_SB + D_MLA, D_CK)], axis=-1)
    y = jnp.dot(merged, wout_ref[...], preferred_element_type=F32)
    xn = x_ref[...] + mod_ref[2:3, :] * y
    xo_ref[...] = xn
    h2 = _rms(xn, nffn_ref[...], D_MODEL) * (1.0 + mod_ref[4:5, :]) + mod_ref[3:4, :]
    h2_ref[...] = h2.astype(h2_ref.dtype)
    if moe:
        wr = wr_ref[...]
        w_hi = wr.astype(BF16)
        w_lo = (wr - w_hi.astype(F32)).astype(BF16)
        h_hi = h2.astype(BF16)
        h_lo = (h2 - h_hi.astype(F32)).astype(BF16)
        logits = (jnp.dot(h_hi, w_hi, preferred_element_type=F32)
                  + (jnp.dot(h_hi, w_lo, preferred_element_type=F32)
                     + jnp.dot(h_lo, w_hi, preferred_element_type=F32)))
        lane = lax.broadcasted_iota(jnp.int32, logits.shape, 1).astype(F32)
        logits = jnp.where(lane < N_EXPERTS, logits, -jnp.inf)
        m1 = jnp.max(logits, axis=-1, keepdims=True)
        i1 = jnp.min(jnp.where(logits == m1, lane, float(LANES)), axis=-1, keepdims=True)
        rest_l = jnp.where(lane == i1, -jnp.inf, logits)
        m2 = jnp.max(rest_l, axis=-1, keepdims=True)
        i2 = jnp.min(jnp.where(rest_l == m2, lane, float(LANES)), axis=-1, keepdims=True)
        e2 = jnp.exp(m2 - m1)
        w1 = 1.0 / (1.0 + e2)
        w2 = e2 / (1.0 + e2)
        gates_ref[...] = jnp.where(lane == 0.0, i1, jnp.where(lane == 1.0, i2,
                                   jnp.where(lane == 2.0, w1, jnp.where(lane == 3.0, w2, 0.0))))


def _outproj(osb, omla, ock, x2, mod_l, gn, wout, nffn, wr, seq):
    t = x2.shape[0]
    tm = 512
    tpb = seq // tm
    moe = wr is not None
    full = lambda a: pl.BlockSpec(a.shape, lambda i: (0,) * a.ndim)
    row = lambda w: pl.BlockSpec((tm, w), lambda i: (i, 0))
    colt = lambda w: pl.BlockSpec((None, w, tm), lambda i: (i // tpb, 0, i % tpb))
    ins = [osb, omla, ock, x2, mod_l, gn, wout, nffn]
    slabs = pl.BlockSpec((None, tm // CK_QB, D_CK, CK_QB), lambda i: (i // tpb, i % tpb, 0, 0))
    in_specs = [colt(D_SB), colt(D_MLA), slabs, row(D_MODEL),
                pl.BlockSpec((None, 6, D_MODEL), lambda i: (i // tpb, 0, 0)),
                full(gn), full(wout), full(nffn)]
    out_shape = [jax.ShapeDtypeStruct((t, D_MODEL), F32),
                 jax.ShapeDtypeStruct((t, D_MODEL), F32 if moe else BF16)]
    out_specs = [row(D_MODEL), row(D_MODEL)]
    if moe:
        ins.append(wr)
        in_specs.append(full(wr))
        out_shape.append(jax.ShapeDtypeStruct((t, LANES), F32))
        out_specs.append(row(LANES))
    return pl.pallas_call(
        functools.partial(_outproj_kernel, moe=moe),
        out_shape=tuple(out_shape),
        grid=(t // tm,),
        in_specs=in_specs,
        out_specs=tuple(out_specs),
        compiler_params=_cparams(("parallel",)),
        name="outproj_moe" if moe else "outproj",
    )(*ins)


FFN_TM = 512
MXU_DEPTH = 256
FFN_CHUNKS = ((0, 5 * MXU_DEPTH), (5 * MXU_DEPTH, 6 * MXU_DEPTH))
RESIDENT = pl.Buffered(1)


def _swiglu(h, wg_ref, wu_ref, wd_ref):
    y = None
    for c0, w in FFN_CHUNKS:
        g = jnp.dot(h, wg_ref[:, c0:c0 + w], preferred_element_type=F32)
        u = jnp.dot(h, wu_ref[:, c0:c0 + w], preferred_element_type=F32)
        a = (g * jax.nn.sigmoid(g) * u).astype(BF16)
        part = jnp.dot(a, wd_ref[c0:c0 + w, :], preferred_element_type=F32)
        y = part if y is None else y + part
    return y


def _ffn_kernel(h_ref, x_ref, mod_ref, wg_ref, wu_ref, wd_ref, o_ref):
    o_ref[...] = x_ref[...] + mod_ref[5:6, :] * _swiglu(h_ref[...], wg_ref, wu_ref, wd_ref)


def _ffn(h2, x2, mod_l, wg, wu, wd, seq):
    t = h2.shape[0]
    tm = FFN_TM
    tpb = seq // tm
    rmap = lambda i: (i, 0)
    const = lambda a: pl.BlockSpec(a.shape, lambda i: (0, 0), pipeline_mode=RESIDENT)
    return pl.pallas_call(
        _ffn_kernel,
        out_shape=jax.ShapeDtypeStruct((t, D_MODEL), F32),
        grid=(t // tm,),
        in_specs=[pl.BlockSpec((tm, D_MODEL), rmap), pl.BlockSpec((tm, D_MODEL), rmap),
                  pl.BlockSpec((None, 6, D_MODEL), lambda i: (i // tpb, 0, 0)),
                  const(wg), const(wu), const(wd)],
        out_specs=pl.BlockSpec((tm, D_MODEL), rmap),
        compiler_params=_cparams(("parallel",)),
        name="dense_ffn",
    )(h2, x2, mod_l, wg, wu, wd)


ROUTE_TM = 512


def _rank_kernel(route_ref, rank_ref, count_ref, base_ref):
    i = pl.program_id(0)

    @pl.when(i == 0)
    def _():
        base_ref[...] = jnp.zeros_like(base_ref)

    tm = route_ref.shape[0]
    route = route_ref[...]
    lane = lax.broadcasted_iota(jnp.int32, (tm, LANES), 1).astype(F32)
    sel1 = lane == route[:, 0:1]
    sel2 = lane == route[:, 1:2]
    hot = jnp.where(sel1 | sel2, 1.0, 0.0)
    r = lax.broadcasted_iota(jnp.int32, (tm, tm), 0)
    c = lax.broadcasted_iota(jnp.int32, (tm, tm), 1)
    before = jnp.where(c < r, 1.0, 0.0).astype(BF16)
    seen = jnp.dot(before, hot.astype(BF16), preferred_element_type=F32) + base_ref[...]
    r1 = jnp.sum(jnp.where(sel1, seen, 0.0), axis=-1, keepdims=True)
    r2 = jnp.sum(jnp.where(sel2, seen, 0.0), axis=-1, keepdims=True)
    rank_ref[...] = jnp.where(lane == 0.0, r1, jnp.where(lane == 1.0, r2, 0.0))
    base_ref[...] += jnp.sum(hot, axis=0, keepdims=True)
    count_ref[...] = base_ref[...]


def _moe_rank(route):
    t = route.shape[0]
    tm = ROUTE_TM
    return pl.pallas_call(
        _rank_kernel,
        out_shape=(jax.ShapeDtypeStruct((t, LANES), F32), jax.ShapeDtypeStruct((1, LANES), F32)),
        grid=(t // tm,),
        in_specs=[pl.BlockSpec((tm, LANES), lambda i: (i, 0))],
        out_specs=(pl.BlockSpec((tm, LANES), lambda i: (i, 0)),
                   pl.BlockSpec((1, LANES), lambda i: (0, 0))),
        scratch_shapes=[pltpu.VMEM((1, LANES), F32)],
        compiler_params=_cparams(("arbitrary",)),
        name="moe_rank",
    )(route)


def _dispatch_kernel(pos_ref, h_ref, xs_in_ref, xs_ref, sem):
    del xs_in_ref
    td = pos_ref.shape[1]

    def issue(r, carry):
        src = h_ref.at[pl.ds(r, 1), :]
        for k in range(2):
            pltpu.make_async_copy(src, xs_ref.at[pl.ds(pos_ref[k, r], 1), :],
                                  sem.at[k]).start(priority=k)
        return carry

    lax.fori_loop(0, td, issue, 0, unroll=8)
    for k in range(2):
        pltpu.make_async_copy(h_ref, xs_ref.at[pl.ds(0, td), :], sem.at[k]).wait()


def _moe_dispatch(pos, h2, n_rows):
    nt, _, td = pos.shape
    xs0 = jnp.zeros((n_rows, D_MODEL), F32)
    return pl.pallas_call(
        _dispatch_kernel,
        out_shape=jax.ShapeDtypeStruct((n_rows, D_MODEL), F32),
        grid=(nt,),
        in_specs=[pl.BlockSpec((None, 2, td), lambda i: (i, 0, 0), memory_space=pltpu.SMEM),
                  pl.BlockSpec((td, D_MODEL), lambda i: (i, 0)),
                  pl.BlockSpec(memory_space=pl.ANY)],
        out_specs=pl.BlockSpec(memory_space=pl.ANY),
        scratch_shapes=[pltpu.SemaphoreType.DMA((2,))],
        input_output_aliases={2: 0},
        compiler_params=_cparams(("arbitrary",)),
        name="moe_dispatch",
    )(pos, h2, xs0)


def _group_ffn_kernel(te_ref, nu_ref, xs_ref, wg_ref, wu_ref, wd_ref, y_ref):
    del te_ref
    i = pl.program_id(0)

    @pl.when(i < nu_ref[0])
    def _():
        y_ref[...] = _swiglu(xs_ref[...].astype(BF16), wg_ref, wu_ref, wd_ref)

    @pl.when(i >= nu_ref[0])
    def _():
        y_ref[...] = jnp.zeros_like(y_ref)


def _moe_group_ffn(tile_expert, n_used, xs, wg, wu, wd):
    n_rows = xs.shape[0]
    tm = FFN_TM
    row = lambda i, te, nu: (jnp.minimum(i, nu[0] - 1), 0)
    wspec = lambda a: pl.BlockSpec((None,) + a.shape[1:], lambda i, te, nu: (te[i], 0, 0))
    return pl.pallas_call(
        _group_ffn_kernel,
        out_shape=jax.ShapeDtypeStruct((n_rows, D_MODEL), F32),
        grid_spec=pltpu.PrefetchScalarGridSpec(
            num_scalar_prefetch=2,
            grid=(n_rows // tm,),
            in_specs=[pl.BlockSpec((tm, D_MODEL), row), wspec(wg), wspec(wu), wspec(wd)],
            out_specs=pl.BlockSpec((tm, D_MODEL), lambda i, te, nu: (i, 0))),
        compiler_params=_cparams(("arbitrary",)),
        name="moe_group_ffn",
    )(tile_expert, n_used, xs, wg, wu, wd)


def _combine_kernel(pos_ref, route_ref, x_ref, mod_ref, y_ref, o_ref, buf_ref, sem):
    tc = pos_ref.shape[1]

    def issue(r, carry):
        for k in range(2):
            pltpu.make_async_copy(y_ref.at[pl.ds(pos_ref[k, r], 1), :],
                                  buf_ref.at[k, pl.ds(r, 1), :], sem.at[k]).start(priority=k)
        return carry

    lax.fori_loop(0, tc, issue, 0, unroll=8)
    for k in range(2):
        pltpu.make_async_copy(y_ref.at[pl.ds(0, tc), :], buf_ref.at[k], sem.at[k]).wait()
    route = route_ref[...]
    y = route[:, 2:3] * buf_ref[0] + route[:, 3:4] * buf_ref[1]
    o_ref[...] = x_ref[...] + mod_ref[5:6, :] * y


def _moe_combine(pos, route, x2, mod_l, y, seq):
    nt, _, tc = pos.shape
    t = x2.shape[0]
    tpb = seq // tc
    return pl.pallas_call(
        _combine_kernel,
        out_shape=jax.ShapeDtypeStruct((t, D_MODEL), F32),
        grid=(nt,),
        in_specs=[pl.BlockSpec((None, 2, tc), lambda i: (i, 0, 0), memory_space=pltpu.SMEM),
                  pl.BlockSpec((tc, LANES), lambda i: (i, 0)),
                  pl.BlockSpec((tc, D_MODEL), lambda i: (i, 0)),
                  pl.BlockSpec((None, 6, D_MODEL), lambda i: (i // tpb, 0, 0)),
                  pl.BlockSpec(memory_space=pl.ANY)],
        out_specs=pl.BlockSpec((tc, D_MODEL), lambda i: (i, 0)),
        scratch_shapes=[pltpu.VMEM((2, tc, D_MODEL), F32), pltpu.SemaphoreType.DMA((2,))],
        compiler_params=_cparams(("arbitrary",)),
        name="moe_combine",
    )(pos, route, x2, mod_l, y)


def _moe(h2, x2, mod_l, route, wg, wu, wd, seq):
    t = h2.shape[0]
    ne = wg.shape[0]
    tm = FFN_TM
    n_tiles = (2 * t) // tm + ne
    n_rows = n_tiles * tm
    rank, count = _moe_rank(route)
    counts = count[0, :ne].astype(jnp.int32)
    tiles_per = (counts + tm - 1) // tm
    tile_end = jnp.cumsum(tiles_per)
    start = (tile_end - tiles_per) * tm
    n_used = tile_end[-1:]
    tile_ids = jnp.minimum(jnp.arange(n_tiles, dtype=jnp.int32), n_used[0] - 1)
    tile_expert = jnp.sum((tile_ids[:, None] >= tile_end[None, :]).astype(jnp.int32), axis=1)
    experts = route[:, 0:2].astype(jnp.int32)
    pos = jnp.take(start, experts) + rank[:, 0:2].astype(jnp.int32)
    pos = pos.reshape(t // ROUTE_TM, ROUTE_TM, 2).transpose(0, 2, 1)
    xs = _moe_dispatch(pos, h2, n_rows)
    y = _moe_group_ffn(tile_expert, n_used, xs, wg, wu, wd)
    return _moe_combine(pos, route, x2, mod_l, y, seq)


def _rot_half_cols(w):
    half = w.shape[-1] // 2
    return jnp.concatenate([-w[..., half:], w[..., :half]], axis=-1)


def _pad_cols(w, left, total):
    return jnp.pad(w, ((0, 0), (left, total - left - w.shape[-1])))


def _layout_w_in(w_in):
    sizes = (D_SB, D_SB, D_SB, MLA_Q_RANK, MLA_KV_RANK, MLA_ROPE, D_CK, D_CK, D_CK)
    splits = [int(v) for v in np.cumsum(sizes)[:-1]]
    sbq, sbk, sbv, cq, ckv, kpe, ckq, ckk, ckv2 = jnp.split(w_in, splits, axis=-1)
    kpa = _pad_cols(kpe, MLA_NOPE, LANES)
    kpb = _pad_cols(_rot_half_cols(kpe), MLA_NOPE, LANES)
    main = jnp.concatenate([sbq, sbk, cq, ckv, kpa, kpb, ckq, ckk], axis=-1)
    return main.astype(BF16), jnp.concatenate([sbv, ckv2], axis=-1).T.astype(BF16)


def _layout_w_q_up(w):
    r = w.shape[0]
    w3 = w.reshape(r, MLA_HEADS, MLA_QK)
    nope, pe = w3[..., :MLA_NOPE], w3[..., MLA_NOPE:]
    zpad = jnp.zeros((r, MLA_HEADS, LANES - MLA_QK), w.dtype)
    a = jnp.concatenate([nope, pe, zpad], axis=-1).reshape(r, MLA_HEADS * LANES)
    b = jnp.concatenate([jnp.zeros_like(nope), _rot_half_cols(pe), zpad], axis=-1)
    return jnp.concatenate([a, b.reshape(r, MLA_HEADS * LANES)], axis=-1).astype(BF16)


def _layout_w_kv_up(w):
    r = w.shape[0]
    w3 = w.reshape(r, MLA_HEADS, MLA_NOPE + MLA_V)
    kn = jnp.pad(w3[..., :MLA_NOPE], ((0, 0), (0, 0), (0, LANES - MLA_NOPE)))
    return (kn.reshape(r, MLA_HEADS * LANES).astype(BF16),
            w3[..., MLA_NOPE:].reshape(r, D_MLA).T.astype(BF16))


def _ck_bias_table(rel_bias):
    heads = rel_bias.shape[0]
    n_g = CHUNK + CK_KEYS - 1
    blocks = []
    for u in range(CK_QB // CHUNK):
        d = np.arange(n_g)
        idx = np.clip(d - (CK_KEYS - 1) + CK_PAD + CHUNK * u, -REL_CLIP, REL_CLIP) + REL_CLIP
        g = rel_bias.astype(F32)[:, idx] * LOG2E
        hank = jnp.tile(g, (1, CHUNK + 1))[:, :CHUNK * (n_g + 1)]
        hank = hank.reshape(heads, CHUNK, n_g + 1)[:, :, :CK_KEYS]
        blk = jnp.transpose(hank[:, :, ::-1], (0, 2, 1))
        j = np.arange(CK_KEYS) - CHUNK * u
        in_band = ((j >= 0) & (j < CK_WINDOW))[None, :, None]
        blocks.append(jnp.where(in_band, blk, NEG))
    return jnp.concatenate(blocks, axis=-1)


def kernel(x, c, positions, ada_w, ada_b, norm_mix, norm_ffn, w_in, mla_q_norm, w_q_up, mla_kv_norm, w_kv_up, mla_q_qknorm, mla_k_qknorm, ck_q_qknorm, ck_k_qknorm, ck_rel_bias, group_out_norm, w_out, ffn_w_gate, ffn_w_up, ffn_w_down, moe_router, moe_w_gate, moe_w_up, moe_w_down):
    bsz, seq, d = x.shape
    depth = ada_w.shape[0]
    t = bsz * seq
    x2 = x.reshape(t, d)

    mod = _ada(c, ada_w, ada_b).reshape(depth, bsz, 6, d)
    rc, rs = _rope_tables(positions)
    pad128 = lambda g: jnp.pad(g, (0, LANES - g.shape[0])).reshape(1, LANES)
    pair = lambda g: jnp.concatenate([g, g]).reshape(1, LANES)

    for layer in range(depth):
        win, wsbvt = _layout_w_in(w_in[layer])
        wq = _layout_w_q_up(w_q_up[layer])
        wk, wvt = _layout_w_kv_up(w_kv_up[layer])
        sbq, sbk, sbvt, mq, mk, mvt, cq, ck, cvt = _inproj(
            x2, mod[layer], norm_mix[layer].reshape(1, d), win, wsbvt, rc, rs,
            mla_q_norm[layer].reshape(1, -1), wq, mla_kv_norm[layer].reshape(1, -1), wk, wvt,
            pad128(mla_q_qknorm[layer]), pad128(mla_k_qknorm[layer]),
            pair(ck_q_qknorm[layer]), pair(ck_k_qknorm[layer]), seq)

        r3 = lambda a: a.reshape(bsz, seq, a.shape[-1])
        o_sb = _sb_attention(r3(sbq), r3(sbk), sbvt)
        o_mla = _mla_attention(r3(mq), r3(mk), mvt)
        kpad = jnp.pad(r3(ck), ((0, 0), (CK_PAD, 0), (0, 0)))
        vtpad = jnp.pad(cvt, ((0, 0), (CK_PAD // CK_QB, 0), (0, 0), (0, 0)))
        o_ck = _ck_attention(r3(cq), kpad, vtpad, _ck_bias_table(ck_rel_bias[layer]))

        i = layer // 2
        is_moe = layer % 2 == 1
        wr = None
        if is_moe:
            wr = jnp.pad(moe_router[i], ((0, 0), (0, LANES - N_EXPERTS)))
        res = _outproj(o_sb, o_mla, o_ck, x2,
                       mod[layer], group_out_norm[layer].reshape(1, -1),
                       w_out[layer].astype(BF16), norm_ffn[layer].reshape(1, d), wr, seq)
        if is_moe:
            x2, h2, route = res
            x2 = _moe(h2, x2, mod[layer], route, moe_w_gate[i].astype(BF16),
                      moe_w_up[i].astype(BF16), moe_w_down[i].astype(BF16), seq)
        else:
            x2, h2 = res
            x2 = _ffn(h2, x2, mod[layer], ffn_w_gate[i].astype(BF16),
                      ffn_w_up[i].astype(BF16), ffn_w_down[i].astype(BF16), seq)
    return x2.reshape(bsz, seq, d)
```

```python
import functools

import numpy as np
import jax
import jax.numpy as jnp
from jax import lax
from jax.experimental import pallas as pl
from jax.experimental.pallas import tpu as pltpu

F32 = jnp.float32
BF16 = jnp.bfloat16

D_MODEL = 1024
CHUNK = 64
HEAD_DIM = 64
SB_HEADS = 4
MLA_HEADS = 8
MLA_Q_RANK = 256
MLA_KV_RANK = 128
MLA_NOPE = 64
MLA_ROPE = 32
MLA_V = 64
MLA_QK = MLA_NOPE + MLA_ROPE
ROPE_THETA = 10000.0
CK_HEADS = 4
CK_LEFT_CHUNKS = 8
CK_WINDOW = (CK_LEFT_CHUNKS + 1) * CHUNK
REL_CLIP = 128
D_SB = SB_HEADS * HEAD_DIM
D_MLA = MLA_HEADS * MLA_V
D_CK = CK_HEADS * HEAD_DIM
D_FF = 2816
N_EXPERTS = 8
EPS = 1e-6
NEG = -1e30

LANES = 128
VMEM_LIMIT = 56 * 1024 * 1024

C_SBQ, C_SBK = 0, 256
C_CQ = 512
C_CKV = 768
C_KPA = 896
C_KPB = 1024
C_CKQ, C_CKK = 1152, 1408
IN_EXT = 1664

NT_DIMS = (((1,), (1,)), ((), ()))
TN_DIMS = (((0,), (0,)), ((), ()))
TKV = 256
LOG2E = 1.4426950408889634


def _cparams(sem, vmem=VMEM_LIMIT):
    return pltpu.CompilerParams(dimension_semantics=sem, vmem_limit_bytes=vmem)


def _ada_kernel(c_ref, w_ref, b_ref, o_ref):
    c = c_ref[...]
    ca = c * jax.nn.sigmoid(c)
    o_ref[...] = jnp.dot(ca, w_ref[...], preferred_element_type=F32,
                         precision=lax.Precision.HIGHEST) + b_ref[...]


def _ada(c, ada_w, ada_b):
    depth, d, n = ada_w.shape
    b = c.shape[0]
    tn = 1536
    return pl.pallas_call(
        _ada_kernel,
        out_shape=jax.ShapeDtypeStruct((depth, b, n), F32),
        grid=(depth, n // tn),
        in_specs=[pl.BlockSpec((b, d), lambda l, j: (0, 0)),
                  pl.BlockSpec((None, d, tn), lambda l, j: (l, 0, j)),
                  pl.BlockSpec((None, 1, tn), lambda l, j: (l, 0, j))],
        out_specs=pl.BlockSpec((None, b, tn), lambda l, j: (l, 0, j)),
        compiler_params=_cparams(("parallel", "parallel")),
        name="ada_mod",
    )(c, ada_w, ada_b.reshape(depth, 1, n))


def _rope_kernel(pos_ref, invf_ref, c_ref, s_ref):
    ang = pos_ref[...] * invf_ref[...]
    lane = lax.broadcasted_iota(jnp.int32, ang.shape, 1)
    rope = (lane >= MLA_NOPE) & (lane < MLA_QK)
    c_ref[...] = jnp.where(rope, jnp.cos(ang), jnp.where(lane < MLA_NOPE, 1.0, 0.0))
    s_ref[...] = jnp.where(rope, jnp.sin(ang), 0.0)


def _rope_tables(positions):
    t = positions.size
    tm = 1024
    inv_freq = ROPE_THETA ** (-jnp.arange(0, MLA_ROPE, 2, dtype=F32) / MLA_ROPE)
    invf = jnp.zeros((1, LANES), F32)
    invf = invf.at[0, MLA_NOPE:MLA_NOPE + 16].set(inv_freq)
    invf = invf.at[0, MLA_NOPE + 16:MLA_QK].set(inv_freq)
    pos_b = jnp.broadcast_to(positions.reshape(t, 1).astype(F32), (t, LANES))
    return pl.pallas_call(
        _rope_kernel,
        out_shape=(jax.ShapeDtypeStruct((t, LANES), F32),
                   jax.ShapeDtypeStruct((t, LANES), F32)),
        grid=(t // tm,),
        in_specs=[pl.BlockSpec((tm, LANES), lambda i: (i, 0)),
                  pl.BlockSpec((1, LANES), lambda i: (0, 0))],
        out_specs=(pl.BlockSpec((tm, LANES), lambda i: (i, 0)),
                   pl.BlockSpec((tm, LANES), lambda i: (i, 0))),
        compiler_params=_cparams(("parallel",)),
        name="rope_tables",
    )(pos_b, invf)


def _rms(x, g, n):
    ms = jnp.sum(x * x, axis=-1, keepdims=True) * (1.0 / n)
    return x * lax.rsqrt(ms + EPS) * g


def _pair_rms(blk, g, scale):
    lane = lax.broadcasted_iota(jnp.int32, blk.shape, 1)
    lo = lane < HEAD_DIM
    sq = blk * blk
    s_all = jnp.sum(sq, axis=-1, keepdims=True)
    s_lo = jnp.sum(jnp.where(lo, sq, 0.0), axis=-1, keepdims=True)
    ms = jnp.where(lo, s_lo, s_all - s_lo) * (1.0 / HEAD_DIM)
    return blk * lax.rsqrt(ms + EPS) * (g * scale)


def _store_kv_blocks(ref, xt):
    slab = ref.shape[-1]
    for cblk in range(xt.shape[1] // slab):
        ref[cblk] = xt[:, cblk * slab:(cblk + 1) * slab].astype(ref.dtype)


def _inproj_kernel(x_ref, mod_ref, nw_ref, win_ref, wsbvt_ref, rc_ref, rs_ref, qn_ref, wq_ref,
                   kvn_ref, wk_ref, wvt_ref, gq_ref, gk_ref, gcq_ref, gck_ref,
                   sbq_ref, sbk_ref, sbvt_ref, mq_ref, mk_ref, mvt_ref,
                   cq_ref, ck_ref, cvt_ref):
    x = x_ref[...]
    xn = _rms(x, nw_ref[...], D_MODEL)
    h = (xn * (1.0 + mod_ref[1:2, :]) + mod_ref[0:1, :]).astype(BF16)
    proj = jnp.dot(h, win_ref[...], preferred_element_type=F32)

    sbq_ref[...] = (proj[:, C_SBQ:C_SBQ + D_SB] * (HEAD_DIM ** -0.5 * LOG2E)).astype(BF16)
    sbk_ref[...] = proj[:, C_SBK:C_SBK + D_SB].astype(BF16)
    vt = lax.dot_general(wsbvt_ref[...], h, NT_DIMS, preferred_element_type=F32)
    _store_kv_blocks(sbvt_ref, vt[:D_SB])
    _store_kv_blocks(cvt_ref, vt[D_SB:])

    rc = rc_ref[...]
    rs = rs_ref[...]
    cqn = _rms(proj[:, C_CQ:C_CQ + MLA_Q_RANK], qn_ref[...], MLA_Q_RANK).astype(BF16)
    qq = jnp.dot(cqn, wq_ref[...], preferred_element_type=F32)
    gq = gq_ref[...] * (MLA_QK ** -0.5 * LOG2E)
    nq = MLA_HEADS * LANES
    for hh in range(MLA_HEADS):
        a = qq[:, hh * LANES:(hh + 1) * LANES]
        b = qq[:, nq + hh * LANES:nq + (hh + 1) * LANES]
        qh = a * rc + b * rs
        mq_ref[:, hh * LANES:(hh + 1) * LANES] = _rms(qh, gq, MLA_QK).astype(BF16)

    ckvn = _rms(proj[:, C_CKV:C_CKV + MLA_KV_RANK], kvn_ref[...], MLA_KV_RANK).astype(BF16)
    kn = jnp.dot(ckvn, wk_ref[...], preferred_element_type=F32)
    _store_kv_blocks(mvt_ref, lax.dot_general(wvt_ref[...], ckvn, NT_DIMS,
                                              preferred_element_type=F32))
    kpe = proj[:, C_KPA:C_KPA + LANES] * rc + proj[:, C_KPB:C_KPB + LANES] * rs
    gk = gk_ref[...]
    for hh in range(MLA_HEADS):
        kh = kn[:, hh * LANES:(hh + 1) * LANES] + kpe
        mk_ref[:, hh * LANES:(hh + 1) * LANES] = _rms(kh, gk, MLA_QK).astype(BF16)

    gcq = gcq_ref[...]
    gck = gck_ref[...]
    for p in range(D_CK // LANES):
        qb = proj[:, C_CKQ + p * LANES:C_CKQ + (p + 1) * LANES]
        kb = proj[:, C_CKK + p * LANES:C_CKK + (p + 1) * LANES]
        cq_ref[:, p * LANES:(p + 1) * LANES] = _pair_rms(
            qb, gcq, HEAD_DIM ** -0.5 * LOG2E).astype(BF16)
        ck_ref[:, p * LANES:(p + 1) * LANES] = _pair_rms(kb, gck, 1.0).astype(BF16)


def _inproj(x2, mod_l, nw, win, wsbvt, rc, rs, qn, wq, kvn, wk, wvt, gq, gk, gcq, gck, seq):
    t = x2.shape[0]
    tm = 512
    tpb = seq // tm
    bsz = t // seq
    full = lambda a: pl.BlockSpec(a.shape, lambda i: (0,) * a.ndim)
    row = lambda w: pl.BlockSpec((tm, w), lambda i: (i, 0))
    rows = lambda w: (jax.ShapeDtypeStruct((t, w), BF16), row(w))
    kvt = lambda w, slab=TKV: (
        jax.ShapeDtypeStruct((bsz, seq // slab, w, slab), BF16),
        pl.BlockSpec((None, tm // slab, w, slab), lambda i: (i // tpb, i % tpb, 0, 0)))
    outs = [rows(D_SB), rows(D_SB), kvt(D_SB), rows(MLA_HEADS * LANES), rows(MLA_HEADS * LANES),
            kvt(D_MLA), rows(D_CK), rows(D_CK), kvt(D_CK, CK_QB)]
    return pl.pallas_call(
        _inproj_kernel,
        out_shape=tuple(o[0] for o in outs),
        grid=(t // tm,),
        in_specs=[row(D_MODEL),
                  pl.BlockSpec((None, 6, D_MODEL), lambda i: (i // tpb, 0, 0)),
                  full(nw), full(win), full(wsbvt), row(LANES), row(LANES), full(qn), full(wq),
                  full(kvn), full(wk), full(wvt), full(gq), full(gk), full(gcq), full(gck)],
        out_specs=tuple(o[1] for o in outs),
        compiler_params=_cparams(("parallel",)),
        name="inproj",
    )(x2, mod_l, nw, win, wsbvt, rc, rs, qn, wq, kvn, wk, wvt, gq, gk, gcq, gck)


def _kv_iotas(tk, tq):
    return (lax.broadcasted_iota(jnp.int32, (tk, tq), 0),
            lax.broadcasted_iota(jnp.int32, (tk, tq), 1))


SB_DEAD = 160.0


def _sb_kernel(q_ref, k_ref, vt_ref, o_ref, *, tq, heads):
    tk = TKV
    i = pl.program_id(2)
    lo = lax.broadcasted_iota(jnp.int32, (tq, LANES), 1) < HEAD_DIM
    qs = []
    for pb in range(heads // 2):
        q = q_ref[:, pb * LANES:(pb + 1) * LANES]
        zq = jnp.zeros_like(q)
        qs += [jnp.where(lo, q, zq), jnp.where(lo, zq, q)]
    krow, qcol = _kv_iotas(tk, tq)
    strict = krow < qcol
    ur, uc = _kv_iotas(tk, tk)
    later = jnp.where(uc > ur, 1.0, 0.0).astype(BF16)

    sign = jnp.uint32(0x80000000)

    def stage1(j, diag):
        r0 = pl.multiple_of(j * tk, tk)
        out = []
        for hh in range(heads):
            pb = hh // 2
            kb = k_ref[pl.ds(r0, tk), pb * LANES:(pb + 1) * LANES]
            y = lax.dot_general(kb, qs[hh], NT_DIMS, preferred_element_type=F32)
            neg_abs = pltpu.bitcast(pltpu.bitcast(y, jnp.uint32) | sign, F32)
            sp = jnp.maximum(y, 0.0) + jnp.log2(1.0 + jnp.exp2(neg_abs))
            spm = jnp.where(strict, sp, 0.0) if diag else sp
            tail = jnp.dot(later, spm.astype(BF16), preferred_element_type=F32)
            pre = y - sp - tail
            if diag:
                pre = jnp.where(strict, pre, NEG)
            out.append((pre, tail[0:1, :] + spm[0:1, :]))
        return tuple(out)

    def stage2(j, pres, carry):
        out = []
        for hh in range(heads):
            acc, car = carry[hh]
            pre, inc = pres[hh]
            w = jnp.exp2(pre - car)
            vb = vt_ref[j, hh * HEAD_DIM:(hh + 1) * HEAD_DIM, :]
            acc = acc + jnp.dot(vb, w.astype(BF16), preferred_element_type=F32)
            out.append((acc, car + inc))
        return tuple(out)

    init = tuple((jnp.zeros((HEAD_DIM, tq), F32), jnp.zeros((1, tq), F32))
                 for _ in range(heads))

    def live(carry, pres):
        mass = functools.reduce(jnp.minimum, [carry[hh][1] + pres[hh][1] for hh in range(heads)])
        return (jnp.min(mass) < SB_DEAD).astype(jnp.int32)

    def cond(state):
        jj, _, _, more = state
        return (jj < i) & (more > 0)

    def body(state):
        jj, carry, pres, _ = state
        j = i - jj
        ahead = stage1(j - 1, False)
        carry = stage2(j, pres, carry)
        return jj + 1, carry, ahead, live(carry, ahead)

    first = stage1(i, True)
    jj, carry, pres, _ = lax.while_loop(cond, body, (jnp.int32(0), init, first, live(init, first)))
    carry = stage2(i - jj, pres, carry)
    for hh in range(heads):
        o_ref[hh * HEAD_DIM:(hh + 1) * HEAD_DIM, :] = carry[hh][0].astype(o_ref.dtype)


def _sb_attention(q, k, vt):
    b, s, d = q.shape
    tq = TKV
    heads = 4
    w = heads * HEAD_DIM
    return pl.pallas_call(
        functools.partial(_sb_kernel, tq=tq, heads=heads),
        out_shape=jax.ShapeDtypeStruct((b, d, s), BF16),
        grid=(b, d // w, s // tq),
        in_specs=[pl.BlockSpec((None, tq, w), lambda bi, p, i: (bi, i, p)),
                  pl.BlockSpec((None, s, w), lambda bi, p, i: (bi, 0, p)),
                  pl.BlockSpec((None, s // TKV, w, TKV), lambda bi, p, i: (bi, 0, p, 0))],
        out_specs=pl.BlockSpec((None, w, tq), lambda bi, p, i: (bi, p, i)),
        compiler_params=_cparams(("parallel", "parallel", "arbitrary")),
        name="sb_attn",
    )(q, k, vt)


DEN_ROWS = 16


MLA_SAFE_BOUND = 60.0


def _mla_kernel(bound_ref, q_ref, k_ref, vt_ref, o_ref, *, tq, heads):
    tk = TKV
    i = pl.program_id(2)
    nsub = tq // tk
    krow, qcol = _kv_iotas(tk, tq)
    shift = CHUNK.bit_length() - 1
    ones = jnp.ones((DEN_ROWS, tk), BF16)
    base = i * nsub

    def sweep(bounded):
        qs = tuple(q_ref[:, hh * LANES:(hh + 1) * LANES] for hh in range(heads))
        ref = bound_ref[0]

        def scores(j, diag):
            r0 = pl.multiple_of(j * tk, tk)
            if diag is not None:
                allowed = (jnp.right_shift(krow + diag * tk, shift)
                           <= jnp.right_shift(qcol, shift))
            out = []
            for hh in range(heads):
                s = lax.dot_general(k_ref[pl.ds(r0, tk), hh * LANES:(hh + 1) * LANES], qs[hh],
                                    NT_DIMS, preferred_element_type=F32)
                if diag is not None:
                    s = jnp.where(allowed, s, NEG)
                if bounded:
                    out.append(jnp.exp2(s - ref).astype(BF16))
                else:
                    out.append((s, jnp.max(s, axis=0, keepdims=True)))
            return tuple(out)

        def update(j, ss, carry):
            out = []
            for hh in range(heads):
                vb = jnp.concatenate([vt_ref[j, hh * MLA_V:(hh + 1) * MLA_V, :], ones], axis=0)
                if bounded:
                    out.append(carry[hh] + jnp.dot(vb, ss[hh], preferred_element_type=F32))
                else:
                    m, acc = carry[hh]
                    s, smax = ss[hh]
                    m_new = jnp.maximum(m, smax)
                    alpha = jnp.exp2(m - m_new)
                    p = jnp.exp2(s - m_new).astype(BF16)
                    acc = alpha * acc + jnp.dot(vb, p, preferred_element_type=F32)
                    out.append((m_new, acc))
            return tuple(out)

        acc0 = jnp.zeros((MLA_V + DEN_ROWS, tq), F32)
        carry = tuple(acc0 if bounded else (jnp.full((1, tq), NEG, F32), acc0)
                      for _ in range(heads))
        group = nsub if bounded else 1

        def scores_n(j):
            return tuple(scores(jnp.maximum(j - g, 0), None) for g in range(group))

        ss = scores(base + nsub - 1, nsub - 1)
        for d in range(nsub - 1, -1, -1):
            s_ahead = scores(base + d - 1, d - 1) if d > 0 else scores_n(base - 1)
            carry = update(base + d, ss, carry)
            ss = s_ahead

        def body(jj, state):
            carry, ss = state
            j = base - 1 - group * jj
            s_ahead = scores_n(j - group)
            for g in range(group):
                carry = update(j - g, ss[g], carry)
            return carry, s_ahead

        carry, _ = lax.fori_loop(0, base // group, body, (carry, ss))
        for hh in range(heads):
            acc = carry[hh] if bounded else carry[hh][1]
            o_ref[hh * MLA_V:(hh + 1) * MLA_V, :] = (
                acc[:MLA_V] / acc[MLA_V:MLA_V + 1]).astype(o_ref.dtype)

    safe = bound_ref[0] <= MLA_SAFE_BOUND
    pl.when(safe)(functools.partial(sweep, True))
    pl.when(jnp.logical_not(safe))(functools.partial(sweep, False))


def _mla_attention(bound, q, k, vt):
    b, s, _ = q.shape
    tq = 2 * TKV
    heads = 2
    return pl.pallas_call(
        functools.partial(_mla_kernel, tq=tq, heads=heads),
        out_shape=jax.ShapeDtypeStruct((b, D_MLA, s), BF16),
        grid=(b, MLA_HEADS // heads, s // tq),
        in_specs=[pl.BlockSpec(memory_space=pltpu.SMEM),
                  pl.BlockSpec((None, tq, heads * LANES), lambda bi, p, i: (bi, i, p)),
                  pl.BlockSpec((None, s, heads * LANES), lambda bi, p, i: (bi, 0, p)),
                  pl.BlockSpec((None, s // TKV, heads * MLA_V, TKV),
                               lambda bi, p, i: (bi, 0, p, 0))],
        out_specs=pl.BlockSpec((None, heads * MLA_V, tq), lambda bi, p, i: (bi, p, i)),
        compiler_params=_cparams(("parallel", "parallel", "arbitrary")),
        name="mla_attn",
    )(bound, q, k, vt)


def _mla_score_bound(g_q, g_k):
    b = 1.02 * MLA_QK * (MLA_QK ** -0.5 * LOG2E) * jnp.max(jnp.abs(g_q)) * jnp.max(jnp.abs(g_k))
    return b.astype(F32).reshape(1)


CK_QB = 2 * CHUNK
CK_KEYS = CK_WINDOW + CHUNK
CK_PAD = CK_LEFT_CHUNKS * CHUNK


def _ck_kernel(q_ref, k_ref, vt_ref, bias_ref, o_ref, *, n_blocks, heads):
    lo = lax.broadcasted_iota(jnp.int32, (CK_QB, LANES), 1) < HEAD_DIM
    krow = lax.broadcasted_iota(jnp.int32, (CK_KEYS, CK_QB), 0)
    n_slabs = CK_KEYS // CK_QB

    ones = jnp.ones((DEN_ROWS, CK_KEYS), BF16)

    def scores(c2, masked):
        r0 = pl.multiple_of(c2 * CK_QB, CK_QB)
        if masked:
            valid = krow >= CK_PAD - c2 * CK_QB
        out = []
        for pb in range(heads // 2):
            q = q_ref[pl.ds(r0, CK_QB), pb * LANES:(pb + 1) * LANES]
            kw = k_ref[pl.ds(r0, CK_KEYS), pb * LANES:(pb + 1) * LANES]
            zq = jnp.zeros_like(q)
            for hl in range(2):
                qh = jnp.where(lo, q, zq) if hl == 0 else jnp.where(lo, zq, q)
                s = (lax.dot_general(kw, qh, NT_DIMS, preferred_element_type=F32)
                     + bias_ref[2 * pb + hl])
                if masked:
                    s = jnp.where(valid, s, NEG)
                out.append((s, jnp.max(s, axis=0, keepdims=True)))
        return tuple(out)

    def finish(c2, ss):
        for hh in range(heads):
            s, m = ss[hh]
            p = jnp.exp2(s - m).astype(BF16)
            vw = jnp.concatenate(
                [jnp.concatenate([vt_ref[c2 + sb, hh * HEAD_DIM:(hh + 1) * HEAD_DIM, :]
                                  for sb in range(n_slabs)], axis=1), ones], axis=0)
            acc = jnp.dot(vw, p, preferred_element_type=F32)
            o_ref[c2, hh * HEAD_DIM:(hh + 1) * HEAD_DIM, :] = (
                acc[:HEAD_DIM] / acc[HEAD_DIM:HEAD_DIM + 1]).astype(o_ref.dtype)

    def sweep(lo_blk, hi_blk, masked):
        def body(c2, ss):
            ahead = scores(jnp.minimum(c2 + 1, hi_blk - 1), masked)
            finish(c2, ss)
            return ahead

        lax.fori_loop(lo_blk, hi_blk, body, scores(lo_blk, masked))

    n_masked = CK_PAD // CK_QB
    sweep(0, n_masked, True)
    sweep(n_masked, n_blocks, False)


def _ck_attention(q, kpad, vtpad, bias):
    b, s, d = q.shape
    sp = kpad.shape[1]
    nb = s // CK_QB
    heads = d // HEAD_DIM
    kern = functools.partial(_ck_kernel, n_blocks=nb, heads=heads)
    return pl.pallas_call(
        kern,
        out_shape=jax.ShapeDtypeStruct((b, nb, d, CK_QB), BF16),
        grid=(b,),
        in_specs=[pl.BlockSpec((None, s, d), lambda bi: (bi, 0, 0)),
                  pl.BlockSpec((None, sp, d), lambda bi: (bi, 0, 0)),
                  pl.BlockSpec((None, sp // CK_QB, d, CK_QB), lambda bi: (bi, 0, 0, 0)),
                  pl.BlockSpec((heads, CK_KEYS, CK_QB), lambda bi: (0, 0, 0))],
        out_specs=pl.BlockSpec((None, nb, d, CK_QB), lambda bi: (bi, 0, 0, 0)),
        compiler_params=_cparams(("parallel",)),
        name="ck_attn",
    )(q, kpad, vtpad, bias)


def _outproj_kernel(osb_ref, omla_ref, ock_ref, x_ref, mod_ref, gn_ref, wout_ref, nffn_ref,
                    *rest, moe):
    if moe:
        wr_ref, xo_ref, h2_ref, gates_ref = rest
    else:
        xo_ref, h2_ref = rest
    gn = gn_ref[...]

    def gnorm_t(ot, c0, w):
        ot = ot.astype(F32)
        ms = jnp.sum(ot * ot, axis=0, keepdims=True) * (1.0 / w)
        return ((ot * lax.rsqrt(ms + EPS)).T * gn[:, c0:c0 + w]).astype(BF16)

    ock = jnp.concatenate([ock_ref[sb] for sb in range(ock_ref.shape[0])], axis=1)
    merged = jnp.concatenate([gnorm_t(osb_ref[...], 0, D_SB), gnorm_t(omla_ref[...], D_SB, D_MLA),
                              gnorm_t(ock, D_SB + D_MLA, D_CK)], axis=-1)
    y = jnp.dot(merged, wout_ref[...], preferred_element_type=F32)
    xn = x_ref[...] + mod_ref[2:3, :] * y
    xo_ref[...] = xn
    h2 = _rms(xn, nffn_ref[...], D_MODEL) * (1.0 + mod_ref[4:5, :]) + mod_ref[3:4, :]
    h2_ref[...] = h2.astype(h2_ref.dtype)
    if moe:
        wr = wr_ref[...]
        w_hi = wr.astype(BF16)
        w_lo = (wr - w_hi.astype(F32)).astype(BF16)
        h_hi = h2.astype(BF16)
        h_lo = (h2 - h_hi.astype(F32)).astype(BF16)
        logits = (jnp.dot(h_hi, w_hi, preferred_element_type=F32)
                  + (jnp.dot(h_hi, w_lo, preferred_element_type=F32)
                     + jnp.dot(h_lo, w_hi, preferred_element_type=F32)))
        lane = lax.broadcasted_iota(jnp.int32, logits.shape, 1).astype(F32)
        logits = jnp.where(lane < N_EXPERTS, logits, -jnp.inf)
        m1 = jnp.max(logits, axis=-1, keepdims=True)
        i1 = jnp.min(jnp.where(logits == m1, lane, float(LANES)), axis=-1, keepdims=True)
        rest_l = jnp.where(lane == i1, -jnp.inf, logits)
        m2 = jnp.max(rest_l, axis=-1, keepdims=True)
        i2 = jnp.min(jnp.where(rest_l == m2, lane, float(LANES)), axis=-1, keepdims=True)
        e2 = jnp.exp(m2 - m1)
        w1 = 1.0 / (1.0 + e2)
        w2 = e2 / (1.0 + e2)
        gates_ref[...] = jnp.where(lane == 0.0, i1, jnp.where(lane == 1.0, i2,
                                   jnp.where(lane == 2.0, w1, jnp.where(lane == 3.0, w2, 0.0))))


def _outproj(osb, omla, ock, x2, mod_l, gn, wout, nffn, wr, seq):
    t = x2.shape[0]
    tm = 512
    tpb = seq // tm
    moe = wr is not None
    full = lambda a: pl.BlockSpec(a.shape, lambda i: (0,) * a.ndim)
    row = lambda w: pl.BlockSpec((tm, w), lambda i: (i, 0))
    colt = lambda w: pl.BlockSpec((None, w, tm), lambda i: (i // tpb, 0, i % tpb))
    ins = [osb, omla, ock, x2, mod_l, gn, wout, nffn]
    slabs = pl.BlockSpec((None, tm // CK_QB, D_CK, CK_QB), lambda i: (i // tpb, i % tpb, 0, 0))
    in_specs = [colt(D_SB), colt(D_MLA), slabs, row(D_MODEL),
                pl.BlockSpec((None, 6, D_MODEL), lambda i: (i // tpb, 0, 0)),
                full(gn), full(wout), full(nffn)]
    out_shape = [jax.ShapeDtypeStruct((t, D_MODEL), F32),
                 jax.ShapeDtypeStruct((t, D_MODEL), F32 if moe else BF16)]
    out_specs = [row(D_MODEL), row(D_MODEL)]
    if moe:
        ins.append(wr)
        in_specs.append(full(wr))
        out_shape.append(jax.ShapeDtypeStruct((t, LANES), F32))
        out_specs.append(row(LANES))
    return pl.pallas_call(
        functools.partial(_outproj_kernel, moe=moe),
        out_shape=tuple(out_shape),
        grid=(t // tm,),
        in_specs=in_specs,
        out_specs=tuple(out_specs),
        compiler_params=_cparams(("parallel",)),
        name="outproj_moe" if moe else "outproj",
    )(*ins)


FFN_TM = 512
MXU_DEPTH = 256
FFN_CHUNKS = ((0, 5 * MXU_DEPTH), (5 * MXU_DEPTH, 6 * MXU_DEPTH))
RESIDENT = pl.Buffered(1)


def _swiglu(h, wg_ref, wu_ref, wd_ref):
    y = None
    for c0, w in FFN_CHUNKS:
        g = jnp.dot(h, wg_ref[:, c0:c0 + w], preferred_element_type=F32)
        u = jnp.dot(h, wu_ref[:, c0:c0 + w], preferred_element_type=F32)
        a = (g * jax.nn.sigmoid(g) * u).astype(BF16)
        part = jnp.dot(a, wd_ref[c0:c0 + w, :], preferred_element_type=F32)
        y = part if y is None else y + part
    return y


def _ffn_kernel(h_ref, x_ref, mod_ref, wg_ref, wu_ref, wd_ref, o_ref):
    o_ref[...] = x_ref[...] + mod_ref[5:6, :] * _swiglu(h_ref[...], wg_ref, wu_ref, wd_ref)


def _ffn(h2, x2, mod_l, wg, wu, wd, seq):
    t = h2.shape[0]
    tm = FFN_TM
    tpb = seq // tm
    rmap = lambda i: (i, 0)
    const = lambda a: pl.BlockSpec(a.shape, lambda i: (0, 0), pipeline_mode=RESIDENT)
    return pl.pallas_call(
        _ffn_kernel,
        out_shape=jax.ShapeDtypeStruct((t, D_MODEL), F32),
        grid=(t // tm,),
        in_specs=[pl.BlockSpec((tm, D_MODEL), rmap), pl.BlockSpec((tm, D_MODEL), rmap),
                  pl.BlockSpec((None, 6, D_MODEL), lambda i: (i // tpb, 0, 0)),
                  const(wg), const(wu), const(wd)],
        out_specs=pl.BlockSpec((tm, D_MODEL), rmap),
        compiler_params=_cparams(("parallel",)),
        name="dense_ffn",
    )(h2, x2, mod_l, wg, wu, wd)


ROUTE_TM = 512


def _rank_kernel(route_ref, rank_ref, count_ref, base_ref):
    i = pl.program_id(0)

    @pl.when(i == 0)
    def _():
        base_ref[...] = jnp.zeros_like(base_ref)

    tm = route_ref.shape[0]
    route = route_ref[...]
    lane = lax.broadcasted_iota(jnp.int32, (tm, LANES), 1).astype(F32)
    sel1 = lane == route[:, 0:1]
    sel2 = lane == route[:, 1:2]
    hot = jnp.where(sel1 | sel2, 1.0, 0.0)
    r = lax.broadcasted_iota(jnp.int32, (tm, tm), 0)
    c = lax.broadcasted_iota(jnp.int32, (tm, tm), 1)
    before = jnp.where(c < r, 1.0, 0.0).astype(BF16)
    seen = jnp.dot(before, hot.astype(BF16), preferred_element_type=F32) + base_ref[...]
    r1 = jnp.sum(jnp.where(sel1, seen, 0.0), axis=-1, keepdims=True)
    r2 = jnp.sum(jnp.where(sel2, seen, 0.0), axis=-1, keepdims=True)
    rank_ref[...] = jnp.where(lane == 0.0, r1, jnp.where(lane == 1.0, r2, 0.0))
    base_ref[...] += jnp.sum(hot, axis=0, keepdims=True)
    count_ref[...] = base_ref[...]


def _moe_rank(route):
    t = route.shape[0]
    tm = ROUTE_TM
    return pl.pallas_call(
        _rank_kernel,
        out_shape=(jax.ShapeDtypeStruct((t, LANES), F32), jax.ShapeDtypeStruct((1, LANES), F32)),
        grid=(t // tm,),
        in_specs=[pl.BlockSpec((tm, LANES), lambda i: (i, 0))],
        out_specs=(pl.BlockSpec((tm, LANES), lambda i: (i, 0)),
                   pl.BlockSpec((1, LANES), lambda i: (0, 0))),
        scratch_shapes=[pltpu.VMEM((1, LANES), F32)],
        compiler_params=_cparams(("arbitrary",)),
        name="moe_rank",
    )(route)


def _dispatch_kernel(pos_ref, h_ref, xs_in_ref, xs_ref, sem):
    del xs_in_ref
    td = pos_ref.shape[1]

    def issue(r, carry):
        src = h_ref.at[pl.ds(r, 1), :]
        for k in range(2):
            pltpu.make_async_copy(src, xs_ref.at[pl.ds(pos_ref[k, r], 1), :],
                                  sem.at[k]).start(priority=k)
        return carry

    lax.fori_loop(0, td, issue, 0, unroll=8)
    for k in range(2):
        pltpu.make_async_copy(h_ref, xs_ref.at[pl.ds(0, td), :], sem.at[k]).wait()


def _moe_dispatch(pos, h2, n_rows):
    nt, _, td = pos.shape
    xs0 = jnp.zeros((n_rows, D_MODEL), F32)
    return pl.pallas_call(
        _dispatch_kernel,
        out_shape=jax.ShapeDtypeStruct((n_rows, D_MODEL), F32),
        grid=(nt,),
        in_specs=[pl.BlockSpec((None, 2, td), lambda i: (i, 0, 0), memory_space=pltpu.SMEM),
                  pl.BlockSpec((td, D_MODEL), lambda i: (i, 0)),
                  pl.BlockSpec(memory_space=pl.ANY)],
        out_specs=pl.BlockSpec(memory_space=pl.ANY),
        scratch_shapes=[pltpu.SemaphoreType.DMA((2,))],
        input_output_aliases={2: 0},
        compiler_params=_cparams(("arbitrary",)),
        name="moe_dispatch",
    )(pos, h2, xs0)


def _group_ffn_kernel(te_ref, nu_ref, xs_ref, wg_ref, wu_ref, wd_ref, y_ref):
    del te_ref
    i = pl.program_id(0)

    @pl.when(i < nu_ref[0])
    def _():
        y_ref[...] = _swiglu(xs_ref[...].astype(BF16), wg_ref, wu_ref, wd_ref)

    @pl.when(i >= nu_ref[0])
    def _():
        y_ref[...] = jnp.zeros_like(y_ref)


def _moe_group_ffn(tile_expert, n_used, xs, wg, wu, wd):
    n_rows = xs.shape[0]
    tm = FFN_TM
    row = lambda i, te, nu: (jnp.minimum(i, nu[0] - 1), 0)
    wspec = lambda a: pl.BlockSpec((None,) + a.shape[1:], lambda i, te, nu: (te[i], 0, 0))
    return pl.pallas_call(
        _group_ffn_kernel,
        out_shape=jax.ShapeDtypeStruct((n_rows, D_MODEL), F32),
        grid_spec=pltpu.PrefetchScalarGridSpec(
            num_scalar_prefetch=2,
            grid=(n_rows // tm,),
            in_specs=[pl.BlockSpec((tm, D_MODEL), row), wspec(wg), wspec(wu), wspec(wd)],
            out_specs=pl.BlockSpec((tm, D_MODEL), lambda i, te, nu: (i, 0))),
        compiler_params=_cparams(("arbitrary",)),
        name="moe_group_ffn",
    )(tile_expert, n_used, xs, wg, wu, wd)


def _combine_kernel(pos_ref, route_ref, x_ref, mod_ref, y_ref, o_ref, buf_ref, sem):
    tc = pos_ref.shape[1]

    def issue(r, carry):
        for k in range(2):
            pltpu.make_async_copy(y_ref.at[pl.ds(pos_ref[k, r], 1), :],
                                  buf_ref.at[k, pl.ds(r, 1), :], sem.at[k]).start(priority=k)
        return carry

    lax.fori_loop(0, tc, issue, 0, unroll=8)
    for k in range(2):
        pltpu.make_async_copy(y_ref.at[pl.ds(0, tc), :], buf_ref.at[k], sem.at[k]).wait()
    route = route_ref[...]
    y = route[:, 2:3] * buf_ref[0] + route[:, 3:4] * buf_ref[1]
    o_ref[...] = x_ref[...] + mod_ref[5:6, :] * y


def _moe_combine(pos, route, x2, mod_l, y, seq):
    nt, _, tc = pos.shape
    t = x2.shape[0]
    tpb = seq // tc
    return pl.pallas_call(
        _combine_kernel,
        out_shape=jax.ShapeDtypeStruct((t, D_MODEL), F32),
        grid=(nt,),
        in_specs=[pl.BlockSpec((None, 2, tc), lambda i: (i, 0, 0), memory_space=pltpu.SMEM),
                  pl.BlockSpec((tc, LANES), lambda i: (i, 0)),
                  pl.BlockSpec((tc, D_MODEL), lambda i: (i, 0)),
                  pl.BlockSpec((None, 6, D_MODEL), lambda i: (i // tpb, 0, 0)),
                  pl.BlockSpec(memory_space=pl.ANY)],
        out_specs=pl.BlockSpec((tc, D_MODEL), lambda i: (i, 0)),
        scratch_shapes=[pltpu.VMEM((2, tc, D_MODEL), F32), pltpu.SemaphoreType.DMA((2,))],
        compiler_params=_cparams(("arbitrary",)),
        name="moe_combine",
    )(pos, route, x2, mod_l, y)


def _moe(h2, x2, mod_l, route, wg, wu, wd, seq):
    t = h2.shape[0]
    ne = wg.shape[0]
    tm = FFN_TM
    n_tiles = (2 * t) // tm + ne
    n_rows = n_tiles * tm
    rank, count = _moe_rank(route)
    counts = count[0, :ne].astype(jnp.int32)
    tiles_per = (counts + tm - 1) // tm
    tile_end = jnp.cumsum(tiles_per)
    start = (tile_end - tiles_per) * tm
    n_used = tile_end[-1:]
    tile_ids = jnp.minimum(jnp.arange(n_tiles, dtype=jnp.int32), n_used[0] - 1)
    tile_expert = jnp.sum((tile_ids[:, None] >= tile_end[None, :]).astype(jnp.int32), axis=1)
    experts = route[:, 0:2].astype(jnp.int32)
    pos = jnp.take(start, experts) + rank[:, 0:2].astype(jnp.int32)
    pos = pos.reshape(t // ROUTE_TM, ROUTE_TM, 2).transpose(0, 2, 1)
    xs = _moe_dispatch(pos, h2, n_rows)
    y = _moe_group_ffn(tile_expert, n_used, xs, wg, wu, wd)
    return _moe_combine(pos, route, x2, mod_l, y, seq)


def _rot_half_cols(w):
    half = w.shape[-1] // 2
    return jnp.concatenate([-w[..., half:], w[..., :half]], axis=-1)


def _pad_cols(w, left, total):
    return jnp.pad(w, ((0, 0), (left, total - left - w.shape[-1])))


def _layout_w_in(w_in):
    sizes = (D_SB, D_SB, D_SB, MLA_Q_RANK, MLA_KV_RANK, MLA_ROPE, D_CK, D_CK, D_CK)
    splits = [int(v) for v in np.cumsum(sizes)[:-1]]
    sbq, sbk, sbv, cq, ckv, kpe, ckq, ckk, ckv2 = jnp.split(w_in, splits, axis=-1)
    kpa = _pad_cols(kpe, MLA_NOPE, LANES)
    kpb = _pad_cols(_rot_half_cols(kpe), MLA_NOPE, LANES)
    main = jnp.concatenate([sbq, sbk, cq, ckv, kpa, kpb, ckq, ckk], axis=-1)
    return main.astype(BF16), jnp.concatenate([sbv, ckv2], axis=-1).T.astype(BF16)


def _layout_w_q_up(w):
    r = w.shape[0]
    w3 = w.reshape(r, MLA_HEADS, MLA_QK)
    nope, pe = w3[..., :MLA_NOPE], w3[..., MLA_NOPE:]
    zpad = jnp.zeros((r, MLA_HEADS, LANES - MLA_QK), w.dtype)
    a = jnp.concatenate([nope, pe, zpad], axis=-1).reshape(r, MLA_HEADS * LANES)
    b = jnp.concatenate([jnp.zeros_like(nope), _rot_half_cols(pe), zpad], axis=-1)
    return jnp.concatenate([a, b.reshape(r, MLA_HEADS * LANES)], axis=-1).astype(BF16)


def _layout_w_kv_up(w):
    r = w.shape[0]
    w3 = w.reshape(r, MLA_HEADS, MLA_NOPE + MLA_V)
    kn = jnp.pad(w3[..., :MLA_NOPE], ((0, 0), (0, 0), (0, LANES - MLA_NOPE)))
    return (kn.reshape(r, MLA_HEADS * LANES).astype(BF16),
            w3[..., MLA_NOPE:].reshape(r, D_MLA).T.astype(BF16))


def _ck_bias_table(rel_bias):
    heads = rel_bias.shape[0]
    n_g = CHUNK + CK_KEYS - 1
    blocks = []
    for u in range(CK_QB // CHUNK):
        d = np.arange(n_g)
        idx = np.clip(d - (CK_KEYS - 1) + CK_PAD + CHUNK * u, -REL_CLIP, REL_CLIP) + REL_CLIP
        g = rel_bias.astype(F32)[:, idx] * LOG2E
        hank = jnp.tile(g, (1, CHUNK + 1))[:, :CHUNK * (n_g + 1)]
        hank = hank.reshape(heads, CHUNK, n_g + 1)[:, :, :CK_KEYS]
        blk = jnp.transpose(hank[:, :, ::-1], (0, 2, 1))
        j = np.arange(CK_KEYS) - CHUNK * u
        in_band = ((j >= 0) & (j < CK_WINDOW))[None, :, None]
        blocks.append(jnp.where(in_band, blk, NEG))
    return jnp.concatenate(blocks, axis=-1)


def kernel(x, c, positions, ada_w, ada_b, norm_mix, norm_ffn, w_in, mla_q_norm, w_q_up, mla_kv_norm, w_kv_up, mla_q_qknorm, mla_k_qknorm, ck_q_qknorm, ck_k_qknorm, ck_rel_bias, group_out_norm, w_out, ffn_w_gate, ffn_w_up, ffn_w_down, moe_router, moe_w_gate, moe_w_up, moe_w_down):
    bsz, seq, d = x.shape
    depth = ada_w.shape[0]
    t = bsz * seq
    x2 = x.reshape(t, d)

    mod = _ada(c, ada_w, ada_b).reshape(depth, bsz, 6, d)
    rc, rs = _rope_tables(positions)
    pad128 = lambda g: jnp.pad(g, (0, LANES - g.shape[0])).reshape(1, LANES)
    pair = lambda g: jnp.concatenate([g, g]).reshape(1, LANES)

    for layer in range(depth):
        win, wsbvt = _layout_w_in(w_in[layer])
        wq = _layout_w_q_up(w_q_up[layer])
        wk, wvt = _layout_w_kv_up(w_kv_up[layer])
        sbq, sbk, sbvt, mq, mk, mvt, cq, ck, cvt = _inproj(
            x2, mod[layer], norm_mix[layer].reshape(1, d), win, wsbvt, rc, rs,
            mla_q_norm[layer].reshape(1, -1), wq, mla_kv_norm[layer].reshape(1, -1), wk, wvt,
            pad128(mla_q_qknorm[layer]), pad128(mla_k_qknorm[layer]),
            pair(ck_q_qknorm[layer]), pair(ck_k_qknorm[layer]), seq)

        r3 = lambda a: a.reshape(bsz, seq, a.shape[-1])
        o_sb = _sb_attention(r3(sbq), r3(sbk), sbvt)
        o_mla = _mla_attention(_mla_score_bound(mla_q_qknorm[layer], mla_k_qknorm[layer]),
                               r3(mq), r3(mk), mvt)
        kpad = jnp.pad(r3(ck), ((0, 0), (CK_PAD, 0), (0, 0)))
        vtpad = jnp.pad(cvt, ((0, 0), (CK_PAD // CK_QB, 0), (0, 0), (0, 0)))
        o_ck = _ck_attention(r3(cq), kpad, vtpad, _ck_bias_table(ck_rel_bias[layer]))

        i = layer // 2
        is_moe = layer % 2 == 1
        wr = None
        if is_moe:
            wr = jnp.pad(moe_router[i], ((0, 0), (0, LANES - N_EXPERTS)))
        res = _outproj(o_sb, o_mla, o_ck, x2,
                       mod[layer], group_out_norm[layer].reshape(1, -1),
                       w_out[layer].astype(BF16), norm_ffn[layer].reshape(1, d), wr, seq)
        if is_moe:
            x2, h2, route = res
            x2 = _moe(h2, x2, mod[layer], route, moe_w_gate[i].astype(BF16),
                      moe_w_up[i].astype(BF16), moe_w_down[i].astype(BF16), seq)
        else:
            x2, h2 = res
            x2 = _ffn(h2, x2, mod[layer], ffn_w_gate[i].astype(BF16),
                      ffn_w_up[i].astype(BF16), ffn_w_down[i].astype(BF16), seq)
    return x2.reshape(bsz, seq, d)
```

```python
import functools

import numpy as np
import jax
import jax.numpy as jnp
from jax import lax
from jax.experimental import pallas as pl
from jax.experimental.pallas import tpu as pltpu

F32 = jnp.float32
BF16 = jnp.bfloat16

D_MODEL = 1024
CHUNK = 64
HEAD_DIM = 64
SB_HEADS = 4
MLA_HEADS = 8
MLA_Q_RANK = 256
MLA_KV_RANK = 128
MLA_NOPE = 64
MLA_ROPE = 32
MLA_V = 64
MLA_QK = MLA_NOPE + MLA_ROPE
ROPE_THETA = 10000.0
CK_HEADS = 4
CK_LEFT_CHUNKS = 8
CK_WINDOW = (CK_LEFT_CHUNKS + 1) * CHUNK
REL_CLIP = 128
D_SB = SB_HEADS * HEAD_DIM
D_MLA = MLA_HEADS * MLA_V
D_CK = CK_HEADS * HEAD_DIM
D_FF = 2816
N_EXPERTS = 8
EPS = 1e-6
NEG = -1e30

LANES = 128
VMEM_LIMIT = 56 * 1024 * 1024

C_SBQ, C_SBK = 0, 256
C_CQ = 512
C_CKV = 768
C_KPA = 896
C_KPB = 1024
C_CKQ, C_CKK = 1152, 1408
IN_EXT = 1664

NT_DIMS = (((1,), (1,)), ((), ()))
TN_DIMS = (((0,), (0,)), ((), ()))
TKV = 256
LOG2E = 1.4426950408889634


def _cparams(sem, vmem=VMEM_LIMIT):
    return pltpu.CompilerParams(dimension_semantics=sem, vmem_limit_bytes=vmem)


def _ada_kernel(c_ref, w_ref, b_ref, o_ref):
    c = c_ref[...]
    ca = c * jax.nn.sigmoid(c)
    o_ref[...] = jnp.dot(ca, w_ref[...], preferred_element_type=F32,
                         precision=lax.Precision.HIGHEST) + b_ref[...]


def _ada(c, ada_w, ada_b):
    depth, d, n = ada_w.shape
    b = c.shape[0]
    tn = 1536
    return pl.pallas_call(
        _ada_kernel,
        out_shape=jax.ShapeDtypeStruct((depth, b, n), F32),
        grid=(depth, n // tn),
        in_specs=[pl.BlockSpec((b, d), lambda l, j: (0, 0)),
                  pl.BlockSpec((None, d, tn), lambda l, j: (l, 0, j)),
                  pl.BlockSpec((None, 1, tn), lambda l, j: (l, 0, j))],
        out_specs=pl.BlockSpec((None, b, tn), lambda l, j: (l, 0, j)),
        compiler_params=_cparams(("parallel", "parallel")),
        name="ada_mod",
    )(c, ada_w, ada_b.reshape(depth, 1, n))


def _rope_kernel(pos_ref, invf_ref, c_ref, s_ref):
    ang = pos_ref[...] * invf_ref[...]
    lane = lax.broadcasted_iota(jnp.int32, ang.shape, 1)
    rope = (lane >= MLA_NOPE) & (lane < MLA_QK)
    c_ref[...] = jnp.where(rope, jnp.cos(ang), jnp.where(lane < MLA_NOPE, 1.0, 0.0))
    s_ref[...] = jnp.where(rope, jnp.sin(ang), 0.0)


def _rope_tables(positions):
    t = positions.size
    tm = 1024
    inv_freq = ROPE_THETA ** (-jnp.arange(0, MLA_ROPE, 2, dtype=F32) / MLA_ROPE)
    invf = jnp.zeros((1, LANES), F32)
    invf = invf.at[0, MLA_NOPE:MLA_NOPE + 16].set(inv_freq)
    invf = invf.at[0, MLA_NOPE + 16:MLA_QK].set(inv_freq)
    pos_b = jnp.broadcast_to(positions.reshape(t, 1).astype(F32), (t, LANES))
    return pl.pallas_call(
        _rope_kernel,
        out_shape=(jax.ShapeDtypeStruct((t, LANES), F32),
                   jax.ShapeDtypeStruct((t, LANES), F32)),
        grid=(t // tm,),
        in_specs=[pl.BlockSpec((tm, LANES), lambda i: (i, 0)),
                  pl.BlockSpec((1, LANES), lambda i: (0, 0))],
        out_specs=(pl.BlockSpec((tm, LANES), lambda i: (i, 0)),
                   pl.BlockSpec((tm, LANES), lambda i: (i, 0))),
        compiler_params=_cparams(("parallel",)),
        name="rope_tables",
    )(pos_b, invf)


def _rms(x, g, n):
    ms = jnp.sum(x * x, axis=-1, keepdims=True) * (1.0 / n)
    return x * lax.rsqrt(ms + EPS) * g


def _pair_rms(blk, g, scale):
    lane = lax.broadcasted_iota(jnp.int32, blk.shape, 1)
    lo = lane < HEAD_DIM
    sq = blk * blk
    s_all = jnp.sum(sq, axis=-1, keepdims=True)
    s_lo = jnp.sum(jnp.where(lo, sq, 0.0), axis=-1, keepdims=True)
    ms = jnp.where(lo, s_lo, s_all - s_lo) * (1.0 / HEAD_DIM)
    return blk * lax.rsqrt(ms + EPS) * (g * scale)


def _store_kv_blocks(ref, xt):
    slab = ref.shape[-1]
    for cblk in range(xt.shape[1] // slab):
        ref[cblk] = xt[:, cblk * slab:(cblk + 1) * slab].astype(ref.dtype)


def _inproj_kernel(x_ref, mod_ref, nw_ref, win_ref, wsbvt_ref, rc_ref, rs_ref, qn_ref, wq_ref,
                   kvn_ref, wk_ref, wvt_ref, gq_ref, gk_ref, gcq_ref, gck_ref,
                   sbq_ref, sbk_ref, sbvt_ref, mq_ref, mk_ref, mvt_ref,
                   cq_ref, ck_ref, cvt_ref):
    x = x_ref[...]
    xn = _rms(x, nw_ref[...], D_MODEL)
    h = (xn * (1.0 + mod_ref[1:2, :]) + mod_ref[0:1, :]).astype(BF16)
    proj = jnp.dot(h, win_ref[...], preferred_element_type=F32)

    sbq_ref[...] = (proj[:, C_SBQ:C_SBQ + D_SB] * (HEAD_DIM ** -0.5 * LOG2E)).astype(BF16)
    sbk_ref[...] = proj[:, C_SBK:C_SBK + D_SB].astype(BF16)
    vt = lax.dot_general(wsbvt_ref[...], h, NT_DIMS, preferred_element_type=F32)
    _store_kv_blocks(sbvt_ref, vt[:D_SB])
    _store_kv_blocks(cvt_ref, vt[D_SB:])

    rc = rc_ref[...]
    rs = rs_ref[...]
    cqn = _rms(proj[:, C_CQ:C_CQ + MLA_Q_RANK], qn_ref[...], MLA_Q_RANK).astype(BF16)
    qq = jnp.dot(cqn, wq_ref[...], preferred_element_type=F32)
    gq = gq_ref[...] * (MLA_QK ** -0.5 * LOG2E)
    nq = MLA_HEADS * LANES
    for hh in range(MLA_HEADS):
        a = qq[:, hh * LANES:(hh + 1) * LANES]
        b = qq[:, nq + hh * LANES:nq + (hh + 1) * LANES]
        qh = a * rc + b * rs
        mq_ref[:, hh * LANES:(hh + 1) * LANES] = _rms(qh, gq, MLA_QK).astype(BF16)

    ckvn = _rms(proj[:, C_CKV:C_CKV + MLA_KV_RANK], kvn_ref[...], MLA_KV_RANK).astype(BF16)
    kn = jnp.dot(ckvn, wk_ref[...], preferred_element_type=F32)
    _store_kv_blocks(mvt_ref, lax.dot_general(wvt_ref[...], ckvn, NT_DIMS,
                                              preferred_element_type=F32))
    kpe = proj[:, C_KPA:C_KPA + LANES] * rc + proj[:, C_KPB:C_KPB + LANES] * rs
    gk = gk_ref[...]
    for hh in range(MLA_HEADS):
        kh = kn[:, hh * LANES:(hh + 1) * LANES] + kpe
        mk_ref[:, hh * LANES:(hh + 1) * LANES] = _rms(kh, gk, MLA_QK).astype(BF16)

    gcq = gcq_ref[...]
    gck = gck_ref[...]
    for p in range(D_CK // LANES):
        qb = proj[:, C_CKQ + p * LANES:C_CKQ + (p + 1) * LANES]
        kb = proj[:, C_CKK + p * LANES:C_CKK + (p + 1) * LANES]
        cq_ref[:, p * LANES:(p + 1) * LANES] = _pair_rms(
            qb, gcq, HEAD_DIM ** -0.5 * LOG2E).astype(BF16)
        ck_ref[:, p * LANES:(p + 1) * LANES] = _pair_rms(kb, gck, 1.0).astype(BF16)


def _inproj(x2, mod_l, nw, win, wsbvt, rc, rs, qn, wq, kvn, wk, wvt, gq, gk, gcq, gck, seq):
    t = x2.shape[0]
    tm = 512
    tpb = seq // tm
    bsz = t // seq
    full = lambda a: pl.BlockSpec(a.shape, lambda i: (0,) * a.ndim)
    row = lambda w: pl.BlockSpec((tm, w), lambda i: (i, 0))
    rows = lambda w: (jax.ShapeDtypeStruct((t, w), BF16), row(w))
    kvt = lambda w, slab=TKV: (
        jax.ShapeDtypeStruct((bsz, seq // slab, w, slab), BF16),
        pl.BlockSpec((None, tm // slab, w, slab), lambda i: (i // tpb, i % tpb, 0, 0)))
    outs = [rows(D_SB), rows(D_SB), kvt(D_SB), rows(MLA_HEADS * LANES), rows(MLA_HEADS * LANES),
            kvt(D_MLA), rows(D_CK), rows(D_CK), kvt(D_CK, CK_QB)]
    return pl.pallas_call(
        _inproj_kernel,
        out_shape=tuple(o[0] for o in outs),
        grid=(t // tm,),
        in_specs=[row(D_MODEL),
                  pl.BlockSpec((None, 6, D_MODEL), lambda i: (i // tpb, 0, 0)),
                  full(nw), full(win), full(wsbvt), row(LANES), row(LANES), full(qn), full(wq),
                  full(kvn), full(wk), full(wvt), full(gq), full(gk), full(gcq), full(gck)],
        out_specs=tuple(o[1] for o in outs),
        compiler_params=_cparams(("parallel",)),
        name="inproj",
    )(x2, mod_l, nw, win, wsbvt, rc, rs, qn, wq, kvn, wk, wvt, gq, gk, gcq, gck)


def _kv_iotas(tk, tq):
    return (lax.broadcasted_iota(jnp.int32, (tk, tq), 0),
            lax.broadcasted_iota(jnp.int32, (tk, tq), 1))


SB_DEAD = 160.0


def _sb_kernel(q_ref, k_ref, vt_ref, o_ref, *, tq, heads):
    tk = TKV
    i = pl.program_id(2)
    lo = lax.broadcasted_iota(jnp.int32, (tq, LANES), 1) < HEAD_DIM
    qs = []
    for pb in range(heads // 2):
        q = q_ref[:, pb * LANES:(pb + 1) * LANES]
        zq = jnp.zeros_like(q)
        qs += [jnp.where(lo, q, zq), jnp.where(lo, zq, q)]
    krow, qcol = _kv_iotas(tk, tq)
    strict = krow < qcol
    ur, uc = _kv_iotas(tk, tk)
    later = jnp.where(uc > ur, 1.0, 0.0).astype(BF16)

    sign = jnp.uint32(0x80000000)

    def stage1(j, diag):
        r0 = pl.multiple_of(j * tk, tk)
        out = []
        for hh in range(heads):
            pb = hh // 2
            kb = k_ref[pl.ds(r0, tk), pb * LANES:(pb + 1) * LANES]
            y = lax.dot_general(kb, qs[hh], NT_DIMS, preferred_element_type=F32)
            neg_abs = pltpu.bitcast(pltpu.bitcast(y, jnp.uint32) | sign, F32)
            sp = jnp.maximum(y, 0.0) + jnp.log2(1.0 + jnp.exp2(neg_abs))
            spm = jnp.where(strict, sp, 0.0) if diag else sp
            tail = jnp.dot(later, spm.astype(BF16), preferred_element_type=F32)
            pre = y - sp - tail
            if diag:
                pre = jnp.where(strict, pre, NEG)
            out.append((pre, tail[0:1, :] + spm[0:1, :]))
        return tuple(out)

    def stage2(j, pres, carry):
        out = []
        for hh in range(heads):
            acc, car = carry[hh]
            pre, inc = pres[hh]
            w = jnp.exp2(pre - car)
            vb = vt_ref[j, hh * HEAD_DIM:(hh + 1) * HEAD_DIM, :]
            acc = acc + jnp.dot(vb, w.astype(BF16), preferred_element_type=F32)
            out.append((acc, car + inc))
        return tuple(out)

    init = tuple((jnp.zeros((HEAD_DIM, tq), F32), jnp.zeros((1, tq), F32))
                 for _ in range(heads))

    def live(carry, pres):
        mass = functools.reduce(jnp.minimum, [carry[hh][1] + pres[hh][1] for hh in range(heads)])
        return (jnp.min(mass) < SB_DEAD).astype(jnp.int32)

    def cond(state):
        jj, _, _, more = state
        return (jj < i) & (more > 0)

    def body(state):
        jj, carry, pres, _ = state
        j = i - jj
        ahead = stage1(j - 1, False)
        carry = stage2(j, pres, carry)
        return jj + 1, carry, ahead, live(carry, ahead)

    first = stage1(i, True)
    jj, carry, pres, _ = lax.while_loop(cond, body, (jnp.int32(0), init, first, live(init, first)))
    carry = stage2(i - jj, pres, carry)
    for hh in range(heads):
        o_ref[hh * HEAD_DIM:(hh + 1) * HEAD_DIM, :] = carry[hh][0].astype(o_ref.dtype)


def _sb_attention(q, k, vt):
    b, s, d = q.shape
    tq = TKV
    heads = 4
    w = heads * HEAD_DIM
    return pl.pallas_call(
        functools.partial(_sb_kernel, tq=tq, heads=heads),
        out_shape=jax.ShapeDtypeStruct((b, d, s), BF16),
        grid=(b, d // w, s // tq),
        in_specs=[pl.BlockSpec((None, tq, w), lambda bi, p, i: (bi, i, p)),
                  pl.BlockSpec((None, s, w), lambda bi, p, i: (bi, 0, p)),
                  pl.BlockSpec((None, s // TKV, w, TKV), lambda bi, p, i: (bi, 0, p, 0))],
        out_specs=pl.BlockSpec((None, w, tq), lambda bi, p, i: (bi, p, i)),
        compiler_params=_cparams(("parallel", "parallel", "arbitrary")),
        name="sb_attn",
    )(q, k, vt)


DEN_ROWS = 16


MLA_SAFE_BOUND = 60.0


def _mla_kernel(bound_ref, q_ref, k_ref, vt_ref, o_ref, *, tq, heads):
    tk = TKV
    i = pl.program_id(2)
    nsub = tq // tk
    krow, qcol = _kv_iotas(tk, tq)
    shift = CHUNK.bit_length() - 1
    ones = jnp.ones((DEN_ROWS, tk), BF16)
    base = i * nsub

    def sweep(bounded):
        qs = tuple(q_ref[:, hh * LANES:(hh + 1) * LANES] for hh in range(heads))
        ref = bound_ref[0]

        def scores(j, diag):
            r0 = pl.multiple_of(j * tk, tk)
            if diag is not None:
                allowed = (jnp.right_shift(krow + diag * tk, shift)
                           <= jnp.right_shift(qcol, shift))
            out = []
            for hh in range(heads):
                s = lax.dot_general(k_ref[pl.ds(r0, tk), hh * LANES:(hh + 1) * LANES], qs[hh],
                                    NT_DIMS, preferred_element_type=F32)
                if diag is not None:
                    s = jnp.where(allowed, s, NEG)
                if bounded:
                    out.append(jnp.exp2(s - ref).astype(BF16))
                else:
                    out.append((s, jnp.max(s, axis=0, keepdims=True)))
            return tuple(out)

        def update(j, ss, carry):
            out = []
            for hh in range(heads):
                vb = jnp.concatenate([vt_ref[j, hh * MLA_V:(hh + 1) * MLA_V, :], ones], axis=0)
                if bounded:
                    out.append(carry[hh] + jnp.dot(vb, ss[hh], preferred_element_type=F32))
                else:
                    m, acc = carry[hh]
                    s, smax = ss[hh]
                    m_new = jnp.maximum(m, smax)
                    alpha = jnp.exp2(m - m_new)
                    p = jnp.exp2(s - m_new).astype(BF16)
                    acc = alpha * acc + jnp.dot(vb, p, preferred_element_type=F32)
                    out.append((m_new, acc))
            return tuple(out)

        acc0 = jnp.zeros((MLA_V + DEN_ROWS, tq), F32)
        carry = tuple(acc0 if bounded else (jnp.full((1, tq), NEG, F32), acc0)
                      for _ in range(heads))
        group = nsub if bounded else 1

        def scores_n(j):
            return tuple(scores(jnp.maximum(j - g, 0), None) for g in range(group))

        ss = scores(base + nsub - 1, nsub - 1)
        for d in range(nsub - 1, -1, -1):
            s_ahead = scores(base + d - 1, d - 1) if d > 0 else scores_n(base - 1)
            carry = update(base + d, ss, carry)
            ss = s_ahead

        def body(jj, state):
            carry, ss = state
            j = base - 1 - group * jj
            s_ahead = scores_n(j - group)
            for g in range(group):
                carry = update(j - g, ss[g], carry)
            return carry, s_ahead

        carry, _ = lax.fori_loop(0, base // group, body, (carry, ss))
        for hh in range(heads):
            acc = carry[hh] if bounded else carry[hh][1]
            o_ref[hh * MLA_V:(hh + 1) * MLA_V, :] = (
                acc[:MLA_V] / acc[MLA_V:MLA_V + 1]).astype(o_ref.dtype)

    safe = bound_ref[0] <= MLA_SAFE_BOUND
    pl.when(safe)(functools.partial(sweep, True))
    pl.when(jnp.logical_not(safe))(functools.partial(sweep, False))


def _mla_attention(bound, q, k, vt):
    b, s, _ = q.shape
    tq = 2 * TKV
    heads = 4
    return pl.pallas_call(
        functools.partial(_mla_kernel, tq=tq, heads=heads),
        out_shape=jax.ShapeDtypeStruct((b, D_MLA, s), BF16),
        grid=(b, MLA_HEADS // heads, s // tq),
        in_specs=[pl.BlockSpec(memory_space=pltpu.SMEM),
                  pl.BlockSpec((None, tq, heads * LANES), lambda bi, p, i: (bi, i, p)),
                  pl.BlockSpec((None, s, heads * LANES), lambda bi, p, i: (bi, 0, p)),
                  pl.BlockSpec((None, s // TKV, heads * MLA_V, TKV),
                               lambda bi, p, i: (bi, 0, p, 0))],
        out_specs=pl.BlockSpec((None, heads * MLA_V, tq), lambda bi, p, i: (bi, p, i)),
        compiler_params=_cparams(("parallel", "parallel", "arbitrary")),
        name="mla_attn",
    )(bound, q, k, vt)


def _mla_score_bound(g_q, g_k):
    b = 1.02 * MLA_QK * (MLA_QK ** -0.5 * LOG2E) * jnp.max(jnp.abs(g_q)) * jnp.max(jnp.abs(g_k))
    return b.astype(F32).reshape(1)


CK_QB = 2 * CHUNK
CK_KEYS = CK_WINDOW + CHUNK
CK_PAD = CK_LEFT_CHUNKS * CHUNK


def _ck_kernel(width_ref, q_ref, k_ref, vt_ref, bias_ref, o_ref, *, n_blocks, heads):
    lo = lax.broadcasted_iota(jnp.int32, (CK_QB, LANES), 1) < HEAD_DIM
    krow = lax.broadcasted_iota(jnp.int32, (CK_KEYS, CK_QB), 0)
    n_slabs = CK_KEYS // CK_QB
    ones = jnp.ones((DEN_ROWS, CK_KEYS), BF16)

    def run(bounded):
        def scores(c2, masked):
            r0 = pl.multiple_of(c2 * CK_QB, CK_QB)
            if masked:
                valid = krow >= CK_PAD - c2 * CK_QB
            out = []
            for pb in range(heads // 2):
                q = q_ref[pl.ds(r0, CK_QB), pb * LANES:(pb + 1) * LANES]
                kw = k_ref[pl.ds(r0, CK_KEYS), pb * LANES:(pb + 1) * LANES]
                zq = jnp.zeros_like(q)
                for hl in range(2):
                    qh = jnp.where(lo, q, zq) if hl == 0 else jnp.where(lo, zq, q)
                    s = (lax.dot_general(kw, qh, NT_DIMS, preferred_element_type=F32)
                         + bias_ref[2 * pb + hl])
                    if masked:
                        s = jnp.where(valid, s, NEG)
                    if bounded:
                        out.append(jnp.exp2(s).astype(BF16))
                    else:
                        out.append((s, jnp.max(s, axis=0, keepdims=True)))
            return tuple(out)

        def finish(c2, ss):
            for hh in range(heads):
                p = ss[hh] if bounded else jnp.exp2(ss[hh][0] - ss[hh][1]).astype(BF16)
                vw = jnp.concatenate(
                    [jnp.concatenate([vt_ref[c2 + sb, hh * HEAD_DIM:(hh + 1) * HEAD_DIM, :]
                                      for sb in range(n_slabs)], axis=1), ones], axis=0)
                acc = jnp.dot(vw, p, preferred_element_type=F32)
                o_ref[c2, hh * HEAD_DIM:(hh + 1) * HEAD_DIM, :] = (
                    acc[:HEAD_DIM] / acc[HEAD_DIM:HEAD_DIM + 1]).astype(o_ref.dtype)

        def sweep(lo_blk, hi_blk, masked):
            def body(c2, ss):
                ahead = scores(jnp.minimum(c2 + 1, hi_blk - 1), masked)
                finish(c2, ss)
                return ahead

            lax.fori_loop(lo_blk, hi_blk, body, scores(lo_blk, masked))

        n_masked = CK_PAD // CK_QB
        sweep(0, n_masked, True)
        sweep(n_masked, n_blocks, False)

    safe = width_ref[0] <= 2.0 * MLA_SAFE_BOUND
    pl.when(safe)(functools.partial(run, True))
    pl.when(jnp.logical_not(safe))(functools.partial(run, False))


def _ck_attention(width, q, kpad, vtpad, bias):
    b, s, d = q.shape
    sp = kpad.shape[1]
    nb = s // CK_QB
    heads = d // HEAD_DIM
    kern = functools.partial(_ck_kernel, n_blocks=nb, heads=heads)
    return pl.pallas_call(
        kern,
        out_shape=jax.ShapeDtypeStruct((b, nb, d, CK_QB), BF16),
        grid=(b,),
        in_specs=[pl.BlockSpec(memory_space=pltpu.SMEM),
                  pl.BlockSpec((None, s, d), lambda bi: (bi, 0, 0)),
                  pl.BlockSpec((None, sp, d), lambda bi: (bi, 0, 0)),
                  pl.BlockSpec((None, sp // CK_QB, d, CK_QB), lambda bi: (bi, 0, 0, 0)),
                  pl.BlockSpec((heads, CK_KEYS, CK_QB), lambda bi: (0, 0, 0))],
        out_specs=pl.BlockSpec((None, nb, d, CK_QB), lambda bi: (bi, 0, 0, 0)),
        compiler_params=_cparams(("parallel",)),
        name="ck_attn",
    )(width, q, kpad, vtpad, bias)


def _outproj_kernel(osb_ref, omla_ref, ock_ref, x_ref, mod_ref, gn_ref, wout_ref, nffn_ref,
                    *rest, moe):
    if moe:
        wr_ref, xo_ref, h2_ref, gates_ref = rest
    else:
        xo_ref, h2_ref = rest
    gn = gn_ref[...]

    def gnorm_t(ot, c0, w):
        ot = ot.astype(F32)
        ms = jnp.sum(ot * ot, axis=0, keepdims=True) * (1.0 / w)
        return ((ot * lax.rsqrt(ms + EPS)).T * gn[:, c0:c0 + w]).astype(BF16)

    ock = jnp.concatenate([ock_ref[sb] for sb in range(ock_ref.shape[0])], axis=1)
    merged = jnp.concatenate([gnorm_t(osb_ref[...], 0, D_SB), gnorm_t(omla_ref[...], D_SB, D_MLA),
                              gnorm_t(ock, D_SB + D_MLA, D_CK)], axis=-1)
    y = jnp.dot(merged, wout_ref[...], preferred_element_type=F32)
    xn = x_ref[...] + mod_ref[2:3, :] * y
    xo_ref[...] = xn
    h2 = _rms(xn, nffn_ref[...], D_MODEL) * (1.0 + mod_ref[4:5, :]) + mod_ref[3:4, :]
    h2_ref[...] = h2.astype(h2_ref.dtype)
    if moe:
        wr = wr_ref[...]
        w_hi = wr.astype(BF16)
        w_lo = (wr - w_hi.astype(F32)).astype(BF16)
        h_hi = h2.astype(BF16)
        h_lo = (h2 - h_hi.astype(F32)).astype(BF16)
        logits = (jnp.dot(h_hi, w_hi, preferred_element_type=F32)
                  + (jnp.dot(h_hi, w_lo, preferred_element_type=F32)
                     + jnp.dot(h_lo, w_hi, preferred_element_type=F32)))
        lane = lax.broadcasted_iota(jnp.int32, logits.shape, 1).astype(F32)
        logits = jnp.where(lane < N_EXPERTS, logits, -jnp.inf)
        m1 = jnp.max(logits, axis=-1, keepdims=True)
        i1 = jnp.min(jnp.where(logits == m1, lane, float(LANES)), axis=-1, keepdims=True)
        rest_l = jnp.where(lane == i1, -jnp.inf, logits)
        m2 = jnp.max(rest_l, axis=-1, keepdims=True)
        i2 = jnp.min(jnp.where(rest_l == m2, lane, float(LANES)), axis=-1, keepdims=True)
        e2 = jnp.exp(m2 - m1)
        w1 = 1.0 / (1.0 + e2)
        w2 = e2 / (1.0 + e2)
        gates_ref[...] = jnp.where(lane == 0.0, i1, jnp.where(lane == 1.0, i2,
                                   jnp.where(lane == 2.0, w1, jnp.where(lane == 3.0, w2, 0.0))))


def _outproj(osb, omla, ock, x2, mod_l, gn, wout, nffn, wr, seq):
    t = x2.shape[0]
    tm = 512
    tpb = seq // tm
    moe = wr is not None
    full = lambda a: pl.BlockSpec(a.shape, lambda i: (0,) * a.ndim)
    row = lambda w: pl.BlockSpec((tm, w), lambda i: (i, 0))
    colt = lambda w: pl.BlockSpec((None, w, tm), lambda i: (i // tpb, 0, i % tpb))
    ins = [osb, omla, ock, x2, mod_l, gn, wout, nffn]
    slabs = pl.BlockSpec((None, tm // CK_QB, D_CK, CK_QB), lambda i: (i // tpb, i % tpb, 0, 0))
    in_specs = [colt(D_SB), colt(D_MLA), slabs, row(D_MODEL),
                pl.BlockSpec((None, 6, D_MODEL), lambda i: (i // tpb, 0, 0)),
                full(gn), full(wout), full(nffn)]
    out_shape = [jax.ShapeDtypeStruct((t, D_MODEL), F32),
                 jax.ShapeDtypeStruct((t, D_MODEL), F32 if moe else BF16)]
    out_specs = [row(D_MODEL), row(D_MODEL)]
    if moe:
        ins.append(wr)
        in_specs.append(full(wr))
        out_shape.append(jax.ShapeDtypeStruct((t, LANES), F32))
        out_specs.append(row(LANES))
    return pl.pallas_call(
        functools.partial(_outproj_kernel, moe=moe),
        out_shape=tuple(out_shape),
        grid=(t // tm,),
        in_specs=in_specs,
        out_specs=tuple(out_specs),
        compiler_params=_cparams(("parallel",)),
        name="outproj_moe" if moe else "outproj",
    )(*ins)


FFN_TM = 512
MXU_DEPTH = 256
FFN_CHUNKS = ((0, 5 * MXU_DEPTH), (5 * MXU_DEPTH, 6 * MXU_DEPTH))
RESIDENT = pl.Buffered(1)


def _swiglu(h, wg_ref, wu_ref, wd_ref):
    y = None
    for c0, w in FFN_CHUNKS:
        g = jnp.dot(h, wg_ref[:, c0:c0 + w], preferred_element_type=F32)
        u = jnp.dot(h, wu_ref[:, c0:c0 + w], preferred_element_type=F32)
        a = (g * jax.nn.sigmoid(g) * u).astype(BF16)
        part = jnp.dot(a, wd_ref[c0:c0 + w, :], preferred_element_type=F32)
        y = part if y is None else y + part
    return y


def _ffn_kernel(h_ref, x_ref, mod_ref, wg_ref, wu_ref, wd_ref, o_ref):
    o_ref[...] = x_ref[...] + mod_ref[5:6, :] * _swiglu(h_ref[...], wg_ref, wu_ref, wd_ref)


def _ffn(h2, x2, mod_l, wg, wu, wd, seq):
    t = h2.shape[0]
    tm = FFN_TM
    tpb = seq // tm
    rmap = lambda i: (i, 0)
    const = lambda a: pl.BlockSpec(a.shape, lambda i: (0, 0), pipeline_mode=RESIDENT)
    return pl.pallas_call(
        _ffn_kernel,
        out_shape=jax.ShapeDtypeStruct((t, D_MODEL), F32),
        grid=(t // tm,),
        in_specs=[pl.BlockSpec((tm, D_MODEL), rmap), pl.BlockSpec((tm, D_MODEL), rmap),
                  pl.BlockSpec((None, 6, D_MODEL), lambda i: (i // tpb, 0, 0)),
                  const(wg), const(wu), const(wd)],
        out_specs=pl.BlockSpec((tm, D_MODEL), rmap),
        compiler_params=_cparams(("parallel",)),
        name="dense_ffn",
    )(h2, x2, mod_l, wg, wu, wd)


ROUTE_TM = 512


def _rank_kernel(route_ref, rank_ref, count_ref, base_ref):
    i = pl.program_id(0)

    @pl.when(i == 0)
    def _():
        base_ref[...] = jnp.zeros_like(base_ref)

    tm = route_ref.shape[0]
    route = route_ref[...]
    lane = lax.broadcasted_iota(jnp.int32, (tm, LANES), 1).astype(F32)
    sel1 = lane == route[:, 0:1]
    sel2 = lane == route[:, 1:2]
    hot = jnp.where(sel1 | sel2, 1.0, 0.0)
    r = lax.broadcasted_iota(jnp.int32, (tm, tm), 0)
    c = lax.broadcasted_iota(jnp.int32, (tm, tm), 1)
    before = jnp.where(c < r, 1.0, 0.0).astype(BF16)
    seen = jnp.dot(before, hot.astype(BF16), preferred_element_type=F32) + base_ref[...]
    r1 = jnp.sum(jnp.where(sel1, seen, 0.0), axis=-1, keepdims=True)
    r2 = jnp.sum(jnp.where(sel2, seen, 0.0), axis=-1, keepdims=True)
    rank_ref[...] = jnp.where(lane == 0.0, r1, jnp.where(lane == 1.0, r2, 0.0))
    base_ref[...] += jnp.sum(hot, axis=0, keepdims=True)
    count_ref[...] = base_ref[...]


def _moe_rank(route):
    t = route.shape[0]
    tm = ROUTE_TM
    return pl.pallas_call(
        _rank_kernel,
        out_shape=(jax.ShapeDtypeStruct((t, LANES), F32), jax.ShapeDtypeStruct((1, LANES), F32)),
        grid=(t // tm,),
        in_specs=[pl.BlockSpec((tm, LANES), lambda i: (i, 0))],
        out_specs=(pl.BlockSpec((tm, LANES), lambda i: (i, 0)),
                   pl.BlockSpec((1, LANES), lambda i: (0, 0))),
        scratch_shapes=[pltpu.VMEM((1, LANES), F32)],
        compiler_params=_cparams(("arbitrary",)),
        name="moe_rank",
    )(route)


def _dispatch_kernel(pos_ref, h_ref, xs_in_ref, xs_ref, sem):
    del xs_in_ref
    td = pos_ref.shape[1]

    def issue(r, carry):
        src = h_ref.at[pl.ds(r, 1), :]
        for k in range(2):
            pltpu.make_async_copy(src, xs_ref.at[pl.ds(pos_ref[k, r], 1), :],
                                  sem.at[k]).start(priority=k)
        return carry

    lax.fori_loop(0, td, issue, 0, unroll=8)
    for k in range(2):
        pltpu.make_async_copy(h_ref, xs_ref.at[pl.ds(0, td), :], sem.at[k]).wait()


def _moe_dispatch(pos, h2, n_rows):
    nt, _, td = pos.shape
    xs0 = jnp.zeros((n_rows, D_MODEL), F32)
    return pl.pallas_call(
        _dispatch_kernel,
        out_shape=jax.ShapeDtypeStruct((n_rows, D_MODEL), F32),
        grid=(nt,),
        in_specs=[pl.BlockSpec((None, 2, td), lambda i: (i, 0, 0), memory_space=pltpu.SMEM),
                  pl.BlockSpec((td, D_MODEL), lambda i: (i, 0)),
                  pl.BlockSpec(memory_space=pl.ANY)],
        out_specs=pl.BlockSpec(memory_space=pl.ANY),
        scratch_shapes=[pltpu.SemaphoreType.DMA((2,))],
        input_output_aliases={2: 0},
        compiler_params=_cparams(("arbitrary",)),
        name="moe_dispatch",
    )(pos, h2, xs0)


def _group_ffn_kernel(te_ref, nu_ref, xs_ref, wg_ref, wu_ref, wd_ref, y_ref):
    del te_ref
    i = pl.program_id(0)

    @pl.when(i < nu_ref[0])
    def _():
        y_ref[...] = _swiglu(xs_ref[...].astype(BF16), wg_ref, wu_ref, wd_ref)

    @pl.when(i >= nu_ref[0])
    def _():
        y_ref[...] = jnp.zeros_like(y_ref)


def _moe_group_ffn(tile_expert, n_used, xs, wg, wu, wd):
    n_rows = xs.shape[0]
    tm = FFN_TM
    row = lambda i, te, nu: (jnp.minimum(i, nu[0] - 1), 0)
    wspec = lambda a: pl.BlockSpec((None,) + a.shape[1:], lambda i, te, nu: (te[i], 0, 0))
    return pl.pallas_call(
        _group_ffn_kernel,
        out_shape=jax.ShapeDtypeStruct((n_rows, D_MODEL), F32),
        grid_spec=pltpu.PrefetchScalarGridSpec(
            num_scalar_prefetch=2,
            grid=(n_rows // tm,),
            in_specs=[pl.BlockSpec((tm, D_MODEL), row), wspec(wg), wspec(wu), wspec(wd)],
            out_specs=pl.BlockSpec((tm, D_MODEL), lambda i, te, nu: (i, 0))),
        compiler_params=_cparams(("arbitrary",)),
        name="moe_group_ffn",
    )(tile_expert, n_used, xs, wg, wu, wd)


def _combine_kernel(pos_ref, route_ref, x_ref, mod_ref, y_ref, o_ref, buf_ref, sem):
    tc = pos_ref.shape[1]

    def issue(r, carry):
        for k in range(2):
            pltpu.make_async_copy(y_ref.at[pl.ds(pos_ref[k, r], 1), :],
                                  buf_ref.at[k, pl.ds(r, 1), :], sem.at[k]).start(priority=k)
        return carry

    lax.fori_loop(0, tc, issue, 0, unroll=8)
    for k in range(2):
        pltpu.make_async_copy(y_ref.at[pl.ds(0, tc), :], buf_ref.at[k], sem.at[k]).wait()
    route = route_ref[...]
    y = route[:, 2:3] * buf_ref[0] + route[:, 3:4] * buf_ref[1]
    o_ref[...] = x_ref[...] + mod_ref[5:6, :] * y


def _moe_combine(pos, route, x2, mod_l, y, seq):
    nt, _, tc = pos.shape
    t = x2.shape[0]
    tpb = seq // tc
    return pl.pallas_call(
        _combine_kernel,
        out_shape=jax.ShapeDtypeStruct((t, D_MODEL), F32),
        grid=(nt,),
        in_specs=[pl.BlockSpec((None, 2, tc), lambda i: (i, 0, 0), memory_space=pltpu.SMEM),
                  pl.BlockSpec((tc, LANES), lambda i: (i, 0)),
                  pl.BlockSpec((tc, D_MODEL), lambda i: (i, 0)),
                  pl.BlockSpec((None, 6, D_MODEL), lambda i: (i // tpb, 0, 0)),
                  pl.BlockSpec(memory_space=pl.ANY)],
        out_specs=pl.BlockSpec((tc, D_MODEL), lambda i: (i, 0)),
        scratch_shapes=[pltpu.VMEM((2, tc, D_MODEL), F32), pltpu.SemaphoreType.DMA((2,))],
        compiler_params=_cparams(("arbitrary",)),
        name="moe_combine",
    )(pos, route, x2, mod_l, y)


def _moe(h2, x2, mod_l, route, wg, wu, wd, seq):
    t = h2.shape[0]
    ne = wg.shape[0]
    tm = FFN_TM
    n_tiles = (2 * t) // tm + ne
    n_rows = n_tiles * tm
    rank, count = _moe_rank(route)
    counts = count[0, :ne].astype(jnp.int32)
    tiles_per = (counts + tm - 1) // tm
    tile_end = jnp.cumsum(tiles_per)
    start = (tile_end - tiles_per) * tm
    n_used = tile_end[-1:]
    tile_ids = jnp.minimum(jnp.arange(n_tiles, dtype=jnp.int32), n_used[0] - 1)
    tile_expert = jnp.sum((tile_ids[:, None] >= tile_end[None, :]).astype(jnp.int32), axis=1)
    experts = route[:, 0:2].astype(jnp.int32)
    pos = jnp.take(start, experts) + rank[:, 0:2].astype(jnp.int32)
    pos = pos.reshape(t // ROUTE_TM, ROUTE_TM, 2).transpose(0, 2, 1)
    xs = _moe_dispatch(pos, h2, n_rows)
    y = _moe_group_ffn(tile_expert, n_used, xs, wg, wu, wd)
    return _moe_combine(pos, route, x2, mod_l, y, seq)


def _rot_half_cols(w):
    half = w.shape[-1] // 2
    return jnp.concatenate([-w[..., half:], w[..., :half]], axis=-1)


def _pad_cols(w, left, total):
    return jnp.pad(w, ((0, 0), (left, total - left - w.shape[-1])))


def _layout_w_in(w_in):
    sizes = (D_SB, D_SB, D_SB, MLA_Q_RANK, MLA_KV_RANK, MLA_ROPE, D_CK, D_CK, D_CK)
    splits = [int(v) for v in np.cumsum(sizes)[:-1]]
    sbq, sbk, sbv, cq, ckv, kpe, ckq, ckk, ckv2 = jnp.split(w_in, splits, axis=-1)
    kpa = _pad_cols(kpe, MLA_NOPE, LANES)
    kpb = _pad_cols(_rot_half_cols(kpe), MLA_NOPE, LANES)
    main = jnp.concatenate([sbq, sbk, cq, ckv, kpa, kpb, ckq, ckk], axis=-1)
    return main.astype(BF16), jnp.concatenate([sbv, ckv2], axis=-1).T.astype(BF16)


def _layout_w_q_up(w):
    r = w.shape[0]
    w3 = w.reshape(r, MLA_HEADS, MLA_QK)
    nope, pe = w3[..., :MLA_NOPE], w3[..., MLA_NOPE:]
    zpad = jnp.zeros((r, MLA_HEADS, LANES - MLA_QK), w.dtype)
    a = jnp.concatenate([nope, pe, zpad], axis=-1).reshape(r, MLA_HEADS * LANES)
    b = jnp.concatenate([jnp.zeros_like(nope), _rot_half_cols(pe), zpad], axis=-1)
    return jnp.concatenate([a, b.reshape(r, MLA_HEADS * LANES)], axis=-1).astype(BF16)


def _layout_w_kv_up(w):
    r = w.shape[0]
    w3 = w.reshape(r, MLA_HEADS, MLA_NOPE + MLA_V)
    kn = jnp.pad(w3[..., :MLA_NOPE], ((0, 0), (0, 0), (0, LANES - MLA_NOPE)))
    return (kn.reshape(r, MLA_HEADS * LANES).astype(BF16),
            w3[..., MLA_NOPE:].reshape(r, D_MLA).T.astype(BF16))


def _ck_bias_table(rel_bias, g_q, g_k):
    qk = (1.02 * HEAD_DIM * (HEAD_DIM ** -0.5 * LOG2E)
          * jnp.max(jnp.abs(g_q)) * jnp.max(jnp.abs(g_k)))
    b_hi = jnp.max(rel_bias) * LOG2E
    b_lo = jnp.min(rel_bias) * LOG2E
    width = (2.0 * qk + (b_hi - b_lo)).astype(F32).reshape(1)
    rel_bias = rel_bias.astype(F32) - (qk + b_hi) / LOG2E
    heads = rel_bias.shape[0]
    n_g = CHUNK + CK_KEYS - 1
    blocks = []
    for u in range(CK_QB // CHUNK):
        d = np.arange(n_g)
        idx = np.clip(d - (CK_KEYS - 1) + CK_PAD + CHUNK * u, -REL_CLIP, REL_CLIP) + REL_CLIP
        g = rel_bias.astype(F32)[:, idx] * LOG2E
        hank = jnp.tile(g, (1, CHUNK + 1))[:, :CHUNK * (n_g + 1)]
        hank = hank.reshape(heads, CHUNK, n_g + 1)[:, :, :CK_KEYS]
        blk = jnp.transpose(hank[:, :, ::-1], (0, 2, 1))
        j = np.arange(CK_KEYS) - CHUNK * u
        in_band = ((j >= 0) & (j < CK_WINDOW))[None, :, None]
        blocks.append(jnp.where(in_band, blk, NEG))
    return jnp.concatenate(blocks, axis=-1), width


def kernel(x, c, positions, ada_w, ada_b, norm_mix, norm_ffn, w_in, mla_q_norm, w_q_up, mla_kv_norm, w_kv_up, mla_q_qknorm, mla_k_qknorm, ck_q_qknorm, ck_k_qknorm, ck_rel_bias, group_out_norm, w_out, ffn_w_gate, ffn_w_up, ffn_w_down, moe_router, moe_w_gate, moe_w_up, moe_w_down):
    bsz, seq, d = x.shape
    depth = ada_w.shape[0]
    t = bsz * seq
    x2 = x.reshape(t, d)

    mod = _ada(c, ada_w, ada_b).reshape(depth, bsz, 6, d)
    rc, rs = _rope_tables(positions)
    pad128 = lambda g: jnp.pad(g, (0, LANES - g.shape[0])).reshape(1, LANES)
    pair = lambda g: jnp.concatenate([g, g]).reshape(1, LANES)

    for layer in range(depth):
        win, wsbvt = _layout_w_in(w_in[layer])
        wq = _layout_w_q_up(w_q_up[layer])
        wk, wvt = _layout_w_kv_up(w_kv_up[layer])
        sbq, sbk, sbvt, mq, mk, mvt, cq, ck, cvt = _inproj(
            x2, mod[layer], norm_mix[layer].reshape(1, d), win, wsbvt, rc, rs,
            mla_q_norm[layer].reshape(1, -1), wq, mla_kv_norm[layer].reshape(1, -1), wk, wvt,
            pad128(mla_q_qknorm[layer]), pad128(mla_k_qknorm[layer]),
            pair(ck_q_qknorm[layer]), pair(ck_k_qknorm[layer]), seq)

        r3 = lambda a: a.reshape(bsz, seq, a.shape[-1])
        o_sb = _sb_attention(r3(sbq), r3(sbk), sbvt)
        o_mla = _mla_attention(_mla_score_bound(mla_q_qknorm[layer], mla_k_qknorm[layer]),
                               r3(mq), r3(mk), mvt)
        kpad = jnp.pad(r3(ck), ((0, 0), (CK_PAD, 0), (0, 0)))
        vtpad = jnp.pad(cvt, ((0, 0), (CK_PAD // CK_QB, 0), (0, 0), (0, 0)))
        ck_bias, ck_width = _ck_bias_table(ck_rel_bias[layer], ck_q_qknorm[layer],
                                           ck_k_qknorm[layer])
        o_ck = _ck_attention(ck_width, r3(cq), kpad, vtpad, ck_bias)

        i = layer // 2
        is_moe = layer % 2 == 1
        wr = None
        if is_moe:
            wr = jnp.pad(moe_router[i], ((0, 0), (0, LANES - N_EXPERTS)))
        res = _outproj(o_sb, o_mla, o_ck, x2,
                       mod[layer], group_out_norm[layer].reshape(1, -1),
                       w_out[layer].astype(BF16), norm_ffn[layer].reshape(1, d), wr, seq)
        if is_moe:
            x2, h2, route = res
            x2 = _moe(h2, x2, mod[layer], route, moe_w_gate[i].astype(BF16),
                      moe_w_up[i].astype(BF16), moe_w_down[i].astype(BF16), seq)
        else:
            x2, h2 = res
            x2 = _ffn(h2, x2, mod[layer], ffn_w_gate[i].astype(BF16),
                      ffn_w_up[i].astype(BF16), ffn_w_down[i].astype(BF16), seq)
    return x2.reshape(bsz, seq, d)
```

```python
import functools

import numpy as np
import jax
import jax.numpy as jnp
from jax import lax
from jax.experimental import pallas as pl
from jax.experimental.pallas import tpu as pltpu

F32 = jnp.float32
BF16 = jnp.bfloat16

D_MODEL = 1024
CHUNK = 64
HEAD_DIM = 64
SB_HEADS = 4
MLA_HEADS = 8
MLA_Q_RANK = 256
MLA_KV_RANK = 128
MLA_NOPE = 64
MLA_ROPE = 32
MLA_V = 64
MLA_QK = MLA_NOPE + MLA_ROPE
ROPE_THETA = 10000.0
CK_HEADS = 4
CK_LEFT_CHUNKS = 8
CK_WINDOW = (CK_LEFT_CHUNKS + 1) * CHUNK
REL_CLIP = 128
D_SB = SB_HEADS * HEAD_DIM
D_MLA = MLA_HEADS * MLA_V
D_CK = CK_HEADS * HEAD_DIM
D_FF = 2816
N_EXPERTS = 8
EPS = 1e-6
NEG = -1e30

LANES = 128
VMEM_LIMIT = 56 * 1024 * 1024

C_SBQ, C_SBK = 0, 256
C_CQ = 512
C_CKV = 768
C_KPA = 896
C_KPB = 1024
C_CKQ, C_CKK = 1152, 1408
IN_EXT = 1664

NT_DIMS = (((1,), (1,)), ((), ()))
TN_DIMS = (((0,), (0,)), ((), ()))
TKV = 256
LOG2E = 1.4426950408889634


def _cparams(sem, vmem=VMEM_LIMIT):
    return pltpu.CompilerParams(dimension_semantics=sem, vmem_limit_bytes=vmem)


def _ada_kernel(c_ref, w_ref, b_ref, o_ref):
    c = c_ref[...]
    ca = c * jax.nn.sigmoid(c)
    o_ref[...] = jnp.dot(ca, w_ref[...], preferred_element_type=F32,
                         precision=lax.Precision.HIGHEST) + b_ref[...]


def _ada(c, ada_w, ada_b):
    depth, d, n = ada_w.shape
    b = c.shape[0]
    tn = 1536
    return pl.pallas_call(
        _ada_kernel,
        out_shape=jax.ShapeDtypeStruct((depth, b, n), F32),
        grid=(depth, n // tn),
        in_specs=[pl.BlockSpec((b, d), lambda l, j: (0, 0)),
                  pl.BlockSpec((None, d, tn), lambda l, j: (l, 0, j)),
                  pl.BlockSpec((None, 1, tn), lambda l, j: (l, 0, j))],
        out_specs=pl.BlockSpec((None, b, tn), lambda l, j: (l, 0, j)),
        compiler_params=_cparams(("parallel", "parallel")),
        name="ada_mod",
    )(c, ada_w, ada_b.reshape(depth, 1, n))


def _rope_kernel(pos_ref, invf_ref, c_ref, s_ref):
    ang = pos_ref[...] * invf_ref[...]
    lane = lax.broadcasted_iota(jnp.int32, ang.shape, 1)
    rope = (lane >= MLA_NOPE) & (lane < MLA_QK)
    c_ref[...] = jnp.where(rope, jnp.cos(ang), jnp.where(lane < MLA_NOPE, 1.0, 0.0))
    s_ref[...] = jnp.where(rope, jnp.sin(ang), 0.0)


def _rope_tables(positions):
    t = positions.size
    tm = 1024
    inv_freq = ROPE_THETA ** (-jnp.arange(0, MLA_ROPE, 2, dtype=F32) / MLA_ROPE)
    invf = jnp.zeros((1, LANES), F32)
    invf = invf.at[0, MLA_NOPE:MLA_NOPE + 16].set(inv_freq)
    invf = invf.at[0, MLA_NOPE + 16:MLA_QK].set(inv_freq)
    pos_b = jnp.broadcast_to(positions.reshape(t, 1).astype(F32), (t, LANES))
    return pl.pallas_call(
        _rope_kernel,
        out_shape=(jax.ShapeDtypeStruct((t, LANES), F32),
                   jax.ShapeDtypeStruct((t, LANES), F32)),
        grid=(t // tm,),
        in_specs=[pl.BlockSpec((tm, LANES), lambda i: (i, 0)),
                  pl.BlockSpec((1, LANES), lambda i: (0, 0))],
        out_specs=(pl.BlockSpec((tm, LANES), lambda i: (i, 0)),
                   pl.BlockSpec((tm, LANES), lambda i: (i, 0))),
        compiler_params=_cparams(("parallel",)),
        name="rope_tables",
    )(pos_b, invf)


def _rms(x, g, n):
    ms = jnp.sum(x * x, axis=-1, keepdims=True) * (1.0 / n)
    return x * lax.rsqrt(ms + EPS) * g


def _pair_rms(blk, g, scale):
    lane = lax.broadcasted_iota(jnp.int32, blk.shape, 1)
    lo = lane < HEAD_DIM
    sq = blk * blk
    s_all = jnp.sum(sq, axis=-1, keepdims=True)
    s_lo = jnp.sum(jnp.where(lo, sq, 0.0), axis=-1, keepdims=True)
    ms = jnp.where(lo, s_lo, s_all - s_lo) * (1.0 / HEAD_DIM)
    return blk * lax.rsqrt(ms + EPS) * (g * scale)


def _store_kv_blocks(ref, xt):
    slab = ref.shape[-1]
    for cblk in range(xt.shape[1] // slab):
        ref[cblk] = xt[:, cblk * slab:(cblk + 1) * slab].astype(ref.dtype)


def _inproj_kernel(x_ref, mod_ref, nw_ref, win_ref, wsbvt_ref, rc_ref, rs_ref, qn_ref, wq_ref,
                   kvn_ref, wk_ref, wvt_ref, gq_ref, gk_ref, gcq_ref, gck_ref,
                   sbq_ref, sbk_ref, sbvt_ref, mq_ref, mk_ref, mvt_ref,
                   cq_ref, ck_ref, cvt_ref):
    x = x_ref[...]
    xn = _rms(x, nw_ref[...], D_MODEL)
    h = (xn * (1.0 + mod_ref[1:2, :]) + mod_ref[0:1, :]).astype(BF16)
    proj = jnp.dot(h, win_ref[...], preferred_element_type=F32)

    sbq_ref[...] = (proj[:, C_SBQ:C_SBQ + D_SB] * (HEAD_DIM ** -0.5 * LOG2E)).astype(BF16)
    sbk_ref[...] = proj[:, C_SBK:C_SBK + D_SB].astype(BF16)
    vt = lax.dot_general(wsbvt_ref[...], h, NT_DIMS, preferred_element_type=F32)
    _store_kv_blocks(sbvt_ref, vt[:D_SB])
    _store_kv_blocks(cvt_ref, vt[D_SB:])

    rc = rc_ref[...]
    rs = rs_ref[...]
    cqn = _rms(proj[:, C_CQ:C_CQ + MLA_Q_RANK], qn_ref[...], MLA_Q_RANK).astype(BF16)
    qq = jnp.dot(cqn, wq_ref[...], preferred_element_type=F32)
    gq = gq_ref[...] * (MLA_QK ** -0.5 * LOG2E)
    nq = MLA_HEADS * LANES
    for hh in range(MLA_HEADS):
        a = qq[:, hh * LANES:(hh + 1) * LANES]
        b = qq[:, nq + hh * LANES:nq + (hh + 1) * LANES]
        qh = a * rc + b * rs
        mq_ref[:, hh * LANES:(hh + 1) * LANES] = _rms(qh, gq, MLA_QK).astype(BF16)

    ckvn = _rms(proj[:, C_CKV:C_CKV + MLA_KV_RANK], kvn_ref[...], MLA_KV_RANK).astype(BF16)
    kn = jnp.dot(ckvn, wk_ref[...], preferred_element_type=F32)
    _store_kv_blocks(mvt_ref, lax.dot_general(wvt_ref[...], ckvn, NT_DIMS,
                                              preferred_element_type=F32))
    kpe = proj[:, C_KPA:C_KPA + LANES] * rc + proj[:, C_KPB:C_KPB + LANES] * rs
    gk = gk_ref[...]
    for hh in range(MLA_HEADS):
        kh = kn[:, hh * LANES:(hh + 1) * LANES] + kpe
        mk_ref[:, hh * LANES:(hh + 1) * LANES] = _rms(kh, gk, MLA_QK).astype(BF16)

    gcq = gcq_ref[...]
    gck = gck_ref[...]
    for p in range(D_CK // LANES):
        qb = proj[:, C_CKQ + p * LANES:C_CKQ + (p + 1) * LANES]
        kb = proj[:, C_CKK + p * LANES:C_CKK + (p + 1) * LANES]
        cq_ref[:, p * LANES:(p + 1) * LANES] = _pair_rms(
            qb, gcq, HEAD_DIM ** -0.5 * LOG2E).astype(BF16)
        ck_ref[:, p * LANES:(p + 1) * LANES] = _pair_rms(kb, gck, 1.0).astype(BF16)


def _inproj(x2, mod_l, nw, win, wsbvt, rc, rs, qn, wq, kvn, wk, wvt, gq, gk, gcq, gck, seq):
    t = x2.shape[0]
    tm = 512
    tpb = seq // tm
    bsz = t // seq
    full = lambda a: pl.BlockSpec(a.shape, lambda i: (0,) * a.ndim)
    row = lambda w: pl.BlockSpec((tm, w), lambda i: (i, 0))
    rows = lambda w: (jax.ShapeDtypeStruct((t, w), BF16), row(w))
    kvt = lambda w, slab=TKV: (
        jax.ShapeDtypeStruct((bsz, seq // slab, w, slab), BF16),
        pl.BlockSpec((None, tm // slab, w, slab), lambda i: (i // tpb, i % tpb, 0, 0)))
    outs = [rows(D_SB), rows(D_SB), kvt(D_SB), rows(MLA_HEADS * LANES), rows(MLA_HEADS * LANES),
            kvt(D_MLA), rows(D_CK), rows(D_CK), kvt(D_CK, CK_QB)]
    return pl.pallas_call(
        _inproj_kernel,
        out_shape=tuple(o[0] for o in outs),
        grid=(t // tm,),
        in_specs=[row(D_MODEL),
                  pl.BlockSpec((None, 6, D_MODEL), lambda i: (i // tpb, 0, 0)),
                  full(nw), full(win), full(wsbvt), row(LANES), row(LANES), full(qn), full(wq),
                  full(kvn), full(wk), full(wvt), full(gq), full(gk), full(gcq), full(gck)],
        out_specs=tuple(o[1] for o in outs),
        compiler_params=_cparams(("parallel",)),
        name="inproj",
    )(x2, mod_l, nw, win, wsbvt, rc, rs, qn, wq, kvn, wk, wvt, gq, gk, gcq, gck)


def _kv_iotas(tk, tq):
    return (lax.broadcasted_iota(jnp.int32, (tk, tq), 0),
            lax.broadcasted_iota(jnp.int32, (tk, tq), 1))


SB_DEAD = 160.0


def _sb_kernel(q_ref, k_ref, vt_ref, o_ref, *, tq, heads):
    tk = TKV
    i = pl.program_id(2)
    lo = lax.broadcasted_iota(jnp.int32, (tq, LANES), 1) < HEAD_DIM
    qs = []
    for pb in range(heads // 2):
        q = q_ref[:, pb * LANES:(pb + 1) * LANES]
        zq = jnp.zeros_like(q)
        qs += [jnp.where(lo, q, zq), jnp.where(lo, zq, q)]
    krow, qcol = _kv_iotas(tk, tq)
    strict = krow < qcol
    ur, uc = _kv_iotas(tk, tk)
    later = jnp.where(uc > ur, 1.0, 0.0).astype(BF16)

    sign = jnp.uint32(0x80000000)

    def stage1(j, diag):
        r0 = pl.multiple_of(j * tk, tk)
        out = []
        for hh in range(heads):
            pb = hh // 2
            kb = k_ref[pl.ds(r0, tk), pb * LANES:(pb + 1) * LANES]
            y = lax.dot_general(kb, qs[hh], NT_DIMS, preferred_element_type=F32)
            neg_abs = pltpu.bitcast(pltpu.bitcast(y, jnp.uint32) | sign, F32)
            sp = jnp.maximum(y, 0.0) + jnp.log2(1.0 + jnp.exp2(neg_abs))
            spm = jnp.where(strict, sp, 0.0) if diag else sp
            tail = jnp.dot(later, spm.astype(BF16), preferred_element_type=F32)
            pre = y - sp - tail
            if diag:
                pre = jnp.where(strict, pre, NEG)
            out.append((jnp.exp2(pre).astype(BF16), tail[0:1, :] + spm[0:1, :]))
        return tuple(out)

    def stage2(j, pres, carry):
        out = []
        for hh in range(heads):
            acc, car = carry[hh]
            e, inc = pres[hh]
            vb = vt_ref[j, hh * HEAD_DIM:(hh + 1) * HEAD_DIM, :]
            acc = acc + jnp.exp2(-car) * jnp.dot(vb, e, preferred_element_type=F32)
            out.append((acc, car + inc))
        return tuple(out)

    init = tuple((jnp.zeros((HEAD_DIM, tq), F32), jnp.zeros((1, tq), F32))
                 for _ in range(heads))

    def live(carry, pres):
        mass = functools.reduce(jnp.minimum, [carry[hh][1] + pres[hh][1] for hh in range(heads)])
        return (jnp.min(mass) < SB_DEAD).astype(jnp.int32)

    def cond(state):
        jj, _, _, more = state
        return (jj < i) & (more > 0)

    def body(state):
        jj, carry, pres, _ = state
        j = i - jj
        ahead = stage1(j - 1, False)
        carry = stage2(j, pres, carry)
        return jj + 1, carry, ahead, live(carry, ahead)

    first = stage1(i, True)
    jj, carry, pres, _ = lax.while_loop(cond, body, (jnp.int32(0), init, first, live(init, first)))
    carry = stage2(i - jj, pres, carry)
    for hh in range(heads):
        o_ref[hh * HEAD_DIM:(hh + 1) * HEAD_DIM, :] = carry[hh][0].astype(o_ref.dtype)


def _sb_attention(q, k, vt):
    b, s, d = q.shape
    tq = TKV
    heads = 4
    w = heads * HEAD_DIM
    return pl.pallas_call(
        functools.partial(_sb_kernel, tq=tq, heads=heads),
        out_shape=jax.ShapeDtypeStruct((b, d, s), BF16),
        grid=(b, d // w, s // tq),
        in_specs=[pl.BlockSpec((None, tq, w), lambda bi, p, i: (bi, i, p)),
                  pl.BlockSpec((None, s, w), lambda bi, p, i: (bi, 0, p)),
                  pl.BlockSpec((None, s // TKV, w, TKV), lambda bi, p, i: (bi, 0, p, 0))],
        out_specs=pl.BlockSpec((None, w, tq), lambda bi, p, i: (bi, p, i)),
        compiler_params=_cparams(("parallel", "parallel", "arbitrary")),
        name="sb_attn",
    )(q, k, vt)


DEN_ROWS = 16


MLA_SAFE_BOUND = 60.0


def _mla_kernel(bound_ref, q_ref, k_ref, vt_ref, o_ref, *, tq, heads):
    tk = TKV
    i = pl.program_id(2)
    nsub = tq // tk
    krow, qcol = _kv_iotas(tk, tq)
    shift = CHUNK.bit_length() - 1
    ones = jnp.ones((DEN_ROWS, tk), BF16)
    base = i * nsub

    def sweep(bounded):
        qs = tuple(q_ref[:, hh * LANES:(hh + 1) * LANES] for hh in range(heads))
        ref = bound_ref[0]

        def scores(j, diag):
            r0 = pl.multiple_of(j * tk, tk)
            if diag is not None:
                allowed = (jnp.right_shift(krow + diag * tk, shift)
                           <= jnp.right_shift(qcol, shift))
            out = []
            for hh in range(heads):
                s = lax.dot_general(k_ref[pl.ds(r0, tk), hh * LANES:(hh + 1) * LANES], qs[hh],
                                    NT_DIMS, preferred_element_type=F32)
                if diag is not None:
                    s = jnp.where(allowed, s, NEG)
                if bounded:
                    out.append(jnp.exp2(s - ref).astype(BF16))
                else:
                    out.append((s, jnp.max(s, axis=0, keepdims=True)))
            return tuple(out)

        def update(j, ss, carry):
            out = []
            for hh in range(heads):
                vb = jnp.concatenate([vt_ref[j, hh * MLA_V:(hh + 1) * MLA_V, :], ones], axis=0)
                if bounded:
                    out.append(carry[hh] + jnp.dot(vb, ss[hh], preferred_element_type=F32))
                else:
                    m, acc = carry[hh]
                    s, smax = ss[hh]
                    m_new = jnp.maximum(m, smax)
                    alpha = jnp.exp2(m - m_new)
                    p = jnp.exp2(s - m_new).astype(BF16)
                    acc = alpha * acc + jnp.dot(vb, p, preferred_element_type=F32)
                    out.append((m_new, acc))
            return tuple(out)

        acc0 = jnp.zeros((MLA_V + DEN_ROWS, tq), F32)
        carry = tuple(acc0 if bounded else (jnp.full((1, tq), NEG, F32), acc0)
                      for _ in range(heads))
        group = nsub if bounded else 1

        def scores_n(j):
            return tuple(scores(jnp.maximum(j - g, 0), None) for g in range(group))

        ss = scores(base + nsub - 1, nsub - 1)
        for d in range(nsub - 1, -1, -1):
            s_ahead = scores(base + d - 1, d - 1) if d > 0 else scores_n(base - 1)
            carry = update(base + d, ss, carry)
            ss = s_ahead

        def body(jj, state):
            carry, ss = state
            j = base - 1 - group * jj
            s_ahead = scores_n(j - group)
            for g in range(group):
                carry = update(j - g, ss[g], carry)
            return carry, s_ahead

        carry, _ = lax.fori_loop(0, base // group, body, (carry, ss))
        for hh in range(heads):
            acc = carry[hh] if bounded else carry[hh][1]
            o_ref[hh * MLA_V:(hh + 1) * MLA_V, :] = (
                acc[:MLA_V] / acc[MLA_V:MLA_V + 1]).astype(o_ref.dtype)

    safe = bound_ref[0] <= MLA_SAFE_BOUND
    pl.when(safe)(functools.partial(sweep, True))
    pl.when(jnp.logical_not(safe))(functools.partial(sweep, False))


def _mla_attention(bound, q, k, vt):
    b, s, _ = q.shape
    tq = 2 * TKV
    heads = 4
    return pl.pallas_call(
        functools.partial(_mla_kernel, tq=tq, heads=heads),
        out_shape=jax.ShapeDtypeStruct((b, D_MLA, s), BF16),
        grid=(b, MLA_HEADS // heads, s // tq),
        in_specs=[pl.BlockSpec(memory_space=pltpu.SMEM),
                  pl.BlockSpec((None, tq, heads * LANES), lambda bi, p, i: (bi, i, p)),
                  pl.BlockSpec((None, s, heads * LANES), lambda bi, p, i: (bi, 0, p)),
                  pl.BlockSpec((None, s // TKV, heads * MLA_V, TKV),
                               lambda bi, p, i: (bi, 0, p, 0))],
        out_specs=pl.BlockSpec((None, heads * MLA_V, tq), lambda bi, p, i: (bi, p, i)),
        compiler_params=_cparams(("parallel", "parallel", "arbitrary")),
        name="mla_attn",
    )(bound, q, k, vt)


def _mla_score_bound(g_q, g_k):
    b = 1.02 * MLA_QK * (MLA_QK ** -0.5 * LOG2E) * jnp.max(jnp.abs(g_q)) * jnp.max(jnp.abs(g_k))
    return b.astype(F32).reshape(1)


CK_QB = 2 * CHUNK
CK_KEYS = CK_WINDOW + CHUNK
CK_PAD = CK_LEFT_CHUNKS * CHUNK


def _ck_kernel(width_ref, q_ref, k_ref, vt_ref, bias_ref, o_ref, *, n_blocks, heads):
    lo = lax.broadcasted_iota(jnp.int32, (CK_QB, LANES), 1) < HEAD_DIM
    krow = lax.broadcasted_iota(jnp.int32, (CK_KEYS, CK_QB), 0)
    n_slabs = CK_KEYS // CK_QB
    ones = jnp.ones((DEN_ROWS, CK_KEYS), BF16)

    def run(bounded):
        def scores(c2, masked):
            r0 = pl.multiple_of(c2 * CK_QB, CK_QB)
            if masked:
                valid = krow >= CK_PAD - c2 * CK_QB
            out = []
            for pb in range(heads // 2):
                q = q_ref[pl.ds(r0, CK_QB), pb * LANES:(pb + 1) * LANES]
                kw = k_ref[pl.ds(r0, CK_KEYS), pb * LANES:(pb + 1) * LANES]
                zq = jnp.zeros_like(q)
                for hl in range(2):
                    qh = jnp.where(lo, q, zq) if hl == 0 else jnp.where(lo, zq, q)
                    s = (lax.dot_general(kw, qh, NT_DIMS, preferred_element_type=F32)
                         + bias_ref[2 * pb + hl])
                    if masked:
                        s = jnp.where(valid, s, NEG)
                    if bounded:
                        out.append(jnp.exp2(s).astype(BF16))
                    else:
                        out.append((s, jnp.max(s, axis=0, keepdims=True)))
            return tuple(out)

        def finish(c2, ss):
            for hh in range(heads):
                p = ss[hh] if bounded else jnp.exp2(ss[hh][0] - ss[hh][1]).astype(BF16)
                vw = jnp.concatenate(
                    [jnp.concatenate([vt_ref[c2 + sb, hh * HEAD_DIM:(hh + 1) * HEAD_DIM, :]
                                      for sb in range(n_slabs)], axis=1), ones], axis=0)
                acc = jnp.dot(vw, p, preferred_element_type=F32)
                o_ref[c2, hh * HEAD_DIM:(hh + 1) * HEAD_DIM, :] = (
                    acc[:HEAD_DIM] / acc[HEAD_DIM:HEAD_DIM + 1]).astype(o_ref.dtype)

        def sweep(lo_blk, hi_blk, masked):
            def body(c2, ss):
                ahead = scores(jnp.minimum(c2 + 1, hi_blk - 1), masked)
                finish(c2, ss)
                return ahead

            lax.fori_loop(lo_blk, hi_blk, body, scores(lo_blk, masked))

        n_masked = CK_PAD // CK_QB
        sweep(0, n_masked, True)
        sweep(n_masked, n_blocks, False)

    safe = width_ref[0] <= 2.0 * MLA_SAFE_BOUND
    pl.when(safe)(functools.partial(run, True))
    pl.when(jnp.logical_not(safe))(functools.partial(run, False))


def _ck_attention(width, q, kpad, vtpad, bias):
    b, s, d = q.shape
    sp = kpad.shape[1]
    nb = s // CK_QB
    heads = d // HEAD_DIM
    kern = functools.partial(_ck_kernel, n_blocks=nb, heads=heads)
    return pl.pallas_call(
        kern,
        out_shape=jax.ShapeDtypeStruct((b, nb, d, CK_QB), BF16),
        grid=(b,),
        in_specs=[pl.BlockSpec(memory_space=pltpu.SMEM),
                  pl.BlockSpec((None, s, d), lambda bi: (bi, 0, 0)),
                  pl.BlockSpec((None, sp, d), lambda bi: (bi, 0, 0)),
                  pl.BlockSpec((None, sp // CK_QB, d, CK_QB), lambda bi: (bi, 0, 0, 0)),
                  pl.BlockSpec((heads, CK_KEYS, CK_QB), lambda bi: (0, 0, 0))],
        out_specs=pl.BlockSpec((None, nb, d, CK_QB), lambda bi: (bi, 0, 0, 0)),
        compiler_params=_cparams(("parallel",)),
        name="ck_attn",
    )(width, q, kpad, vtpad, bias)


def _outproj_kernel(osb_ref, omla_ref, ock_ref, x_ref, mod_ref, gn_ref, wout_ref, nffn_ref,
                    *rest, moe):
    if moe:
        wr_ref, xo_ref, h2_ref, gates_ref = rest
    else:
        xo_ref, h2_ref = rest
    gn = gn_ref[...]

    def gnorm_t(ot, c0, w):
        ot = ot.astype(F32)
        ms = jnp.sum(ot * ot, axis=0, keepdims=True) * (1.0 / w)
        return ((ot * lax.rsqrt(ms + EPS)).T * gn[:, c0:c0 + w]).astype(BF16)

    ock = jnp.concatenate([ock_ref[sb] for sb in range(ock_ref.shape[0])], axis=1)
    merged = jnp.concatenate([gnorm_t(osb_ref[...], 0, D_SB), gnorm_t(omla_ref[...], D_SB, D_MLA),
                              gnorm_t(ock, D_SB + D_MLA, D_CK)], axis=-1)
    y = jnp.dot(merged, wout_ref[...], preferred_element_type=F32)
    xn = x_ref[...] + mod_ref[2:3, :] * y
    xo_ref[...] = xn
    h2 = _rms(xn, nffn_ref[...], D_MODEL) * (1.0 + mod_ref[4:5, :]) + mod_ref[3:4, :]
    h2_ref[...] = h2.astype(h2_ref.dtype)
    if moe:
        wr = wr_ref[...]
        w_hi = wr.astype(BF16)
        w_lo = (wr - w_hi.astype(F32)).astype(BF16)
        h_hi = h2.astype(BF16)
        h_lo = (h2 - h_hi.astype(F32)).astype(BF16)
        logits = (jnp.dot(h_hi, w_hi, preferred_element_type=F32)
                  + (jnp.dot(h_hi, w_lo, preferred_element_type=F32)
                     + jnp.dot(h_lo, w_hi, preferred_element_type=F32)))
        lane = lax.broadcasted_iota(jnp.int32, logits.shape, 1).astype(F32)
        logits = jnp.where(lane < N_EXPERTS, logits, -jnp.inf)
        m1 = jnp.max(logits, axis=-1, keepdims=True)
        i1 = jnp.min(jnp.where(logits == m1, lane, float(LANES)), axis=-1, keepdims=True)
        rest_l = jnp.where(lane == i1, -jnp.inf, logits)
        m2 = jnp.max(rest_l, axis=-1, keepdims=True)
        i2 = jnp.min(jnp.where(rest_l == m2, lane, float(LANES)), axis=-1, keepdims=True)
        e2 = jnp.exp(m2 - m1)
        w1 = 1.0 / (1.0 + e2)
        w2 = e2 / (1.0 + e2)
        gates_ref[...] = jnp.where(lane == 0.0, i1, jnp.where(lane == 1.0, i2,
                                   jnp.where(lane == 2.0, w1, jnp.where(lane == 3.0, w2, 0.0))))


def _outproj(osb, omla, ock, x2, mod_l, gn, wout, nffn, wr, seq):
    t = x2.shape[0]
    tm = 512
    tpb = seq // tm
    moe = wr is not None
    full = lambda a: pl.BlockSpec(a.shape, lambda i: (0,) * a.ndim)
    row = lambda w: pl.BlockSpec((tm, w), lambda i: (i, 0))
    colt = lambda w: pl.BlockSpec((None, w, tm), lambda i: (i // tpb, 0, i % tpb))
    ins = [osb, omla, ock, x2, mod_l, gn, wout, nffn]
    slabs = pl.BlockSpec((None, tm // CK_QB, D_CK, CK_QB), lambda i: (i // tpb, i % tpb, 0, 0))
    in_specs = [colt(D_SB), colt(D_MLA), slabs, row(D_MODEL),
                pl.BlockSpec((None, 6, D_MODEL), lambda i: (i // tpb, 0, 0)),
                full(gn), full(wout), full(nffn)]
    out_shape = [jax.ShapeDtypeStruct((t, D_MODEL), F32),
                 jax.ShapeDtypeStruct((t, D_MODEL), F32 if moe else BF16)]
    out_specs = [row(D_MODEL), row(D_MODEL)]
    if moe:
        ins.append(wr)
        in_specs.append(full(wr))
        out_shape.append(jax.ShapeDtypeStruct((t, LANES), F32))
        out_specs.append(row(LANES))
    return pl.pallas_call(
        functools.partial(_outproj_kernel, moe=moe),
        out_shape=tuple(out_shape),
        grid=(t // tm,),
        in_specs=in_specs,
        out_specs=tuple(out_specs),
        compiler_params=_cparams(("parallel",)),
        name="outproj_moe" if moe else "outproj",
    )(*ins)


FFN_TM = 512
MXU_DEPTH = 256
FFN_CHUNKS = ((0, 5 * MXU_DEPTH), (5 * MXU_DEPTH, 6 * MXU_DEPTH))
RESIDENT = pl.Buffered(1)


def _swiglu(h, wg_ref, wu_ref, wd_ref):
    y = None
    for c0, w in FFN_CHUNKS:
        g = jnp.dot(h, wg_ref[:, c0:c0 + w], preferred_element_type=F32)
        u = jnp.dot(h, wu_ref[:, c0:c0 + w], preferred_element_type=F32)
        a = (g * jax.nn.sigmoid(g) * u).astype(BF16)
        part = jnp.dot(a, wd_ref[c0:c0 + w, :], preferred_element_type=F32)
        y = part if y is None else y + part
    return y


def _ffn_kernel(h_ref, x_ref, mod_ref, wg_ref, wu_ref, wd_ref, o_ref):
    o_ref[...] = x_ref[...] + mod_ref[5:6, :] * _swiglu(h_ref[...], wg_ref, wu_ref, wd_ref)


def _ffn(h2, x2, mod_l, wg, wu, wd, seq):
    t = h2.shape[0]
    tm = FFN_TM
    tpb = seq // tm
    rmap = lambda i: (i, 0)
    const = lambda a: pl.BlockSpec(a.shape, lambda i: (0, 0), pipeline_mode=RESIDENT)
    return pl.pallas_call(
        _ffn_kernel,
        out_shape=jax.ShapeDtypeStruct((t, D_MODEL), F32),
        grid=(t // tm,),
        in_specs=[pl.BlockSpec((tm, D_MODEL), rmap), pl.BlockSpec((tm, D_MODEL), rmap),
                  pl.BlockSpec((None, 6, D_MODEL), lambda i: (i // tpb, 0, 0)),
                  const(wg), const(wu), const(wd)],
        out_specs=pl.BlockSpec((tm, D_MODEL), rmap),
        compiler_params=_cparams(("parallel",)),
        name="dense_ffn",
    )(h2, x2, mod_l, wg, wu, wd)


ROUTE_TM = 512
MOVE_TM = 1024


def _rank_kernel(route_ref, rank_ref, count_ref, base_ref):
    i = pl.program_id(0)

    @pl.when(i == 0)
    def _():
        base_ref[...] = jnp.zeros_like(base_ref)

    tm = route_ref.shape[0]
    route = route_ref[...]
    lane = lax.broadcasted_iota(jnp.int32, (tm, LANES), 1).astype(F32)
    sel1 = lane == route[:, 0:1]
    sel2 = lane == route[:, 1:2]
    hot = jnp.where(sel1 | sel2, 1.0, 0.0)
    r = lax.broadcasted_iota(jnp.int32, (tm, tm), 0)
    c = lax.broadcasted_iota(jnp.int32, (tm, tm), 1)
    before = jnp.where(c < r, 1.0, 0.0).astype(BF16)
    seen = jnp.dot(before, hot.astype(BF16), preferred_element_type=F32) + base_ref[...]
    r1 = jnp.sum(jnp.where(sel1, seen, 0.0), axis=-1, keepdims=True)
    r2 = jnp.sum(jnp.where(sel2, seen, 0.0), axis=-1, keepdims=True)
    rank_ref[...] = jnp.where(lane == 0.0, r1, jnp.where(lane == 1.0, r2, 0.0))
    base_ref[...] += jnp.sum(hot, axis=0, keepdims=True)
    count_ref[...] = base_ref[...]


def _moe_rank(route):
    t = route.shape[0]
    tm = ROUTE_TM
    return pl.pallas_call(
        _rank_kernel,
        out_shape=(jax.ShapeDtypeStruct((t, LANES), F32), jax.ShapeDtypeStruct((1, LANES), F32)),
        grid=(t // tm,),
        in_specs=[pl.BlockSpec((tm, LANES), lambda i: (i, 0))],
        out_specs=(pl.BlockSpec((tm, LANES), lambda i: (i, 0)),
                   pl.BlockSpec((1, LANES), lambda i: (0, 0))),
        scratch_shapes=[pltpu.VMEM((1, LANES), F32)],
        compiler_params=_cparams(("arbitrary",)),
        name="moe_rank",
    )(route)


def _dispatch_kernel(pos_ref, h_ref, xs_in_ref, xs_ref, sem):
    del xs_in_ref
    td = pos_ref.shape[1]

    def issue(r, carry):
        src = h_ref.at[pl.ds(r, 1), :]
        for k in range(2):
            pltpu.make_async_copy(src, xs_ref.at[pl.ds(pos_ref[k, r], 1), :],
                                  sem.at[k]).start(priority=k)
        return carry

    lax.fori_loop(0, td, issue, 0, unroll=8)
    for k in range(2):
        pltpu.make_async_copy(h_ref, xs_ref.at[pl.ds(0, td), :], sem.at[k]).wait()


def _moe_dispatch(pos, h2, n_rows):
    nt, _, td = pos.shape
    xs0 = jnp.zeros((n_rows, D_MODEL), F32)
    return pl.pallas_call(
        _dispatch_kernel,
        out_shape=jax.ShapeDtypeStruct((n_rows, D_MODEL), F32),
        grid=(nt,),
        in_specs=[pl.BlockSpec((None, 2, td), lambda i: (i, 0, 0), memory_space=pltpu.SMEM),
                  pl.BlockSpec((td, D_MODEL), lambda i: (i, 0)),
                  pl.BlockSpec(memory_space=pl.ANY)],
        out_specs=pl.BlockSpec(memory_space=pl.ANY),
        scratch_shapes=[pltpu.SemaphoreType.DMA((2,))],
        input_output_aliases={2: 0},
        compiler_params=_cparams(("arbitrary",)),
        name="moe_dispatch",
    )(pos, h2, xs0)


def _group_ffn_kernel(te_ref, nu_ref, xs_ref, wg_ref, wu_ref, wd_ref, y_ref):
    del te_ref
    i = pl.program_id(0)

    @pl.when(i < nu_ref[0])
    def _():
        y_ref[...] = _swiglu(xs_ref[...].astype(BF16), wg_ref, wu_ref, wd_ref)

    @pl.when(i >= nu_ref[0])
    def _():
        y_ref[...] = jnp.zeros_like(y_ref)


def _moe_group_ffn(tile_expert, n_used, xs, wg, wu, wd):
    n_rows = xs.shape[0]
    tm = FFN_TM
    row = lambda i, te, nu: (jnp.minimum(i, nu[0] - 1), 0)
    wspec = lambda a: pl.BlockSpec((None,) + a.shape[1:], lambda i, te, nu: (te[i], 0, 0))
    return pl.pallas_call(
        _group_ffn_kernel,
        out_shape=jax.ShapeDtypeStruct((n_rows, D_MODEL), F32),
        grid_spec=pltpu.PrefetchScalarGridSpec(
            num_scalar_prefetch=2,
            grid=(n_rows // tm,),
            in_specs=[pl.BlockSpec((tm, D_MODEL), row), wspec(wg), wspec(wu), wspec(wd)],
            out_specs=pl.BlockSpec((tm, D_MODEL), lambda i, te, nu: (i, 0))),
        compiler_params=_cparams(("arbitrary",)),
        name="moe_group_ffn",
    )(tile_expert, n_used, xs, wg, wu, wd)


def _combine_kernel(pos_ref, route_ref, x_ref, mod_ref, y_ref, o_ref, buf_ref, sem):
    tc = pos_ref.shape[1]

    def issue(r, carry):
        for k in range(2):
            pltpu.make_async_copy(y_ref.at[pl.ds(pos_ref[k, r], 1), :],
                                  buf_ref.at[k, pl.ds(r, 1), :], sem.at[k]).start(priority=k)
        return carry

    lax.fori_loop(0, tc, issue, 0, unroll=8)
    for k in range(2):
        pltpu.make_async_copy(y_ref.at[pl.ds(0, tc), :], buf_ref.at[k], sem.at[k]).wait()
    route = route_ref[...]
    y = route[:, 2:3] * buf_ref[0] + route[:, 3:4] * buf_ref[1]
    o_ref[...] = x_ref[...] + mod_ref[5:6, :] * y


def _moe_combine(pos, route, x2, mod_l, y, seq):
    nt, _, tc = pos.shape
    t = x2.shape[0]
    tpb = seq // tc
    return pl.pallas_call(
        _combine_kernel,
        out_shape=jax.ShapeDtypeStruct((t, D_MODEL), F32),
        grid=(nt,),
        in_specs=[pl.BlockSpec((None, 2, tc), lambda i: (i, 0, 0), memory_space=pltpu.SMEM),
                  pl.BlockSpec((tc, LANES), lambda i: (i, 0)),
                  pl.BlockSpec((tc, D_MODEL), lambda i: (i, 0)),
                  pl.BlockSpec((None, 6, D_MODEL), lambda i: (i // tpb, 0, 0)),
                  pl.BlockSpec(memory_space=pl.ANY)],
        out_specs=pl.BlockSpec((tc, D_MODEL), lambda i: (i, 0)),
        scratch_shapes=[pltpu.VMEM((2, tc, D_MODEL), F32), pltpu.SemaphoreType.DMA((2,))],
        compiler_params=_cparams(("arbitrary",)),
        name="moe_combine",
    )(pos, route, x2, mod_l, y)


def _moe(h2, x2, mod_l, route, wg, wu, wd, seq):
    t = h2.shape[0]
    ne = wg.shape[0]
    tm = FFN_TM
    n_tiles = (2 * t) // tm + ne
    n_rows = n_tiles * tm
    rank, count = _moe_rank(route)
    counts = count[0, :ne].astype(jnp.int32)
    tiles_per = (counts + tm - 1) // tm
    tile_end = jnp.cumsum(tiles_per)
    start = (tile_end - tiles_per) * tm
    n_used = tile_end[-1:]
    tile_ids = jnp.minimum(jnp.arange(n_tiles, dtype=jnp.int32), n_used[0] - 1)
    tile_expert = jnp.sum((tile_ids[:, None] >= tile_end[None, :]).astype(jnp.int32), axis=1)
    experts = route[:, 0:2].astype(jnp.int32)
    pos = jnp.take(start, experts) + rank[:, 0:2].astype(jnp.int32)
    pos = pos.reshape(t // MOVE_TM, MOVE_TM, 2).transpose(0, 2, 1)
    xs = _moe_dispatch(pos, h2, n_rows)
    y = _moe_group_ffn(tile_expert, n_used, xs, wg, wu, wd)
    return _moe_combine(pos, route, x2, mod_l, y, seq)


def _rot_half_cols(w):
    half = w.shape[-1] // 2
    return jnp.concatenate([-w[..., half:], w[..., :half]], axis=-1)


def _pad_cols(w, left, total):
    return jnp.pad(w, ((0, 0), (left, total - left - w.shape[-1])))


def _layout_w_in(w_in):
    sizes = (D_SB, D_SB, D_SB, MLA_Q_RANK, MLA_KV_RANK, MLA_ROPE, D_CK, D_CK, D_CK)
    splits = [int(v) for v in np.cumsum(sizes)[:-1]]
    sbq, sbk, sbv, cq, ckv, kpe, ckq, ckk, ckv2 = jnp.split(w_in, splits, axis=-1)
    kpa = _pad_cols(kpe, MLA_NOPE, LANES)
    kpb = _pad_cols(_rot_half_cols(kpe), MLA_NOPE, LANES)
    main = jnp.concatenate([sbq, sbk, cq, ckv, kpa, kpb, ckq, ckk], axis=-1)
    return main.astype(BF16), jnp.concatenate([sbv, ckv2], axis=-1).T.astype(BF16)


def _layout_w_q_up(w):
    r = w.shape[0]
    w3 = w.reshape(r, MLA_HEADS, MLA_QK)
    nope, pe = w3[..., :MLA_NOPE], w3[..., MLA_NOPE:]
    zpad = jnp.zeros((r, MLA_HEADS, LANES - MLA_QK), w.dtype)
    a = jnp.concatenate([nope, pe, zpad], axis=-1).reshape(r, MLA_HEADS * LANES)
    b = jnp.concatenate([jnp.zeros_like(nope), _rot_half_cols(pe), zpad], axis=-1)
    return jnp.concatenate([a, b.reshape(r, MLA_HEADS * LANES)], axis=-1).astype(BF16)


def _layout_w_kv_up(w):
    r = w.shape[0]
    w3 = w.reshape(r, MLA_HEADS, MLA_NOPE + MLA_V)
    kn = jnp.pad(w3[..., :MLA_NOPE], ((0, 0), (0, 0), (0, LANES - MLA_NOPE)))
    return (kn.reshape(r, MLA_HEADS * LANES).astype(BF16),
            w3[..., MLA_NOPE:].reshape(r, D_MLA).T.astype(BF16))


def _ck_bias_table(rel_bias, g_q, g_k):
    qk = (1.02 * HEAD_DIM * (HEAD_DIM ** -0.5 * LOG2E)
          * jnp.max(jnp.abs(g_q)) * jnp.max(jnp.abs(g_k)))
    b_hi = jnp.max(rel_bias) * LOG2E
    b_lo = jnp.min(rel_bias) * LOG2E
    width = (2.0 * qk + (b_hi - b_lo)).astype(F32).reshape(1)
    rel_bias = rel_bias.astype(F32) - (qk + b_hi) / LOG2E
    heads = rel_bias.shape[0]
    n_g = CHUNK + CK_KEYS - 1
    blocks = []
    for u in range(CK_QB // CHUNK):
        d = np.arange(n_g)
        idx = np.clip(d - (CK_KEYS - 1) + CK_PAD + CHUNK * u, -REL_CLIP, REL_CLIP) + REL_CLIP
        g = rel_bias.astype(F32)[:, idx] * LOG2E
        hank = jnp.tile(g, (1, CHUNK + 1))[:, :CHUNK * (n_g + 1)]
        hank = hank.reshape(heads, CHUNK, n_g + 1)[:, :, :CK_KEYS]
        blk = jnp.transpose(hank[:, :, ::-1], (0, 2, 1))
        j = np.arange(CK_KEYS) - CHUNK * u
        in_band = ((j >= 0) & (j < CK_WINDOW))[None, :, None]
        blocks.append(jnp.where(in_band, blk, NEG))
    return jnp.concatenate(blocks, axis=-1), width


def kernel(x, c, positions, ada_w, ada_b, norm_mix, norm_ffn, w_in, mla_q_norm, w_q_up, mla_kv_norm, w_kv_up, mla_q_qknorm, mla_k_qknorm, ck_q_qknorm, ck_k_qknorm, ck_rel_bias, group_out_norm, w_out, ffn_w_gate, ffn_w_up, ffn_w_down, moe_router, moe_w_gate, moe_w_up, moe_w_down):
    bsz, seq, d = x.shape
    depth = ada_w.shape[0]
    t = bsz * seq
    x2 = x.reshape(t, d)

    mod = _ada(c, ada_w, ada_b).reshape(depth, bsz, 6, d)
    rc, rs = _rope_tables(positions)
    pad128 = lambda g: jnp.pad(g, (0, LANES - g.shape[0])).reshape(1, LANES)
    pair = lambda g: jnp.concatenate([g, g]).reshape(1, LANES)

    for layer in range(depth):
        win, wsbvt = _layout_w_in(w_in[layer])
        wq = _layout_w_q_up(w_q_up[layer])
        wk, wvt = _layout_w_kv_up(w_kv_up[layer])
        sbq, sbk, sbvt, mq, mk, mvt, cq, ck, cvt = _inproj(
            x2, mod[layer], norm_mix[layer].reshape(1, d), win, wsbvt, rc, rs,
            mla_q_norm[layer].reshape(1, -1), wq, mla_kv_norm[layer].reshape(1, -1), wk, wvt,
            pad128(mla_q_qknorm[layer]), pad128(mla_k_qknorm[layer]),
            pair(ck_q_qknorm[layer]), pair(ck_k_qknorm[layer]), seq)

        r3 = lambda a: a.reshape(bsz, seq, a.shape[-1])
        o_sb = _sb_attention(r3(sbq), r3(sbk), sbvt)
        o_mla = _mla_attention(_mla_score_bound(mla_q_qknorm[layer], mla_k_qknorm[layer]),
                               r3(mq), r3(mk), mvt)
        kpad = jnp.pad(r3(ck), ((0, 0), (CK_PAD, 0), (0, 0)))
        vtpad = jnp.pad(cvt, ((0, 0), (CK_PAD // CK_QB, 0), (0, 0), (0, 0)))
        ck_bias, ck_width = _ck_bias_table(ck_rel_bias[layer], ck_q_qknorm[layer],
                                           ck_k_qknorm[layer])
        o_ck = _ck_attention(ck_width, r3(cq), kpad, vtpad, ck_bias)

        i = layer // 2
        is_moe = layer % 2 == 1
        wr = None
        if is_moe:
            wr = jnp.pad(moe_router[i], ((0, 0), (0, LANES - N_EXPERTS)))
        res = _outproj(o_sb, o_mla, o_ck, x2,
                       mod[layer], group_out_norm[layer].reshape(1, -1),
                       w_out[layer].astype(BF16), norm_ffn[layer].reshape(1, d), wr, seq)
        if is_moe:
            x2, h2, route = res
            x2 = _moe(h2, x2, mod[layer], route, moe_w_gate[i].astype(BF16),
                      moe_w_up[i].astype(BF16), moe_w_down[i].astype(BF16), seq)
        else:
            x2, h2 = res
            x2 = _ffn(h2, x2, mod[layer], ffn_w_gate[i].astype(BF16),
                      ffn_w_up[i].astype(BF16), ffn_w_down[i].astype(BF16), seq)
    return x2.reshape(bsz, seq, d)
```

```python
import functools

import numpy as np
import jax
import jax.numpy as jnp
from jax import lax
from jax.experimental import pallas as pl
from jax.experimental.pallas import tpu as pltpu

F32 = jnp.float32
BF16 = jnp.bfloat16

D_MODEL = 1024
CHUNK = 64
HEAD_DIM = 64
SB_HEADS = 4
MLA_HEADS = 8
MLA_Q_RANK = 256
MLA_KV_RANK = 128
MLA_NOPE = 64
MLA_ROPE = 32
MLA_V = 64
MLA_QK = MLA_NOPE + MLA_ROPE
ROPE_THETA = 10000.0
CK_HEADS = 4
CK_LEFT_CHUNKS = 8
CK_WINDOW = (CK_LEFT_CHUNKS + 1) * CHUNK
REL_CLIP = 128
D_SB = SB_HEADS * HEAD_DIM
D_MLA = MLA_HEADS * MLA_V
D_CK = CK_HEADS * HEAD_DIM
D_FF = 2816
N_EXPERTS = 8
EPS = 1e-6
NEG = -1e30

LANES = 128
VMEM_LIMIT = 56 * 1024 * 1024

C_SBQ, C_SBK = 0, 256
C_CQ = 512
C_CKV = 768
C_KPA = 896
C_KPB = 1024
C_CKQ, C_CKK = 1152, 1408
IN_EXT = 1664

NT_DIMS = (((1,), (1,)), ((), ()))
TN_DIMS = (((0,), (0,)), ((), ()))
TKV = 256
LOG2E = 1.4426950408889634


def _cparams(sem, vmem=VMEM_LIMIT):
    return pltpu.CompilerParams(dimension_semantics=sem, vmem_limit_bytes=vmem)


def _ada_kernel(c_ref, w_ref, b_ref, o_ref):
    c = c_ref[...]
    ca = c * jax.nn.sigmoid(c)
    o_ref[...] = jnp.dot(ca, w_ref[...], preferred_element_type=F32,
                         precision=lax.Precision.HIGHEST) + b_ref[...]


def _ada(c, ada_w, ada_b):
    depth, d, n = ada_w.shape
    b = c.shape[0]
    tn = 1536
    return pl.pallas_call(
        _ada_kernel,
        out_shape=jax.ShapeDtypeStruct((depth, b, n), F32),
        grid=(depth, n // tn),
        in_specs=[pl.BlockSpec((b, d), lambda l, j: (0, 0)),
                  pl.BlockSpec((None, d, tn), lambda l, j: (l, 0, j)),
                  pl.BlockSpec((None, 1, tn), lambda l, j: (l, 0, j))],
        out_specs=pl.BlockSpec((None, b, tn), lambda l, j: (l, 0, j)),
        compiler_params=_cparams(("parallel", "parallel")),
        name="ada_mod",
    )(c, ada_w, ada_b.reshape(depth, 1, n))


def _rope_kernel(pos_ref, invf_ref, c_ref, s_ref):
    ang = pos_ref[...] * invf_ref[...]
    lane = lax.broadcasted_iota(jnp.int32, ang.shape, 1)
    rope = (lane >= MLA_NOPE) & (lane < MLA_QK)
    c_ref[...] = jnp.where(rope, jnp.cos(ang), jnp.where(lane < MLA_NOPE, 1.0, 0.0))
    s_ref[...] = jnp.where(rope, jnp.sin(ang), 0.0)


def _rope_tables(positions):
    t = positions.size
    tm = 1024
    inv_freq = ROPE_THETA ** (-jnp.arange(0, MLA_ROPE, 2, dtype=F32) / MLA_ROPE)
    invf = jnp.zeros((1, LANES), F32)
    invf = invf.at[0, MLA_NOPE:MLA_NOPE + 16].set(inv_freq)
    invf = invf.at[0, MLA_NOPE + 16:MLA_QK].set(inv_freq)
    pos_b = jnp.broadcast_to(positions.reshape(t, 1).astype(F32), (t, LANES))
    return pl.pallas_call(
        _rope_kernel,
        out_shape=(jax.ShapeDtypeStruct((t, LANES), F32),
                   jax.ShapeDtypeStruct((t, LANES), F32)),
        grid=(t // tm,),
        in_specs=[pl.BlockSpec((tm, LANES), lambda i: (i, 0)),
                  pl.BlockSpec((1, LANES), lambda i: (0, 0))],
        out_specs=(pl.BlockSpec((tm, LANES), lambda i: (i, 0)),
                   pl.BlockSpec((tm, LANES), lambda i: (i, 0))),
        compiler_params=_cparams(("parallel",)),
        name="rope_tables",
    )(pos_b, invf)


def _rms(x, g, n):
    ms = jnp.sum(x * x, axis=-1, keepdims=True) * (1.0 / n)
    return x * lax.rsqrt(ms + EPS) * g


def _pair_rms(blk, g, scale):
    lane = lax.broadcasted_iota(jnp.int32, blk.shape, 1)
    lo = lane < HEAD_DIM
    sq = blk * blk
    s_all = jnp.sum(sq, axis=-1, keepdims=True)
    s_lo = jnp.sum(jnp.where(lo, sq, 0.0), axis=-1, keepdims=True)
    ms = jnp.where(lo, s_lo, s_all - s_lo) * (1.0 / HEAD_DIM)
    return blk * lax.rsqrt(ms + EPS) * (g * scale)


def _store_kv_blocks(ref, xt):
    slab = ref.shape[-1]
    for cblk in range(xt.shape[1] // slab):
        ref[cblk] = xt[:, cblk * slab:(cblk + 1) * slab].astype(ref.dtype)


def _inproj_kernel(x_ref, mod_ref, nw_ref, win_ref, wsbvt_ref, rc_ref, rs_ref, qn_ref, wq_ref,
                   kvn_ref, wk_ref, wvt_ref, gq_ref, gk_ref, gcq_ref, gck_ref,
                   sbq_ref, sbk_ref, sbvt_ref, mq_ref, mk_ref, mvt_ref,
                   cq_ref, ck_ref, cvt_ref):
    x = x_ref[...]
    xn = _rms(x, nw_ref[...], D_MODEL)
    h = (xn * (1.0 + mod_ref[1:2, :]) + mod_ref[0:1, :]).astype(BF16)
    proj = jnp.dot(h, win_ref[...], preferred_element_type=F32)

    sbq_ref[...] = (proj[:, C_SBQ:C_SBQ + D_SB] * (HEAD_DIM ** -0.5 * LOG2E)).astype(BF16)
    sbk_ref[...] = proj[:, C_SBK:C_SBK + D_SB].astype(BF16)
    vt = lax.dot_general(wsbvt_ref[...], h, NT_DIMS, preferred_element_type=F32)
    _store_kv_blocks(sbvt_ref, vt[:D_SB])
    _store_kv_blocks(cvt_ref, vt[D_SB:])

    rc = rc_ref[...]
    rs = rs_ref[...]
    cqn = _rms(proj[:, C_CQ:C_CQ + MLA_Q_RANK], qn_ref[...], MLA_Q_RANK).astype(BF16)
    qq = jnp.dot(cqn, wq_ref[...], preferred_element_type=F32)
    gq = gq_ref[...] * (MLA_QK ** -0.5 * LOG2E)
    nq = MLA_HEADS * LANES
    for hh in range(MLA_HEADS):
        a = qq[:, hh * LANES:(hh + 1) * LANES]
        b = qq[:, nq + hh * LANES:nq + (hh + 1) * LANES]
        qh = a * rc + b * rs
        mq_ref[:, hh * LANES:(hh + 1) * LANES] = _rms(qh, gq, MLA_QK).astype(BF16)

    ckvn = _rms(proj[:, C_CKV:C_CKV + MLA_KV_RANK], kvn_ref[...], MLA_KV_RANK).astype(BF16)
    kn = jnp.dot(ckvn, wk_ref[...], preferred_element_type=F32)
    _store_kv_blocks(mvt_ref, lax.dot_general(wvt_ref[...], ckvn, NT_DIMS,
                                              preferred_element_type=F32))
    kpe = proj[:, C_KPA:C_KPA + LANES] * rc + proj[:, C_KPB:C_KPB + LANES] * rs
    gk = gk_ref[...]
    for hh in range(MLA_HEADS):
        kh = kn[:, hh * LANES:(hh + 1) * LANES] + kpe
        mk_ref[:, hh * LANES:(hh + 1) * LANES] = _rms(kh, gk, MLA_QK).astype(BF16)

    gcq = gcq_ref[...]
    gck = gck_ref[...]
    for p in range(D_CK // LANES):
        qb = proj[:, C_CKQ + p * LANES:C_CKQ + (p + 1) * LANES]
        kb = proj[:, C_CKK + p * LANES:C_CKK + (p + 1) * LANES]
        cq_ref[:, p * LANES:(p + 1) * LANES] = _pair_rms(
            qb, gcq, HEAD_DIM ** -0.5 * LOG2E).astype(BF16)
        ck_ref[:, p * LANES:(p + 1) * LANES] = _pair_rms(kb, gck, 1.0).astype(BF16)


def _inproj(x2, mod_l, nw, win, wsbvt, rc, rs, qn, wq, kvn, wk, wvt, gq, gk, gcq, gck, seq):
    t = x2.shape[0]
    tm = 512
    tpb = seq // tm
    bsz = t // seq
    full = lambda a: pl.BlockSpec(a.shape, lambda i: (0,) * a.ndim)
    row = lambda w: pl.BlockSpec((tm, w), lambda i: (i, 0))
    rows = lambda w: (jax.ShapeDtypeStruct((t, w), BF16), row(w))
    kvt = lambda w, slab=TKV: (
        jax.ShapeDtypeStruct((bsz, seq // slab, w, slab), BF16),
        pl.BlockSpec((None, tm // slab, w, slab), lambda i: (i // tpb, i % tpb, 0, 0)))
    outs = [rows(D_SB), rows(D_SB), kvt(D_SB), rows(MLA_HEADS * LANES), rows(MLA_HEADS * LANES),
            kvt(D_MLA), rows(D_CK), rows(D_CK), kvt(D_CK, CK_QB)]
    return pl.pallas_call(
        _inproj_kernel,
        out_shape=tuple(o[0] for o in outs),
        grid=(t // tm,),
        in_specs=[row(D_MODEL),
                  pl.BlockSpec((None, 6, D_MODEL), lambda i: (i // tpb, 0, 0)),
                  full(nw), full(win), full(wsbvt), row(LANES), row(LANES), full(qn), full(wq),
                  full(kvn), full(wk), full(wvt), full(gq), full(gk), full(gcq), full(gck)],
        out_specs=tuple(o[1] for o in outs),
        compiler_params=_cparams(("parallel",)),
        name="inproj",
    )(x2, mod_l, nw, win, wsbvt, rc, rs, qn, wq, kvn, wk, wvt, gq, gk, gcq, gck)


def _kv_iotas(tk, tq):
    return (lax.broadcasted_iota(jnp.int32, (tk, tq), 0),
            lax.broadcasted_iota(jnp.int32, (tk, tq), 1))


SB_DEAD = 160.0


def _sb_kernel(q_ref, k_ref, vt_ref, o_ref, *, tq, heads):
    tk = TKV
    i = pl.program_id(2)
    lo = lax.broadcasted_iota(jnp.int32, (tq, LANES), 1) < HEAD_DIM
    qs = []
    for pb in range(heads // 2):
        q = q_ref[:, pb * LANES:(pb + 1) * LANES]
        zq = jnp.zeros_like(q)
        qs += [jnp.where(lo, q, zq), jnp.where(lo, zq, q)]
    krow, qcol = _kv_iotas(tk, tq)
    strict = krow < qcol
    ur, uc = _kv_iotas(tk, tk)
    later = jnp.where(uc > ur, 1.0, 0.0).astype(BF16)

    sign = jnp.uint32(0x80000000)

    def stage1(j, diag):
        r0 = pl.multiple_of(j * tk, tk)
        out = []
        for hh in range(heads):
            pb = hh // 2
            kb = k_ref[pl.ds(r0, tk), pb * LANES:(pb + 1) * LANES]
            y = lax.dot_general(kb, qs[hh], NT_DIMS, preferred_element_type=F32)
            neg_abs = pltpu.bitcast(pltpu.bitcast(y, jnp.uint32) | sign, F32)
            sp = jnp.maximum(y, 0.0) + jnp.log2(1.0 + jnp.exp2(neg_abs))
            spm = jnp.where(strict, sp, 0.0) if diag else sp
            tail = jnp.dot(later, spm.astype(BF16), preferred_element_type=F32)
            pre = y - sp - tail
            if diag:
                pre = jnp.where(strict, pre, NEG)
            out.append((jnp.exp2(pre).astype(BF16), tail[0:1, :] + spm[0:1, :]))
        return tuple(out)

    def stage2(j, pres, carry):
        out = []
        for hh in range(heads):
            acc, car = carry[hh]
            e, inc = pres[hh]
            vb = vt_ref[j, hh * HEAD_DIM:(hh + 1) * HEAD_DIM, :]
            acc = acc + jnp.exp2(-car) * jnp.dot(vb, e, preferred_element_type=F32)
            out.append((acc, car + inc))
        return tuple(out)

    init = tuple((jnp.zeros((HEAD_DIM, tq), F32), jnp.zeros((1, tq), F32))
                 for _ in range(heads))

    def live(carry, pres):
        mass = functools.reduce(jnp.minimum, [carry[hh][1] + pres[hh][1] for hh in range(heads)])
        return (jnp.min(mass) < SB_DEAD).astype(jnp.int32)

    def cond(state):
        jj, _, _, more = state
        return (jj < i) & (more > 0)

    def body(state):
        jj, carry, pres, _ = state
        j = i - jj
        ahead = stage1(j - 1, False)
        carry = stage2(j, pres, carry)
        return jj + 1, carry, ahead, live(carry, ahead)

    first = stage1(i, True)
    jj, carry, pres, _ = lax.while_loop(cond, body, (jnp.int32(0), init, first, live(init, first)))
    carry = stage2(i - jj, pres, carry)
    for hh in range(heads):
        o_ref[hh * HEAD_DIM:(hh + 1) * HEAD_DIM, :] = carry[hh][0].astype(o_ref.dtype)


def _sb_attention(q, k, vt):
    b, s, d = q.shape
    tq = TKV
    heads = 4
    w = heads * HEAD_DIM
    return pl.pallas_call(
        functools.partial(_sb_kernel, tq=tq, heads=heads),
        out_shape=jax.ShapeDtypeStruct((b, d, s), BF16),
        grid=(b, d // w, s // tq),
        in_specs=[pl.BlockSpec((None, tq, w), lambda bi, p, i: (bi, i, p)),
                  pl.BlockSpec((None, s, w), lambda bi, p, i: (bi, 0, p)),
                  pl.BlockSpec((None, s // TKV, w, TKV), lambda bi, p, i: (bi, 0, p, 0))],
        out_specs=pl.BlockSpec((None, w, tq), lambda bi, p, i: (bi, p, i)),
        compiler_params=_cparams(("parallel", "parallel", "arbitrary")),
        name="sb_attn",
    )(q, k, vt)


DEN_ROWS = 16


MLA_SAFE_BOUND = 60.0


def _mla_kernel(bound_ref, q_ref, k_ref, vt_ref, o_ref, *, tq, heads):
    tk = TKV
    i = pl.program_id(2)
    nsub = tq // tk
    krow, qcol = _kv_iotas(tk, tq)
    shift = CHUNK.bit_length() - 1
    ones = jnp.ones((DEN_ROWS, tk), BF16)
    base = i * nsub

    def sweep(bounded):
        qs = tuple(q_ref[:, hh * LANES:(hh + 1) * LANES] for hh in range(heads))
        ref = bound_ref[0]

        def scores(j, diag):
            r0 = pl.multiple_of(j * tk, tk)
            if diag is not None:
                allowed = (jnp.right_shift(krow + diag * tk, shift)
                           <= jnp.right_shift(qcol, shift))
            out = []
            for hh in range(heads):
                s = lax.dot_general(k_ref[pl.ds(r0, tk), hh * LANES:(hh + 1) * LANES], qs[hh],
                                    NT_DIMS, preferred_element_type=F32)
                if diag is not None:
                    s = jnp.where(allowed, s, NEG)
                if bounded:
                    out.append(jnp.exp2(s - ref).astype(BF16))
                else:
                    out.append((s, jnp.max(s, axis=0, keepdims=True)))
            return tuple(out)

        def update(j, ss, carry):
            out = []
            for hh in range(heads):
                vb = jnp.concatenate([vt_ref[j, hh * MLA_V:(hh + 1) * MLA_V, :], ones], axis=0)
                if bounded:
                    out.append(carry[hh] + jnp.dot(vb, ss[hh], preferred_element_type=F32))
                else:
                    m, acc = carry[hh]
                    s, smax = ss[hh]
                    m_new = jnp.maximum(m, smax)
                    alpha = jnp.exp2(m - m_new)
                    p = jnp.exp2(s - m_new).astype(BF16)
                    acc = alpha * acc + jnp.dot(vb, p, preferred_element_type=F32)
                    out.append((m_new, acc))
            return tuple(out)

        acc0 = jnp.zeros((MLA_V + DEN_ROWS, tq), F32)
        carry = tuple(acc0 if bounded else (jnp.full((1, tq), NEG, F32), acc0)
                      for _ in range(heads))
        group = nsub if bounded else 1

        def scores_n(j):
            return tuple(scores(jnp.maximum(j - g, 0), None) for g in range(group))

        ss = scores(base + nsub - 1, nsub - 1)
        for d in range(nsub - 1, -1, -1):
            s_ahead = scores(base + d - 1, d - 1) if d > 0 else scores_n(base - 1)
            carry = update(base + d, ss, carry)
            ss = s_ahead

        def body(jj, state):
            carry, ss = state
            j = base - 1 - group * jj
            s_ahead = scores_n(j - group)
            for g in range(group):
                carry = update(j - g, ss[g], carry)
            return carry, s_ahead

        carry, _ = lax.fori_loop(0, base // group, body, (carry, ss))
        for hh in range(heads):
            acc = carry[hh] if bounded else carry[hh][1]
            o_ref[hh * MLA_V:(hh + 1) * MLA_V, :] = (
                acc[:MLA_V] / acc[MLA_V:MLA_V + 1]).astype(o_ref.dtype)

    safe = bound_ref[0] <= MLA_SAFE_BOUND
    pl.when(safe)(functools.partial(sweep, True))
    pl.when(jnp.logical_not(safe))(functools.partial(sweep, False))


def _mla_attention(bound, q, k, vt):
    b, s, _ = q.shape
    tq = 2 * TKV
    heads = 4
    return pl.pallas_call(
        functools.partial(_mla_kernel, tq=tq, heads=heads),
        out_shape=jax.ShapeDtypeStruct((b, D_MLA, s), BF16),
        grid=(b, MLA_HEADS // heads, s // tq),
        in_specs=[pl.BlockSpec(memory_space=pltpu.SMEM),
                  pl.BlockSpec((None, tq, heads * LANES), lambda bi, p, i: (bi, i, p)),
                  pl.BlockSpec((None, s, heads * LANES), lambda bi, p, i: (bi, 0, p)),
                  pl.BlockSpec((None, s // TKV, heads * MLA_V, TKV),
                               lambda bi, p, i: (bi, 0, p, 0))],
        out_specs=pl.BlockSpec((None, heads * MLA_V, tq), lambda bi, p, i: (bi, p, i)),
        compiler_params=_cparams(("parallel", "parallel", "arbitrary")),
        name="mla_attn",
    )(bound, q, k, vt)


def _mla_score_bound(g_q, g_k):
    b = 1.02 * MLA_QK * (MLA_QK ** -0.5 * LOG2E) * jnp.max(jnp.abs(g_q)) * jnp.max(jnp.abs(g_k))
    return b.astype(F32).reshape(1)


CK_QB = 2 * CHUNK
CK_KEYS = CK_WINDOW + CHUNK
CK_PAD = CK_LEFT_CHUNKS * CHUNK


def _ck_kernel(width_ref, q_ref, k_ref, vt_ref, bias_ref, o_ref, *, n_blocks, heads):
    lo = lax.broadcasted_iota(jnp.int32, (CK_QB, LANES), 1) < HEAD_DIM
    krow = lax.broadcasted_iota(jnp.int32, (CK_KEYS, CK_QB), 0)
    n_slabs = CK_KEYS // CK_QB
    ones = jnp.ones((DEN_ROWS, CK_KEYS), BF16)

    def run(bounded):
        def scores(c2, masked):
            r0 = pl.multiple_of(c2 * CK_QB, CK_QB)
            if masked:
                valid = krow >= CK_PAD - c2 * CK_QB
            out = []
            for pb in range(heads // 2):
                q = q_ref[pl.ds(r0, CK_QB), pb * LANES:(pb + 1) * LANES]
                kw = k_ref[pl.ds(r0, CK_KEYS), pb * LANES:(pb + 1) * LANES]
                zq = jnp.zeros_like(q)
                for hl in range(2):
                    qh = jnp.where(lo, q, zq) if hl == 0 else jnp.where(lo, zq, q)
                    s = (lax.dot_general(kw, qh, NT_DIMS, preferred_element_type=F32)
                         + bias_ref[2 * pb + hl])
                    if masked:
                        s = jnp.where(valid, s, NEG)
                    if bounded:
                        out.append(jnp.exp2(s).astype(BF16))
                    else:
                        out.append((s, jnp.max(s, axis=0, keepdims=True)))
            return tuple(out)

        def finish(c2, ss):
            for hh in range(heads):
                p = ss[hh] if bounded else jnp.exp2(ss[hh][0] - ss[hh][1]).astype(BF16)
                vw = jnp.concatenate(
                    [jnp.concatenate([vt_ref[c2 + sb, hh * HEAD_DIM:(hh + 1) * HEAD_DIM, :]
                                      for sb in range(n_slabs)], axis=1), ones], axis=0)
                acc = jnp.dot(vw, p, preferred_element_type=F32)
                o_ref[c2, hh * HEAD_DIM:(hh + 1) * HEAD_DIM, :] = (
                    acc[:HEAD_DIM] / acc[HEAD_DIM:HEAD_DIM + 1]).astype(o_ref.dtype)

        def sweep(lo_blk, hi_blk, masked):
            def body(c2, ss):
                ahead = scores(jnp.minimum(c2 + 1, hi_blk - 1), masked)
                finish(c2, ss)
                return ahead

            lax.fori_loop(lo_blk, hi_blk, body, scores(lo_blk, masked))

        n_masked = CK_PAD // CK_QB
        sweep(0, n_masked, True)
        sweep(n_masked, n_blocks, False)

    safe = width_ref[0] <= 2.0 * MLA_SAFE_BOUND
    pl.when(safe)(functools.partial(run, True))
    pl.when(jnp.logical_not(safe))(functools.partial(run, False))


def _ck_attention(width, q, kpad, vtpad, bias):
    b, s, d = q.shape
    sp = kpad.shape[1]
    nb = s // CK_QB
    heads = d // HEAD_DIM
    kern = functools.partial(_ck_kernel, n_blocks=nb, heads=heads)
    return pl.pallas_call(
        kern,
        out_shape=jax.ShapeDtypeStruct((b, nb, d, CK_QB), BF16),
        grid=(b,),
        in_specs=[pl.BlockSpec(memory_space=pltpu.SMEM),
                  pl.BlockSpec((None, s, d), lambda bi: (bi, 0, 0)),
                  pl.BlockSpec((None, sp, d), lambda bi: (bi, 0, 0)),
                  pl.BlockSpec((None, sp // CK_QB, d, CK_QB), lambda bi: (bi, 0, 0, 0)),
                  pl.BlockSpec((heads, CK_KEYS, CK_QB), lambda bi: (0, 0, 0))],
        out_specs=pl.BlockSpec((None, nb, d, CK_QB), lambda bi: (bi, 0, 0, 0)),
        compiler_params=_cparams(("parallel",)),
        name="ck_attn",
    )(width, q, kpad, vtpad, bias)


def _outproj_kernel(osb_ref, omla_ref, ock_ref, x_ref, mod_ref, gn_ref, wout_ref, nffn_ref,
                    *rest, moe):
    if moe:
        wr_ref, xo_ref, h2_ref, gates_ref = rest
    else:
        xo_ref, h2_ref = rest
    gn = gn_ref[...]

    def gnorm_t(ot, c0, w):
        ot = ot.astype(F32)
        ms = jnp.sum(ot * ot, axis=0, keepdims=True) * (1.0 / w)
        return ((ot * lax.rsqrt(ms + EPS)).T * gn[:, c0:c0 + w]).astype(BF16)

    ock = jnp.concatenate([ock_ref[sb] for sb in range(ock_ref.shape[0])], axis=1)
    merged = jnp.concatenate([gnorm_t(osb_ref[...], 0, D_SB), gnorm_t(omla_ref[...], D_SB, D_MLA),
                              gnorm_t(ock, D_SB + D_MLA, D_CK)], axis=-1)
    y = jnp.dot(merged, wout_ref[...], preferred_element_type=F32)
    xn = x_ref[...] + mod_ref[2:3, :] * y
    xo_ref[...] = xn
    h2 = _rms(xn, nffn_ref[...], D_MODEL) * (1.0 + mod_ref[4:5, :]) + mod_ref[3:4, :]
    h2_ref[...] = h2.astype(h2_ref.dtype)
    if moe:
        wr = wr_ref[...]
        w_hi = wr.astype(BF16)
        w_lo = (wr - w_hi.astype(F32)).astype(BF16)
        h_hi = h2.astype(BF16)
        h_lo = (h2 - h_hi.astype(F32)).astype(BF16)
        logits = (jnp.dot(h_hi, w_hi, preferred_element_type=F32)
                  + (jnp.dot(h_hi, w_lo, preferred_element_type=F32)
                     + jnp.dot(h_lo, w_hi, preferred_element_type=F32)))
        lane = lax.broadcasted_iota(jnp.int32, logits.shape, 1).astype(F32)
        logits = jnp.where(lane < N_EXPERTS, logits, -jnp.inf)
        m1 = jnp.max(logits, axis=-1, keepdims=True)
        i1 = jnp.min(jnp.where(logits == m1, lane, float(LANES)), axis=-1, keepdims=True)
        rest_l = jnp.where(lane == i1, -jnp.inf, logits)
        m2 = jnp.max(rest_l, axis=-1, keepdims=True)
        i2 = jnp.min(jnp.where(rest_l == m2, lane, float(LANES)), axis=-1, keepdims=True)
        e2 = jnp.exp(m2 - m1)
        w1 = 1.0 / (1.0 + e2)
        w2 = e2 / (1.0 + e2)
        gates_ref[...] = jnp.where(lane == 0.0, i1, jnp.where(lane == 1.0, i2,
                                   jnp.where(lane == 2.0, w1, jnp.where(lane == 3.0, w2, 0.0))))


def _outproj(osb, omla, ock, x2, mod_l, gn, wout, nffn, wr, seq):
    t = x2.shape[0]
    tm = 512
    tpb = seq // tm
    moe = wr is not None
    full = lambda a: pl.BlockSpec(a.shape, lambda i: (0,) * a.ndim)
    row = lambda w: pl.BlockSpec((tm, w), lambda i: (i, 0))
    colt = lambda w: pl.BlockSpec((None, w, tm), lambda i: (i // tpb, 0, i % tpb))
    ins = [osb, omla, ock, x2, mod_l, gn, wout, nffn]
    slabs = pl.BlockSpec((None, tm // CK_QB, D_CK, CK_QB), lambda i: (i // tpb, i % tpb, 0, 0))
    in_specs = [colt(D_SB), colt(D_MLA), slabs, row(D_MODEL),
                pl.BlockSpec((None, 6, D_MODEL), lambda i: (i // tpb, 0, 0)),
                full(gn), full(wout), full(nffn)]
    out_shape = [jax.ShapeDtypeStruct((t, D_MODEL), F32),
                 jax.ShapeDtypeStruct((t, D_MODEL), F32 if moe else BF16)]
    out_specs = [row(D_MODEL), row(D_MODEL)]
    if moe:
        ins.append(wr)
        in_specs.append(full(wr))
        out_shape.append(jax.ShapeDtypeStruct((t, LANES), F32))
        out_specs.append(row(LANES))
    return pl.pallas_call(
        functools.partial(_outproj_kernel, moe=moe),
        out_shape=tuple(out_shape),
        grid=(t // tm,),
        in_specs=in_specs,
        out_specs=tuple(out_specs),
        compiler_params=_cparams(("parallel",)),
        name="outproj_moe" if moe else "outproj",
    )(*ins)


FFN_TM = 512
MXU_DEPTH = 256
FFN_CHUNKS = ((0, 5 * MXU_DEPTH), (5 * MXU_DEPTH, 6 * MXU_DEPTH))
RESIDENT = pl.Buffered(1)


def _swiglu(h, wg_ref, wu_ref, wd_ref):
    y = None
    for c0, w in FFN_CHUNKS:
        g = jnp.dot(h, wg_ref[:, c0:c0 + w], preferred_element_type=F32)
        u = jnp.dot(h, wu_ref[:, c0:c0 + w], preferred_element_type=F32)
        a = (g * jax.nn.sigmoid(g) * u).astype(BF16)
        part = jnp.dot(a, wd_ref[c0:c0 + w, :], preferred_element_type=F32)
        y = part if y is None else y + part
    return y


def _ffn_kernel(h_ref, x_ref, mod_ref, wg_ref, wu_ref, wd_ref, o_ref):
    o_ref[...] = x_ref[...] + mod_ref[5:6, :] * _swiglu(h_ref[...], wg_ref, wu_ref, wd_ref)


def _ffn(h2, x2, mod_l, wg, wu, wd, seq):
    t = h2.shape[0]
    tm = FFN_TM
    tpb = seq // tm
    rmap = lambda i: (i, 0)
    const = lambda a: pl.BlockSpec(a.shape, lambda i: (0, 0), pipeline_mode=RESIDENT)
    return pl.pallas_call(
        _ffn_kernel,
        out_shape=jax.ShapeDtypeStruct((t, D_MODEL), F32),
        grid=(t // tm,),
        in_specs=[pl.BlockSpec((tm, D_MODEL), rmap), pl.BlockSpec((tm, D_MODEL), rmap),
                  pl.BlockSpec((None, 6, D_MODEL), lambda i: (i // tpb, 0, 0)),
                  const(wg), const(wu), const(wd)],
        out_specs=pl.BlockSpec((tm, D_MODEL), rmap),
        compiler_params=_cparams(("parallel",)),
        name="dense_ffn",
    )(h2, x2, mod_l, wg, wu, wd)


ROUTE_TM = 512
MOVE_TM = 2048


def _rank_kernel(route_ref, rank_ref, count_ref, base_ref):
    i = pl.program_id(0)

    @pl.when(i == 0)
    def _():
        base_ref[...] = jnp.zeros_like(base_ref)

    tm = route_ref.shape[0]
    route = route_ref[...]
    lane = lax.broadcasted_iota(jnp.int32, (tm, LANES), 1).astype(F32)
    sel1 = lane == route[:, 0:1]
    sel2 = lane == route[:, 1:2]
    hot = jnp.where(sel1 | sel2, 1.0, 0.0)
    r = lax.broadcasted_iota(jnp.int32, (tm, tm), 0)
    c = lax.broadcasted_iota(jnp.int32, (tm, tm), 1)
    before = jnp.where(c < r, 1.0, 0.0).astype(BF16)
    seen = jnp.dot(before, hot.astype(BF16), preferred_element_type=F32) + base_ref[...]
    r1 = jnp.sum(jnp.where(sel1, seen, 0.0), axis=-1, keepdims=True)
    r2 = jnp.sum(jnp.where(sel2, seen, 0.0), axis=-1, keepdims=True)
    rank_ref[...] = jnp.where(lane == 0.0, r1, jnp.where(lane == 1.0, r2, 0.0))
    base_ref[...] += jnp.sum(hot, axis=0, keepdims=True)
    count_ref[...] = base_ref[...]


def _moe_rank(route):
    t = route.shape[0]
    tm = ROUTE_TM
    return pl.pallas_call(
        _rank_kernel,
        out_shape=(jax.ShapeDtypeStruct((t, LANES), F32), jax.ShapeDtypeStruct((1, LANES), F32)),
        grid=(t // tm,),
        in_specs=[pl.BlockSpec((tm, LANES), lambda i: (i, 0))],
        out_specs=(pl.BlockSpec((tm, LANES), lambda i: (i, 0)),
                   pl.BlockSpec((1, LANES), lambda i: (0, 0))),
        scratch_shapes=[pltpu.VMEM((1, LANES), F32)],
        compiler_params=_cparams(("arbitrary",)),
        name="moe_rank",
    )(route)


def _dispatch_kernel(pos_ref, h_ref, xs_in_ref, xs_ref, sem):
    del xs_in_ref
    td = pos_ref.shape[1]

    def issue(r, carry):
        src = h_ref.at[pl.ds(r, 1), :]
        for k in range(2):
            pltpu.make_async_copy(src, xs_ref.at[pl.ds(pos_ref[k, r], 1), :],
                                  sem.at[k]).start(priority=k)
        return carry

    lax.fori_loop(0, td, issue, 0, unroll=8)
    for k in range(2):
        pltpu.make_async_copy(h_ref, xs_ref.at[pl.ds(0, td), :], sem.at[k]).wait()


def _moe_dispatch(pos, h2, n_rows):
    nt, _, td = pos.shape
    xs0 = jnp.zeros((n_rows, D_MODEL), F32)
    return pl.pallas_call(
        _dispatch_kernel,
        out_shape=jax.ShapeDtypeStruct((n_rows, D_MODEL), F32),
        grid=(nt,),
        in_specs=[pl.BlockSpec((None, 2, td), lambda i: (i, 0, 0), memory_space=pltpu.SMEM),
                  pl.BlockSpec((td, D_MODEL), lambda i: (i, 0)),
                  pl.BlockSpec(memory_space=pl.ANY)],
        out_specs=pl.BlockSpec(memory_space=pl.ANY),
        scratch_shapes=[pltpu.SemaphoreType.DMA((2,))],
        input_output_aliases={2: 0},
        compiler_params=_cparams(("arbitrary",)),
        name="moe_dispatch",
    )(pos, h2, xs0)


def _group_ffn_kernel(te_ref, nu_ref, xs_ref, wg_ref, wu_ref, wd_ref, y_ref):
    del te_ref
    i = pl.program_id(0)

    @pl.when(i < nu_ref[0])
    def _():
        y_ref[...] = _swiglu(xs_ref[...].astype(BF16), wg_ref, wu_ref, wd_ref)

    @pl.when(i >= nu_ref[0])
    def _():
        y_ref[...] = jnp.zeros_like(y_ref)


def _moe_group_ffn(tile_expert, n_used, xs, wg, wu, wd):
    n_rows = xs.shape[0]
    tm = FFN_TM
    row = lambda i, te, nu: (jnp.minimum(i, nu[0] - 1), 0)
    wspec = lambda a: pl.BlockSpec((None,) + a.shape[1:], lambda i, te, nu: (te[i], 0, 0))
    return pl.pallas_call(
        _group_ffn_kernel,
        out_shape=jax.ShapeDtypeStruct((n_rows, D_MODEL), F32),
        grid_spec=pltpu.PrefetchScalarGridSpec(
            num_scalar_prefetch=2,
            grid=(n_rows // tm,),
            in_specs=[pl.BlockSpec((tm, D_MODEL), row), wspec(wg), wspec(wu), wspec(wd)],
            out_specs=pl.BlockSpec((tm, D_MODEL), lambda i, te, nu: (i, 0))),
        compiler_params=_cparams(("arbitrary",)),
        name="moe_group_ffn",
    )(tile_expert, n_used, xs, wg, wu, wd)


def _combine_kernel(pos_ref, route_ref, x_ref, mod_ref, y_ref, o_ref, buf_ref, sem):
    tc = pos_ref.shape[1]

    def issue(r, carry):
        for k in range(2):
            pltpu.make_async_copy(y_ref.at[pl.ds(pos_ref[k, r], 1), :],
                                  buf_ref.at[k, pl.ds(r, 1), :], sem.at[k]).start(priority=k)
        return carry

    lax.fori_loop(0, tc, issue, 0, unroll=8)
    for k in range(2):
        pltpu.make_async_copy(y_ref.at[pl.ds(0, tc), :], buf_ref.at[k], sem.at[k]).wait()
    route = route_ref[...]
    y = route[:, 2:3] * buf_ref[0] + route[:, 3:4] * buf_ref[1]
    o_ref[...] = x_ref[...] + mod_ref[5:6, :] * y


def _moe_combine(pos, route, x2, mod_l, y, seq):
    nt, _, tc = pos.shape
    t = x2.shape[0]
    tpb = seq // tc
    return pl.pallas_call(
        _combine_kernel,
        out_shape=jax.ShapeDtypeStruct((t, D_MODEL), F32),
        grid=(nt,),
        in_specs=[pl.BlockSpec((None, 2, tc), lambda i: (i, 0, 0), memory_space=pltpu.SMEM),
                  pl.BlockSpec((tc, LANES), lambda i: (i, 0)),
                  pl.BlockSpec((tc, D_MODEL), lambda i: (i, 0)),
                  pl.BlockSpec((None, 6, D_MODEL), lambda i: (i // tpb, 0, 0)),
                  pl.BlockSpec(memory_space=pl.ANY)],
        out_specs=pl.BlockSpec((tc, D_MODEL), lambda i: (i, 0)),
        scratch_shapes=[pltpu.VMEM((2, tc, D_MODEL), F32), pltpu.SemaphoreType.DMA((2,))],
        compiler_params=_cparams(("arbitrary",)),
        name="moe_combine",
    )(pos, route, x2, mod_l, y)


def _moe(h2, x2, mod_l, route, wg, wu, wd, seq):
    t = h2.shape[0]
    ne = wg.shape[0]
    tm = FFN_TM
    n_tiles = (2 * t) // tm + ne
    n_rows = n_tiles * tm
    rank, count = _moe_rank(route)
    counts = count[0, :ne].astype(jnp.int32)
    tiles_per = (counts + tm - 1) // tm
    tile_end = jnp.cumsum(tiles_per)
    start = (tile_end - tiles_per) * tm
    n_used = tile_end[-1:]
    tile_ids = jnp.minimum(jnp.arange(n_tiles, dtype=jnp.int32), n_used[0] - 1)
    tile_expert = jnp.sum((tile_ids[:, None] >= tile_end[None, :]).astype(jnp.int32), axis=1)
    experts = route[:, 0:2].astype(jnp.int32)
    pos = jnp.take(start, experts) + rank[:, 0:2].astype(jnp.int32)
    pos = pos.reshape(t // MOVE_TM, MOVE_TM, 2).transpose(0, 2, 1)
    xs = _moe_dispatch(pos, h2, n_rows)
    y = _moe_group_ffn(tile_expert, n_used, xs, wg, wu, wd)
    return _moe_combine(pos, route, x2, mod_l, y, seq)


def _rot_half_cols(w):
    half = w.shape[-1] // 2
    return jnp.concatenate([-w[..., half:], w[..., :half]], axis=-1)


def _pad_cols(w, left, total):
    return jnp.pad(w, ((0, 0), (left, total - left - w.shape[-1])))


def _layout_w_in(w_in):
    sizes = (D_SB, D_SB, D_SB, MLA_Q_RANK, MLA_KV_RANK, MLA_ROPE, D_CK, D_CK, D_CK)
    splits = [int(v) for v in np.cumsum(sizes)[:-1]]
    sbq, sbk, sbv, cq, ckv, kpe, ckq, ckk, ckv2 = jnp.split(w_in, splits, axis=-1)
    kpa = _pad_cols(kpe, MLA_NOPE, LANES)
    kpb = _pad_cols(_rot_half_cols(kpe), MLA_NOPE, LANES)
    main = jnp.concatenate([sbq, sbk, cq, ckv, kpa, kpb, ckq, ckk], axis=-1)
    return main.astype(BF16), jnp.concatenate([sbv, ckv2], axis=-1).T.astype(BF16)


def _layout_w_q_up(w):
    r = w.shape[0]
    w3 = w.reshape(r, MLA_HEADS, MLA_QK)
    nope, pe = w3[..., :MLA_NOPE], w3[..., MLA_NOPE:]
    zpad = jnp.zeros((r, MLA_HEADS, LANES - MLA_QK), w.dtype)
    a = jnp.concatenate([nope, pe, zpad], axis=-1).reshape(r, MLA_HEADS * LANES)
    b = jnp.concatenate([jnp.zeros_like(nope), _rot_half_cols(pe), zpad], axis=-1)
    return jnp.concatenate([a, b.reshape(r, MLA_HEADS * LANES)], axis=-1).astype(BF16)


def _layout_w_kv_up(w):
    r = w.shape[0]
    w3 = w.reshape(r, MLA_HEADS, MLA_NOPE + MLA_V)
    kn = jnp.pad(w3[..., :MLA_NOPE], ((0, 0), (0, 0), (0, LANES - MLA_NOPE)))
    return (kn.reshape(r, MLA_HEADS * LANES).astype(BF16),
            w3[..., MLA_NOPE:].reshape(r, D_MLA).T.astype(BF16))


def _ck_bias_table(rel_bias, g_q, g_k):
    qk = (1.02 * HEAD_DIM * (HEAD_DIM ** -0.5 * LOG2E)
          * jnp.max(jnp.abs(g_q)) * jnp.max(jnp.abs(g_k)))
    b_hi = jnp.max(rel_bias) * LOG2E
    b_lo = jnp.min(rel_bias) * LOG2E
    width = (2.0 * qk + (b_hi - b_lo)).astype(F32).reshape(1)
    rel_bias = rel_bias.astype(F32) - (qk + b_hi) / LOG2E
    heads = rel_bias.shape[0]
    n_g = CHUNK + CK_KEYS - 1
    blocks = []
    for u in range(CK_QB // CHUNK):
        d = np.arange(n_g)
        idx = np.clip(d - (CK_KEYS - 1) + CK_PAD + CHUNK * u, -REL_CLIP, REL_CLIP) + REL_CLIP
        g = rel_bias.astype(F32)[:, idx] * LOG2E
        hank = jnp.tile(g, (1, CHUNK + 1))[:, :CHUNK * (n_g + 1)]
        hank = hank.reshape(heads, CHUNK, n_g + 1)[:, :, :CK_KEYS]
        blk = jnp.transpose(hank[:, :, ::-1], (0, 2, 1))
        j = np.arange(CK_KEYS) - CHUNK * u
        in_band = ((j >= 0) & (j < CK_WINDOW))[None, :, None]
        blocks.append(jnp.where(in_band, blk, NEG))
    return jnp.concatenate(blocks, axis=-1), width


def kernel(x, c, positions, ada_w, ada_b, norm_mix, norm_ffn, w_in, mla_q_norm, w_q_up, mla_kv_norm, w_kv_up, mla_q_qknorm, mla_k_qknorm, ck_q_qknorm, ck_k_qknorm, ck_rel_bias, group_out_norm, w_out, ffn_w_gate, ffn_w_up, ffn_w_down, moe_router, moe_w_gate, moe_w_up, moe_w_down):
    bsz, seq, d = x.shape
    depth = ada_w.shape[0]
    t = bsz * seq
    x2 = x.reshape(t, d)

    mod = _ada(c, ada_w, ada_b).reshape(depth, bsz, 6, d)
    rc, rs = _rope_tables(positions)
    pad128 = lambda g: jnp.pad(g, (0, LANES - g.shape[0])).reshape(1, LANES)
    pair = lambda g: jnp.concatenate([g, g]).reshape(1, LANES)

    for layer in range(depth):
        win, wsbvt = _layout_w_in(w_in[layer])
        wq = _layout_w_q_up(w_q_up[layer])
        wk, wvt = _layout_w_kv_up(w_kv_up[layer])
        sbq, sbk, sbvt, mq, mk, mvt, cq, ck, cvt = _inproj(
            x2, mod[layer], norm_mix[layer].reshape(1, d), win, wsbvt, rc, rs,
            mla_q_norm[layer].reshape(1, -1), wq, mla_kv_norm[layer].reshape(1, -1), wk, wvt,
            pad128(mla_q_qknorm[layer]), pad128(mla_k_qknorm[layer]),
            pair(ck_q_qknorm[layer]), pair(ck_k_qknorm[layer]), seq)

        r3 = lambda a: a.reshape(bsz, seq, a.shape[-1])
        o_sb = _sb_attention(r3(sbq), r3(sbk), sbvt)
        o_mla = _mla_attention(_mla_score_bound(mla_q_qknorm[layer], mla_k_qknorm[layer]),
                               r3(mq), r3(mk), mvt)
        kpad = jnp.pad(r3(ck), ((0, 0), (CK_PAD, 0), (0, 0)))
        vtpad = jnp.pad(cvt, ((0, 0), (CK_PAD // CK_QB, 0), (0, 0), (0, 0)))
        ck_bias, ck_width = _ck_bias_table(ck_rel_bias[layer], ck_q_qknorm[layer],
                                           ck_k_qknorm[layer])
        o_ck = _ck_attention(ck_width, r3(cq), kpad, vtpad, ck_bias)

        i = layer // 2
        is_moe = layer % 2 == 1
        wr = None
        if is_moe:
            wr = jnp.pad(moe_router[i], ((0, 0), (0, LANES - N_EXPERTS)))
        res = _outproj(o_sb, o_mla, o_ck, x2,
                       mod[layer], group_out_norm[layer].reshape(1, -1),
                       w_out[layer].astype(BF16), norm_ffn[layer].reshape(1, d), wr, seq)
        if is_moe:
            x2, h2, route = res
            x2 = _moe(h2, x2, mod[layer], route, moe_w_gate[i].astype(BF16),
                      moe_w_up[i].astype(BF16), moe_w_down[i].astype(BF16), seq)
        else:
            x2, h2 = res
            x2 = _ffn(h2, x2, mod[layer], ffn_w_gate[i].astype(BF16),
                      ffn_w_up[i].astype(BF16), ffn_w_down[i].astype(BF16), seq)
    return x2.reshape(bsz, seq, d)
```
